```python
import math
import jax
import jax.numpy as jnp
from jax import lax
import numpy as np

D_MODEL = 1024
BATCH = 4
SEQ = 4096
DEPTH = 4
DEC_BATCH = 8
DEC_SEQ = 64
PAST_LEN = 1024

CHUNK = 64
N_ATT_HEADS = 8
HEAD_DIM = 64
D_ATT = N_ATT_HEADS * HEAD_DIM
D_SSM = D_MODEL - D_ATT
D_MIX = D_ATT + D_SSM
SSM_GROUP = 16
N_SSM_GROUPS = D_SSM // SSM_GROUP
SSM_STATE = 64
D_FF = ((8 * D_MODEL + 3 * 256 - 1) // (3 * 256)) * 256
D_IN = 3 * D_ATT + N_ATT_HEADS + D_SSM
Q_BLOCK = 128
FORGET_BIAS_INIT = 2.0
EPS = 1e-6

kernel_name = "fox_s5_hybrid_stream_step"


def _rmsnorm(x, g):
    x32 = x.astype(jnp.float32)
    y = x32 * lax.rsqrt(jnp.mean(x32 * x32, axis=-1, keepdims=True) + EPS) * g.astype(jnp.float32)
    return y.astype(x.dtype)


def _heads(t):
    b, n, _ = t.shape
    return t.reshape(b, n, N_ATT_HEADS, HEAD_DIM).transpose(0, 2, 1, 3)


def _fox_attention(q, k, v, f_q, f_k, q_pos, k_pos):
    s = jnp.einsum("bhqd,bhkd->bhqk", q, k).astype(jnp.float32) * (HEAD_DIM ** -0.5)
    s = s + f_q[..., :, None] - f_k[..., None, :]
    s = jnp.where(k_pos[None, :] <= q_pos[:, None], s, -jnp.inf)
    p = jax.nn.softmax(s, axis=-1)
    return jnp.einsum("bhqk,bhkd->bhqd", p.astype(v.dtype), v)


def _fox_prompt(q, k, v, logf):
    b, h, s, d = q.shape
    nb = s // Q_BLOCK
    f_cum = jnp.cumsum(logf, axis=-1)
    pos = jnp.arange(s)
    qb = q.reshape(b, h, nb, Q_BLOCK, d).transpose(2, 0, 1, 3, 4)
    fb = f_cum.reshape(b, h, nb, Q_BLOCK).transpose(2, 0, 1, 3)
    pb = pos.reshape(nb, Q_BLOCK)
    out = lax.map(lambda a: _fox_attention(a[0], k, v, a[1], f_cum, a[2], pos), (qb, fb, pb))
    return out.transpose(1, 2, 0, 3, 4).reshape(b, h, s, d)


def _ssm_combine(left, right):
    a_l, b_l = left
    a_r, b_r = right
    return (a_l * a_r, a_r * b_l + b_r)


def _s5_mix(u, p, h0):
    bsz, t, _ = u.shape
    f32 = jnp.float32
    u32 = u.astype(f32).reshape(bsz, t, N_SSM_GROUPS, SSM_GROUP)
    lam = lax.complex(p["ssm_a_re"].astype(f32), p["ssm_a_im"].astype(f32))
    dt = jnp.exp(p["ssm_log_dt"].astype(f32))[:, None]
    a_bar = jnp.exp(lam * dt)
    b_c = lax.complex(p["ssm_b_re"].astype(f32), p["ssm_b_im"].astype(f32))
    b_bar = ((a_bar - 1.0) / lam)[..., None] * b_c
    bu = jnp.einsum("btgm,gpm->btgp", u32.astype(jnp.complex64), b_bar)
    if h0 is not None:
        bu = bu.at[:, 0].add(a_bar[None] * h0)
    a_seq = jnp.broadcast_to(a_bar, bu.shape)
    _, h = lax.associative_scan(_ssm_combine, (a_seq, bu), axis=1)
    c_c = lax.complex(p["ssm_c_re"].astype(f32), p["ssm_c_im"].astype(f32))
    y = jnp.einsum("btgp,gmp->btgm", h, c_c).real
    y = y + p["ssm_d"].astype(f32).reshape(N_SSM_GROUPS, SSM_GROUP) * u32
    z = jax.nn.gelu(y.reshape(bsz, t, D_SSM))
    out = z * jax.nn.sigmoid(z @ p["w_glu"].astype(f32))
    return out.astype(u.dtype), h[:, -1]


def _layer(x, c, p, past):
    bsz, t, _ = x.shape
    mod = jax.nn.silu(c) @ p["w_ada"] + p["b_ada"]
    sh1, sc1, g1, sh2, sc2, g2 = jnp.split(mod[:, None, :], 6, axis=-1)
    hm = _rmsnorm(x, p["g_pre_mix"]) * (1.0 + sc1) + sh1
    proj = hm @ p["w_in"]
    q = _heads(proj[..., :D_ATT])
    k = _heads(proj[..., D_ATT:2 * D_ATT])
    v = _heads(proj[..., 2 * D_ATT:3 * D_ATT])
    gate_logit = proj[..., 3 * D_ATT:3 * D_ATT + N_ATT_HEADS].astype(jnp.float32)
    logf = jax.nn.log_sigmoid(gate_logit + p["b_forget"].astype(jnp.float32)).transpose(0, 2, 1)
    u = proj[..., 3 * D_ATT + N_ATT_HEADS:]
    if past is None:
        att = _fox_prompt(q, k, v, logf)
        h0 = None
    else:
        k_past, v_past, logf_past, h0_re, h0_im = past
        n_past = k_past.shape[2]
        k_all = jnp.concatenate([k_past.astype(k.dtype), k], axis=2)
        v_all = jnp.concatenate([v_past.astype(v.dtype), v], axis=2)
        f_cum = jnp.cumsum(jnp.concatenate([logf_past.astype(jnp.float32), logf], axis=-1), axis=-1)
        att = _fox_attention(q, k_all, v_all, f_cum[..., n_past:], f_cum,
                             n_past + jnp.arange(t), jnp.arange(n_past + t))
        h0 = lax.complex(h0_re.astype(jnp.float32), h0_im.astype(jnp.float32))
    ssm_out, h_last = _s5_mix(u, p, h0)
    mix = jnp.concatenate([att.transpose(0, 2, 1, 3).reshape(bsz, t, D_ATT), ssm_out], axis=-1)
    x = x + g1 * _rmsnorm(mix @ p["w_out"], p["g_post_mix"])
    hf = _rmsnorm(x, p["g_pre_ffn"]) * (1.0 + sc2) + sh2
    f = (jax.nn.silu(hf @ p["w_gate"]) * (hf @ p["w_up"])) @ p["w_down"]
    x = x + g2 * _rmsnorm(f, p["g_post_ffn"])
    return x, (k, v, logf, h_last.real, h_last.imag)


def setup_inputs(seed: int = 0) -> dict:
    key = jax.random.key(seed)
    ks = jax.random.split(key, 32)
    f32 = jnp.float32

    def nrm(k, shape, scale):
        return jax.random.normal(k, shape, f32) * scale

    n_idx = jnp.arange(SSM_STATE, dtype=f32)
    G, P, M = N_SSM_GROUPS, SSM_STATE, SSM_GROUP
    return {
        "x_prompt": nrm(ks[0], (BATCH, SEQ, D_MODEL), 1.0),
        "x_sample": nrm(ks[1], (DEC_BATCH, DEC_SEQ, D_MODEL), 1.0),
        "c_prompt": nrm(ks[2], (BATCH, D_MODEL), 1.0),
        "c_sample": nrm(ks[3], (DEC_BATCH, D_MODEL), 1.0),
        "cache_k": nrm(ks[4], (DEPTH, DEC_BATCH, N_ATT_HEADS, PAST_LEN, HEAD_DIM), 1.0),
        "cache_v": nrm(ks[5], (DEPTH, DEC_BATCH, N_ATT_HEADS, PAST_LEN, HEAD_DIM), 1.0),
        "cache_logf": jax.nn.log_sigmoid(FORGET_BIAS_INIT + nrm(ks[6], (DEPTH, DEC_BATCH, N_ATT_HEADS, PAST_LEN), 0.5)),
        "state_ssm_re": nrm(ks[7], (DEPTH, DEC_BATCH, G, P), 0.5),
        "state_ssm_im": nrm(ks[8], (DEPTH, DEC_BATCH, G, P), 0.5),
        "w_ada": nrm(ks[9], (DEPTH, D_MODEL, 6 * D_MODEL), D_MODEL ** -0.5),
        "b_ada": nrm(ks[10], (DEPTH, 6 * D_MODEL), 0.02),
        "g_pre_mix": 1.0 + nrm(ks[11], (DEPTH, D_MODEL), 0.02),
        "g_post_mix": 1.0 + nrm(ks[12], (DEPTH, D_MODEL), 0.02),
        "g_pre_ffn": 1.0 + nrm(ks[13], (DEPTH, D_MODEL), 0.02),
        "g_post_ffn": 1.0 + nrm(ks[14], (DEPTH, D_MODEL), 0.02),
        "w_in": nrm(ks[15], (DEPTH, D_MODEL, D_IN), D_MODEL ** -0.5),
        "b_forget": FORGET_BIAS_INIT + nrm(ks[16], (DEPTH, N_ATT_HEADS), 0.1),
        "ssm_a_re": -0.5 + nrm(ks[17], (DEPTH, G, P), 0.01),
        "ssm_a_im": math.pi * n_idx + nrm(ks[18], (DEPTH, G, P), 0.01),
        "ssm_log_dt": jax.random.uniform(ks[19], (DEPTH, G), f32, math.log(1e-3), math.log(1e-1)),
        "ssm_b_re": nrm(ks[20], (DEPTH, G, P, M), (2 * M) ** -0.5),
        "ssm_b_im": nrm(ks[21], (DEPTH, G, P, M), (2 * M) ** -0.5),
        "ssm_c_re": nrm(ks[22], (DEPTH, G, M, P), P ** -0.5),
        "ssm_c_im": nrm(ks[23], (DEPTH, G, M, P), P ** -0.5),
        "ssm_d": nrm(ks[24], (DEPTH, D_SSM), 1.0),
        "w_glu": nrm(ks[25], (DEPTH, D_SSM, D_SSM), D_SSM ** -0.5),
        "w_out": nrm(ks[26], (DEPTH, D_MIX, D_MODEL), D_MIX ** -0.5),
        "w_gate": nrm(ks[27], (DEPTH, D_MODEL, D_FF), D_MODEL ** -0.5),
        "w_up": nrm(ks[28], (DEPTH, D_MODEL, D_FF), D_MODEL ** -0.5),
        "w_down": nrm(ks[29], (DEPTH, D_FF, D_MODEL), D_FF ** -0.5),
    }


def reference(x_prompt, x_sample, c_prompt, c_sample, cache_k, cache_v, cache_logf,
              state_ssm_re, state_ssm_im, w_ada, b_ada, g_pre_mix, g_post_mix, g_pre_ffn,
              g_post_ffn, w_in, b_forget, ssm_a_re, ssm_a_im, ssm_log_dt, ssm_b_re, ssm_b_im,
              ssm_c_re, ssm_c_im, ssm_d, w_glu, w_out, w_gate, w_up, w_down):
    yp, ys = x_prompt, x_sample
    kp, vp, lp, rp, ip = [], [], [], [], []
    ks_, vs_, ls_, rs_, is_ = [], [], [], [], []
    for l in range(DEPTH):
        p = {
            "w_ada": w_ada[l], "b_ada": b_ada[l],
            "g_pre_mix": g_pre_mix[l], "g_post_mix": g_post_mix[l],
            "g_pre_ffn": g_pre_ffn[l], "g_post_ffn": g_post_ffn[l],
            "w_in": w_in[l], "b_forget": b_forget[l],
            "ssm_a_re": ssm_a_re[l], "ssm_a_im": ssm_a_im[l], "ssm_log_dt": ssm_log_dt[l],
            "ssm_b_re": ssm_b_re[l], "ssm_b_im": ssm_b_im[l],
            "ssm_c_re": ssm_c_re[l], "ssm_c_im": ssm_c_im[l], "ssm_d": ssm_d[l],
            "w_glu": w_glu[l], "w_out": w_out[l],
            "w_gate": w_gate[l], "w_up": w_up[l], "w_down": w_down[l],
        }
        yp, (k1, v1, l1, r1, i1) = _layer(yp, c_prompt, p, None)
        ys, (k2, v2, l2, r2, i2) = _layer(ys, c_sample, p, (cache_k[l], cache_v[l], cache_logf[l],
                                                            state_ssm_re[l], state_ssm_im[l]))
        kp.append(k1); vp.append(v1); lp.append(l1); rp.append(r1); ip.append(i1)
        ks_.append(k2); vs_.append(v2); ls_.append(l2); rs_.append(r2); is_.append(i2)
    return (yp, ys,
            jnp.stack(kp), jnp.stack(vp), jnp.stack(lp), jnp.stack(rp), jnp.stack(ip),
            jnp.stack(ks_), jnp.stack(vs_), jnp.stack(ls_), jnp.stack(rs_), jnp.stack(is_))
```

```python
import functools
import math

import numpy as np
import jax
import jax.numpy as jnp
from jax import lax
from jax.experimental import pallas as pl
from jax.experimental.pallas import tpu as pltpu

F32 = jnp.float32
BF16 = jnp.bfloat16

D_MODEL = 1024
N_HEADS = 8
HEAD_DIM = 64
D_ATT = N_HEADS * HEAD_DIM
D_SSM = D_MODEL - D_ATT
SSM_GROUP = 16
N_GROUPS = D_SSM // SSM_GROUP
SSM_STATE = 64
D_FF = 2816
EPS = 1e-6

LANES = 128
S5_CHUNK = 16
S5_ROWS = S5_CHUNK * SSM_GROUP
GATE_PAD = LANES
W_IN_COLS = 3 * D_ATT + GATE_PAD + D_SSM
U_COL0 = 3 * D_ATT + GATE_PAD
N_U_SLABS = D_SSM // LANES
SCAN_PAD = 128
VMEM_LIMIT = 56 * 1024 * 1024
MASK_VALUE = -1e30

NT_DIMS = (((1,), (1,)), ((), ()))
TN_DIMS = (((0,), (0,)), ((), ()))


def _sigmoid(x):
    return 1.0 / (1.0 + jnp.exp(-x))


def _split3(x):
    hi = x.astype(BF16)
    r = x - hi.astype(F32)
    mid = r.astype(BF16)
    lo = (r - mid.astype(F32)).astype(BF16)
    return hi, mid, lo


def _dot(a, b):
    return jnp.dot(a, b, preferred_element_type=F32)


def _rms(x, g):
    ms = jnp.mean(x * x, axis=-1, keepdims=True)
    return x * lax.rsqrt(ms + EPS) * g


def _ada_kernel(c_ref, w_ref, b_ref, o_ref):
    c = c_ref[...]
    a = c * _sigmoid(c)
    w = w_ref[0]
    a_hi = a.astype(BF16)
    a_lo = (a - a_hi.astype(F32)).astype(BF16)
    w_hi = w.astype(BF16)
    w_lo = (w - w_hi.astype(F32)).astype(BF16)
    o_ref[0] = _dot(a_hi, w_hi) + _dot(a_lo, w_hi) + _dot(a_hi, w_lo) + b_ref[0]


def _ada_call(c_all, w_ada, b_ada):
    depth = w_ada.shape[0]
    nb = 1536
    n_out = w_ada.shape[2]
    return pl.pallas_call(
        _ada_kernel,
        grid=(depth, n_out // nb),
        in_specs=[
            pl.BlockSpec((16, D_MODEL), lambda l, j: (0, 0)),
            pl.BlockSpec((1, D_MODEL, nb), lambda l, j: (l, 0, j)),
            pl.BlockSpec((1, 1, nb), lambda l, j: (l, 0, j)),
        ],
        out_specs=pl.BlockSpec((1, 16, nb), lambda l, j: (l, 0, j)),
        out_shape=jax.ShapeDtypeStruct((depth, 16, n_out), F32),
        compiler_params=pltpu.CompilerParams(
            dimension_semantics=("arbitrary", "arbitrary"), vmem_limit_bytes=VMEM_LIMIT),
        name="ada",
    )(c_all, w_ada, b_ada.reshape(depth, 1, n_out))


def _inproj_kernel(x_ref, sh_ref, sc_ref, g_ref, w_ref, bf_ref, tri_ref, sel_ref, one_ref, eye_ref,
                   qa_ref, ka_ref, v_ref, kc_ref, vc_ref, lf_ref, u_ref, carry_ref, *, tb):
    @pl.when(pl.program_id(1) == 0)
    def _():
        carry_ref[...] = jnp.zeros_like(carry_ref)

    x = x_ref[0]
    hm = _rms(x, g_ref[...]) * (1.0 + sc_ref[0]) + sh_ref[0]
    proj = _dot(hm.astype(BF16), w_ref[...])
    q = proj[:, 0:D_ATT] * (HEAD_DIM ** -0.5)
    k = proj[:, D_ATT:2 * D_ATT]
    v = proj[:, 2 * D_ATT:3 * D_ATT]
    gate = proj[:, 3 * D_ATT:U_COL0] + bf_ref[...]
    logf = jnp.minimum(gate, 0.0) - jnp.log1p(jnp.exp(-jnp.abs(gate)))

    lf_parts = _split3(logf)
    cs = _dot(tri_ref[...], jnp.concatenate(lf_parts, axis=1))
    fcum = cs[:, 0:LANES] + cs[:, LANES:2 * LANES] + cs[:, 2 * LANES:3 * LANES] + carry_ref[...]
    carry_ref[...] = fcum[tb - 1:tb, :]

    extra = _dot(jnp.concatenate(_split3(fcum), axis=1), sel_ref[...]) + one_ref[...]
    lo_half = lax.broadcasted_iota(jnp.int32, (tb, LANES), 1) < HEAD_DIM
    for h in range(N_HEADS):
        pair = slice(LANES * (h // 2), LANES * (h // 2) + LANES)
        q2, k2, v2 = q[:, pair], k[:, pair], v[:, pair]
        if h % 2:
            q2 = pltpu.roll(q2, HEAD_DIM, axis=1)
            k2 = pltpu.roll(k2, HEAD_DIM, axis=1)
            v2 = pltpu.roll(v2, HEAD_DIM, axis=1)
        qa_ref[0, h] = jnp.where(lo_half, q2, extra[:, LANES * h:LANES * (h + 1)]).astype(BF16)
        ka_ref[0, h] = jnp.where(
            lo_half, k2, extra[:, D_MODEL + LANES * h:D_MODEL + LANES * (h + 1)]).astype(BF16)
        kc_ref[h] = k2[:, 0:HEAD_DIM]
        vc_ref[h] = v2[:, 0:HEAD_DIM]
    v_ref[0] = v.astype(BF16)
    lf_ref[0] = sum(lax.dot_general(eye_ref[...], part, NT_DIMS, preferred_element_type=F32)
                    for part in lf_parts)
    for j in range(N_U_SLABS):
        u_ref[0, j] = proj[:, U_COL0 + LANES * j:U_COL0 + LANES * (j + 1)]


def _inproj_consts(tb):
    tri = np.tril(np.ones((tb, tb), np.float32))
    sel = np.zeros((3 * LANES, 2 * D_MODEL), np.float32)
    one = np.zeros((1, 2 * D_MODEL), np.float32)
    for h in range(N_HEADS):
        for p in range(3):
            sel[p * LANES + h, LANES * h + HEAD_DIM + p] = 1.0
            sel[p * LANES + h, D_MODEL + LANES * h + HEAD_DIM + 3 + p] = -1.0
            one[0, LANES * h + HEAD_DIM + 3 + p] = 1.0
            one[0, D_MODEL + LANES * h + HEAD_DIM + p] = 1.0
    eye = np.eye(N_HEADS, LANES, dtype=np.float32)
    return jnp.asarray(tri, BF16), jnp.asarray(sel, BF16), jnp.asarray(one, F32), jnp.asarray(eye, BF16)


def _mod_spec(layer, row0, chunk):
    return pl.BlockSpec((1, 1, D_MODEL), lambda b, i: ((layer * 16 + row0 + b) * 6 + chunk, 0, 0))


def _inproj_call(x, modr, layer, row0, g_pre, w_in, b_forget, kv_bufs, u_flat, tb):
    bsz, t, _ = x.shape
    nblk = t // tb
    tri, sel, one, eye = _inproj_consts(tb)
    if u_flat:
        u_shape = (1, N_U_SLABS, bsz * t, LANES)
        u_spec = pl.BlockSpec((1, N_U_SLABS, tb, LANES), lambda b, i: (0, 0, b * nblk + i, 0))
    else:
        u_shape = (bsz, N_U_SLABS, t, LANES)
        u_spec = pl.BlockSpec((1, N_U_SLABS, tb, LANES), lambda b, i: (b, 0, i, 0))
    depth = kv_bufs[0].shape[0] if kv_bufs is not None else None
    const = lambda b, i: (0, 0)
    in_specs = [
        pl.BlockSpec((1, tb, D_MODEL), lambda b, i: (b, i, 0)),
        _mod_spec(layer, row0, 0), _mod_spec(layer, row0, 1),
        pl.BlockSpec((1, D_MODEL), const),
        pl.BlockSpec((D_MODEL, W_IN_COLS), const),
        pl.BlockSpec((1, GATE_PAD), const),
        pl.BlockSpec((tb, tb), const),
        pl.BlockSpec((3 * LANES, 2 * D_MODEL), const),
        pl.BlockSpec((1, 2 * D_MODEL), const),
        pl.BlockSpec((N_HEADS, LANES), const),
    ]
    args = [x, modr, modr, g_pre, w_in, b_forget, tri, sel, one, eye]
    head_blk = pl.BlockSpec((1, N_HEADS, tb, LANES), lambda b, i: (b, 0, i, 0))
    if kv_bufs is None:
        kv_shape = jax.ShapeDtypeStruct((bsz, N_HEADS, t, HEAD_DIM), F32)
        kv_spec = pl.BlockSpec((None, N_HEADS, tb, HEAD_DIM), lambda b, i: (b, 0, i, 0))
        aliases = {}
    else:
        full = kv_bufs[0]
        kv_shape = jax.ShapeDtypeStruct(full.shape, F32)
        kv_spec = pl.BlockSpec((None, None, N_HEADS, tb, HEAD_DIM), lambda b, i: (layer, b, 0, i, 0))
        aliases = {}
        if layer > 0:
            in_specs += [pl.BlockSpec(memory_space=pl.ANY), pl.BlockSpec(memory_space=pl.ANY)]
            args += [kv_bufs[0], kv_bufs[1]]
            aliases = {len(args) - 2: 3, len(args) - 1: 4}
    out_shape = [
        jax.ShapeDtypeStruct((bsz, N_HEADS, t, LANES), BF16),
        jax.ShapeDtypeStruct((bsz, N_HEADS, t, LANES), BF16),
        jax.ShapeDtypeStruct((bsz, t, D_ATT), BF16),
        kv_shape, kv_shape,
        jax.ShapeDtypeStruct((bsz, N_HEADS, t), F32),
        jax.ShapeDtypeStruct(u_shape, F32),
    ]
    out_specs = [
        head_blk, head_blk,
        pl.BlockSpec((1, tb, D_ATT), lambda b, i: (b, i, 0)),
        kv_spec, kv_spec,
        pl.BlockSpec((1, N_HEADS, tb), lambda b, i: (b, 0, i)),
        u_spec,
    ]

    def body(*refs):
        n_in = len(args)
        ins, rest = refs[:10], refs[n_in:]
        _inproj_kernel(*ins, *rest, tb=tb)

    return pl.pallas_call(
        body,
        grid=(bsz, nblk),
        in_specs=in_specs,
        out_specs=out_specs,
        out_shape=out_shape,
        scratch_shapes=[pltpu.VMEM((1, LANES), F32)],
        input_output_aliases=aliases,
        compiler_params=pltpu.CompilerParams(
            dimension_semantics=("arbitrary", "arbitrary"), vmem_limit_bytes=VMEM_LIMIT),
        name="inproj",
    )(*args)


def _attn_kernel(qa_ref, ka_ref, v_ref, o_ref, *, tq):
    i = pl.program_id(2)
    row = lax.broadcasted_iota(jnp.int32, (tq, tq), 0)
    col = lax.broadcasted_iota(jnp.int32, (tq, tq), 1)
    causal = col <= row
    outs = []
    for hh in range(2):
        q = qa_ref[0, hh]

        def tile(j, carry, masked, q=q, hh=hh):
            m, l, acc = carry
            start = pl.multiple_of(j * tq, tq)
            kj = ka_ref[0, hh, pl.ds(start, tq), :]
            vj = v_ref[0, pl.ds(start, tq), :]
            s = lax.dot_general(q, kj, NT_DIMS, preferred_element_type=F32)
            if masked:
                s = jnp.where(causal, s, MASK_VALUE)
            m_new = jnp.maximum(m, jnp.max(s, axis=1, keepdims=True))
            alpha = jnp.exp(m - m_new)
            p = jnp.exp(s - m_new)
            l = alpha * l + jnp.sum(p, axis=1, keepdims=True)
            acc = alpha * acc + _dot(p.astype(BF16), vj)
            return m_new, l, acc

        init = (jnp.full((tq, 1), MASK_VALUE, F32), jnp.zeros((tq, 1), F32),
                jnp.zeros((tq, LANES), F32))
        carry = lax.fori_loop(0, i, functools.partial(tile, masked=False), init)
        m, l, acc = tile(i, carry, True)
        outs.append(acc / l)
    lo_half = lax.broadcasted_iota(jnp.int32, (tq, LANES), 1) < HEAD_DIM
    o_ref[0] = jnp.where(lo_half, outs[0], outs[1]).astype(BF16)


def _attn_call(qa, ka, v, tq):
    bsz, _, t, _ = qa.shape
    return pl.pallas_call(
        functools.partial(_attn_kernel, tq=tq),
        grid=(bsz, N_HEADS // 2, t // tq),
        in_specs=[
            pl.BlockSpec((1, 2, tq, LANES), lambda b, p, i: (b, p, i, 0)),
            pl.BlockSpec((1, 2, t, LANES), lambda b, p, i: (b, p, 0, 0)),
            pl.BlockSpec((1, t, LANES), lambda b, p, i: (b, 0, p)),
        ],
        out_specs=pl.BlockSpec((1, tq, LANES), lambda b, p, i: (b, i, p)),
        out_shape=jax.ShapeDtypeStruct((bsz, t, D_ATT), BF16),
        compiler_params=pltpu.CompilerParams(
            dimension_semantics=("arbitrary", "arbitrary", "arbitrary"),
            vmem_limit_bytes=VMEM_LIMIT),
        name="fox_attn",
    )(qa, ka, v)


def _attn_sample_kernel(qa_ref, ka_ref, v_ref, ck_ref, cv_ref, clf_ref, place_ref, o_ref, fp_scr,
                        *, t, n_past):
    lane_p = lax.broadcasted_iota(jnp.int32, (N_HEADS, n_past), 1)
    c = clf_ref[...]
    total = jnp.sum(c, axis=1, keepdims=True)
    d = 1
    while d < n_past:
        c = c + jnp.where(lane_p >= d, pltpu.roll(c, d, axis=1), 0.0)
        d *= 2
    fp_scr[...] = c - total
    head0 = 2 * pl.program_id(1)
    cvp = (_dot(cv_ref[0].astype(BF16), place_ref[0]) +
           _dot(cv_ref[1].astype(BF16), place_ref[1])).astype(BF16)
    lane = lax.broadcasted_iota(jnp.int32, (t, LANES), 1)
    row = lax.broadcasted_iota(jnp.int32, (t, t), 0)
    col = lax.broadcasted_iota(jnp.int32, (t, t), 1)
    vn = v_ref[0]
    outs = []
    for hh in range(2):
        q = qa_ref[0, hh]
        qf = q.astype(F32)
        fq = jnp.sum(jnp.where((lane >= HEAD_DIM) & (lane < HEAD_DIM + 3), qf, 0.0),
                     axis=1, keepdims=True)
        s_past = lax.dot_general(q[:, 0:HEAD_DIM], ck_ref[hh].astype(BF16), NT_DIMS,
                                 preferred_element_type=F32)
        s_past = s_past + fq - fp_scr[pl.ds(head0 + hh, 1), :]
        s_new = lax.dot_general(q, ka_ref[0, hh], NT_DIMS, preferred_element_type=F32)
        s_new = jnp.where(col <= row, s_new, MASK_VALUE)
        m = jnp.maximum(jnp.max(s_past, axis=1, keepdims=True), jnp.max(s_new, axis=1, keepdims=True))
        p_past = jnp.exp(s_past - m)
        p_new = jnp.exp(s_new - m)
        l = jnp.sum(p_past, axis=1, keepdims=True) + jnp.sum(p_new, axis=1, keepdims=True)
        o = _dot(p_past.astype(BF16), cvp) + _dot(p_new.astype(BF16), vn)
        outs.append(o / l)
    o_ref[0] = jnp.where(lane < HEAD_DIM, outs[0], outs[1]).astype(BF16)


def _attn_sample_call(qa, ka, v, cache_k, cache_v, cache_logf, layer):
    bsz, _, t, _ = qa.shape
    n_past = cache_k.shape[3]
    place = np.zeros((2, HEAD_DIM, LANES), np.float32)
    for hh in range(2):
        place[hh, np.arange(HEAD_DIM), hh * HEAD_DIM + np.arange(HEAD_DIM)] = 1.0
    place = jnp.asarray(place, BF16)
    cache_spec = pl.BlockSpec((None, None, 2, n_past, HEAD_DIM), lambda b, p: (layer, b, p, 0, 0))
    return pl.pallas_call(
        functools.partial(_attn_sample_kernel, t=t, n_past=n_past),
        grid=(bsz, N_HEADS // 2),
        in_specs=[
            pl.BlockSpec((1, 2, t, LANES), lambda b, p: (b, p, 0, 0)),
            pl.BlockSpec((1, 2, t, LANES), lambda b, p: (b, p, 0, 0)),
            pl.BlockSpec((1, t, LANES), lambda b, p: (b, 0, p)),
            cache_spec, cache_spec,
            pl.BlockSpec((None, None, N_HEADS, n_past), lambda b, p: (layer, b, 0, 0)),
            pl.BlockSpec((2, HEAD_DIM, LANES), lambda b, p: (0, 0, 0)),
        ],
        scratch_shapes=[pltpu.VMEM((N_HEADS, n_past), F32)],
        out_specs=pl.BlockSpec((1, t, LANES), lambda b, p: (b, 0, p)),
        out_shape=jax.ShapeDtypeStruct((bsz, t, D_ATT), BF16),
        compiler_params=pltpu.CompilerParams(
            dimension_semantics=("arbitrary", "arbitrary"), vmem_limit_bytes=VMEM_LIMIT),
        name="fox_attn_sample",
    )(qa, ka, v, cache_k, cache_v, cache_logf, place)


def _s5_tables(a_re, a_im, log_dt, b_re, b_im, c_re, c_im, d_skip):
    hp = lax.Precision.HIGHEST
    dt = jnp.exp(log_dt)[..., None]
    x, y = a_re * dt, a_im * dt
    ex, cy, sy = jnp.exp(x), jnp.cos(y), jnp.sin(y)
    ar, ai = ex * cy, ex * sy
    sh = jnp.sin(0.5 * y)
    nr, ni = jnp.expm1(x) * cy - 2.0 * sh * sh, ai
    den = a_re * a_re + a_im * a_im
    fr, fi = (nr * a_re + ni * a_im) / den, (ni * a_re - nr * a_im) / den
    bbr = fr[..., None] * b_re - fi[..., None] * b_im
    bbi = fr[..., None] * b_im + fi[..., None] * b_re
    pr, pi = [jnp.ones_like(ar)], [jnp.zeros_like(ar)]
    for _ in range(S5_CHUNK):
        pr, pi = pr + [pr[-1] * ar - pi[-1] * ai], pi + [pr[-1] * ai + pi[-1] * ar]
    pr, pi = jnp.stack(pr), jnp.stack(pi)
    car = c_re[None] * pr[:, :, :, None, :] - c_im[None] * pi[:, :, :, None, :]
    cai = c_re[None] * pi[:, :, :, None, :] + c_im[None] * pr[:, :, :, None, :]
    kk = (jnp.einsum("ndgmp,dgpq->ndgmq", car[:S5_CHUNK], bbr, precision=hp) -
          jnp.einsum("ndgmp,dgpq->ndgmq", cai[:S5_CHUNK], bbi, precision=hp))
    tt, ss = np.arange(S5_CHUNK)[:, None], np.arange(S5_CHUNK)[None, :]
    lag = np.clip(tt - ss, 0, S5_CHUNK - 1)
    toep = kk[lag] * jnp.asarray(tt >= ss, F32)[:, :, None, None, None, None]
    toep = toep.transpose(2, 3, 0, 4, 1, 5).reshape(a_re.shape[0], N_GROUPS, S5_ROWS, S5_ROWS)
    dd = jnp.tile(d_skip.reshape(-1, N_GROUPS, 1, SSM_GROUP), (1, 1, S5_CHUNK, 1)).reshape(
        -1, N_GROUPS, S5_ROWS)
    toep = toep + dd[..., None] * jnp.eye(S5_ROWS, dtype=F32)
    rev = pr[S5_CHUNK - 1::-1][:S5_CHUNK], pi[S5_CHUNK - 1::-1][:S5_CHUNK]
    bp_r = rev[0][..., None] * bbr[None] - rev[1][..., None] * bbi[None]
    bp_i = rev[0][..., None] * bbi[None] + rev[1][..., None] * bbr[None]
    to_rows = lambda z: z.transpose(1, 2, 0, 4, 3).reshape(-1, N_GROUPS, S5_ROWS, SSM_STATE)
    bpow = jnp.concatenate([to_rows(bp_r), to_rows(bp_i)], axis=-1)
    c_rows = lambda z: z.transpose(1, 2, 0, 3, 4).reshape(-1, N_GROUPS, S5_ROWS, SSM_STATE)
    cpow = jnp.concatenate([c_rows(car[1:]), -c_rows(cai[1:])], axis=-1)
    mr, mi = pr[S5_CHUNK], pi[S5_CHUNK]
    rows_r, rows_i = [], []
    for _ in range(8):
        rows_r.append(jnp.concatenate([mr, mr], axis=-1))
        rows_i.append(jnp.concatenate([-mi, mi], axis=-1))
        mr, mi = mr * mr - mi * mi, 2.0 * mr * mi
    scm = jnp.stack(rows_r + rows_i, axis=2)
    return toep.astype(BF16), bpow.astype(BF16), cpow.astype(BF16), scm


def _gelu_tanh(y):
    return 0.5 * y * (1.0 + jnp.tanh(math.sqrt(2.0 / math.pi) * (y + 0.044715 * (y * y * y))))


def _s5_kernel(*refs, n, n_real, seglen, fin_row0, use_h0):
    if use_h0:
        u_ref, toep_ref, bpow_ref, cpow_ref, scm_ref, wglu_ref, h0_ref = refs[:7]
        o_ref, hfin_ref, z_scr, zt_scr, h_scr = refs[7:]
    else:
        u_ref, toep_ref, bpow_ref, cpow_ref, scm_ref, wglu_ref = refs[:6]
        h0_ref = None
        o_ref, hfin_ref, z_scr, zt_scr, h_scr = refs[6:]
    gps = LANES // SSM_GROUP
    h_scr[0:SCAN_PAD, :] = jnp.zeros((SCAN_PAD, LANES), F32)

    for k in range(S5_CHUNK):
        for j in range(N_U_SLABS):
            vv = u_ref[0, j, pl.ds(k, n_real, stride=S5_CHUNK), :]
            if n_real < n:
                vv = jnp.concatenate([vv, jnp.zeros((n - n_real, LANES), F32)], axis=0)
            z_scr[gps * j:gps * (j + 1), SSM_GROUP * k:SSM_GROUP * (k + 1), :] = (
                vv.T.reshape(gps, SSM_GROUP, n).astype(BF16))

    chunk_in_seg = lax.broadcasted_iota(jnp.int32, (n, LANES), 0) % seglen

    def group(g, _):
        uct = z_scr[g]
        yt = _dot(toep_ref[g], uct)
        h = lax.dot_general(uct, bpow_ref[g], TN_DIMS, preferred_element_type=F32)
        if use_h0:
            h0 = h0_ref[g]
            h = h + scm_ref[g, 0:1, :] * h0 + scm_ref[g, 8:9, :] * pltpu.roll(h0, SSM_STATE, axis=1)
        d, j = 1, 0
        while d < seglen:
            h_scr[SCAN_PAD:SCAN_PAD + n, :] = h
            hs = h_scr[pl.ds(SCAN_PAD - d, n), :]
            if seglen < n:
                hs = jnp.where(chunk_in_seg >= d, hs, 0.0)
            h = h + scm_ref[g, j:j + 1, :] * hs + scm_ref[g, 8 + j:9 + j, :] * pltpu.roll(
                hs, SSM_STATE, axis=1)
            d, j = 2 * d, j + 1
        h_scr[SCAN_PAD:SCAN_PAD + n, :] = h
        hp = h_scr[pl.ds(SCAN_PAD - 1, n), :]
        if seglen < n:
            hp = jnp.where(chunk_in_seg >= 1, hp, h0_ref[g] if use_h0 else 0.0)
        yt = yt + lax.dot_general(cpow_ref[g], hp.astype(BF16), NT_DIMS, preferred_element_type=F32)
        zt_scr[g] = _gelu_tanh(yt)
        hfin_ref[0, g] = h[fin_row0:fin_row0 + 32, :]
        return 0

    lax.fori_loop(0, N_GROUPS, group, 0)

    for k in range(S5_CHUNK):
        slabs = []
        for j in range(N_U_SLABS):
            w = zt_scr[gps * j:gps * (j + 1), SSM_GROUP * k:SSM_GROUP * (k + 1), :]
            slabs.append(w.reshape(LANES, n).T)
        zk = jnp.concatenate(slabs, axis=1)
        out = zk * _sigmoid(_dot(zk.astype(BF16), wglu_ref[...]))
        for j in range(N_U_SLABS):
            o_ref[0, j, pl.ds(k, n_real, stride=S5_CHUNK), :] = out[0:n_real, LANES * j:LANES * (j + 1)]


def _s5_call(u, toep, bpow, cpow, scm, wglu, h0rows, layer, n, n_real, seglen, fin_row0):
    nb, _, tt, _ = u.shape
    use_h0 = h0rows is not None
    one = pl.Buffered(1)
    tab = lambda rows, cols: pl.BlockSpec((None, N_GROUPS, rows, cols), lambda b: (layer, 0, 0, 0),
                                          pipeline_mode=one)
    in_specs = [
        pl.BlockSpec((1, N_U_SLABS, tt, LANES), lambda b: (b, 0, 0, 0), pipeline_mode=one),
        tab(S5_ROWS, S5_ROWS), tab(S5_ROWS, LANES), tab(S5_ROWS, LANES), tab(16, LANES),
        pl.BlockSpec((None, D_SSM, D_SSM), lambda b: (layer, 0, 0), pipeline_mode=one),
    ]
    args = [u, toep, bpow, cpow, scm, wglu]
    if use_h0:
        in_specs.append(pl.BlockSpec((None, N_GROUPS, n, LANES), lambda b: (layer, 0, 0, 0),
                                     pipeline_mode=one))
        args.append(h0rows)
    return pl.pallas_call(
        functools.partial(_s5_kernel, n=n, n_real=n_real, seglen=seglen, fin_row0=fin_row0,
                          use_h0=use_h0),
        grid=(nb,),
        in_specs=in_specs,
        out_specs=[
            pl.BlockSpec((1, N_U_SLABS, tt, LANES), lambda b: (b, 0, 0, 0), pipeline_mode=one),
            pl.BlockSpec((1, N_GROUPS, 32, LANES), lambda b: (b, 0, 0, 0)),
        ],
        out_shape=[
            jax.ShapeDtypeStruct((nb, N_U_SLABS, tt, LANES), F32),
            jax.ShapeDtypeStruct((nb, N_GROUPS, 32, LANES), F32),
        ],
        scratch_shapes=[
            pltpu.VMEM((N_GROUPS, S5_ROWS, n), BF16),
            pltpu.VMEM((N_GROUPS, S5_ROWS, n), F32),
            pltpu.VMEM((SCAN_PAD + n, LANES), F32),
        ],
        compiler_params=pltpu.CompilerParams(
            dimension_semantics=("arbitrary",), vmem_limit_bytes=VMEM_LIMIT),
        name="s5_mix",
    )(*args)


def _out_ffn_kernel(x_ref, att_ref, ssm_ref, g1_ref, sh2_ref, sc2_ref, g2_ref,
                    gpm_ref, gpf_ref, gpo_ref, wo_ref, wg_ref, wu_ref, wd_ref, o_ref):
    mix = jnp.concatenate(
        [att_ref[0]] + [ssm_ref[0, j].astype(BF16) for j in range(N_U_SLABS)], axis=1)
    o = _dot(mix, wo_ref[...])
    x1 = x_ref[0] + g1_ref[0] * _rms(o, gpm_ref[...])
    hf = (_rms(x1, gpf_ref[...]) * (1.0 + sc2_ref[0]) + sh2_ref[0]).astype(BF16)
    gt = _dot(hf, wg_ref[...])
    up = _dot(hf, wu_ref[...])
    hid = (gt * _sigmoid(gt) * up).astype(BF16)
    f = _dot(hid, wd_ref[...])
    o_ref[0] = x1 + g2_ref[0] * _rms(f, gpo_ref[...])


def _out_ffn_call(x, att, ssm, ssm_flat, modr, layer, row0, g_post_mix, g_pre_ffn, g_post_ffn,
                  w_out, w_gate, w_up, w_down, tb):
    bsz, t, _ = x.shape
    nblk = t // tb
    one = pl.Buffered(1)
    const = lambda b, i: (0, 0)
    if ssm_flat:
        ssm_spec = pl.BlockSpec((1, N_U_SLABS, tb, LANES), lambda b, i: (0, 0, b * nblk + i, 0))
    else:
        ssm_spec = pl.BlockSpec((1, N_U_SLABS, tb, LANES), lambda b, i: (b, 0, i, 0))
    row = pl.BlockSpec((1, D_MODEL), const)
    return pl.pallas_call(
        _out_ffn_kernel,
        grid=(bsz, nblk),
        in_specs=[
            pl.BlockSpec((1, tb, D_MODEL), lambda b, i: (b, i, 0)),
            pl.BlockSpec((1, tb, D_ATT), lambda b, i: (b, i, 0)),
            ssm_spec,
            _mod_spec(layer, row0, 2), _mod_spec(layer, row0, 3), _mod_spec(layer, row0, 4),
            _mod_spec(layer, row0, 5),
            row, row, row,
            pl.BlockSpec((D_MODEL, D_MODEL), const, pipeline_mode=one),
            pl.BlockSpec((D_MODEL, D_FF), const, pipeline_mode=one),
            pl.BlockSpec((D_MODEL, D_FF), const, pipeline_mode=one),
            pl.BlockSpec((D_FF, D_MODEL), const, pipeline_mode=one),
        ],
        out_specs=pl.BlockSpec((1, tb, D_MODEL), lambda b, i: (b, i, 0)),
        out_shape=jax.ShapeDtypeStruct((bsz, t, D_MODEL), F32),
        compiler_params=pltpu.CompilerParams(
            dimension_semantics=("arbitrary", "arbitrary"), vmem_limit_bytes=VMEM_LIMIT),
        name="out_ffn",
    )(x, att, ssm, modr, modr, modr, modr, g_post_mix, g_pre_ffn, g_post_ffn,
      w_out, w_gate, w_up, w_down)


def kernel(x_prompt, x_sample, c_prompt, c_sample, cache_k, cache_v, cache_logf, state_ssm_re,
           state_ssm_im, w_ada, b_ada, g_pre_mix, g_post_mix, g_pre_ffn, g_post_ffn, w_in, b_forget,
           ssm_a_re, ssm_a_im, ssm_log_dt, ssm_b_re, ssm_b_im, ssm_c_re, ssm_c_im, ssm_d, w_glu,
           w_out, w_gate, w_up, w_down):
    depth = w_in.shape[0]
    bp, tp, _ = x_prompt.shape
    bs, ts, _ = x_sample.shape
    n_att = 3 * D_ATT + N_HEADS

    c_all = jnp.concatenate([c_prompt, c_sample, jnp.zeros((16 - bp - bs, D_MODEL), F32)], axis=0)
    mod = _ada_call(c_all, w_ada, b_ada)
    modr = mod.reshape(depth * 16 * 6, 1, D_MODEL)

    w_in_p = jnp.concatenate(
        [w_in[:, :, :n_att], jnp.zeros((depth, D_MODEL, GATE_PAD - N_HEADS), F32), w_in[:, :, n_att:]],
        axis=2).astype(BF16)
    bf_p = jnp.pad(b_forget, ((0, 0), (0, GATE_PAD - N_HEADS)))
    w_glu_b, w_out_b = w_glu.astype(BF16), w_out.astype(BF16)
    w_gate_b, w_up_b, w_down_b = w_gate.astype(BF16), w_up.astype(BF16), w_down.astype(BF16)
    toep, bpow, cpow, scm = _s5_tables(ssm_a_re, ssm_a_im, ssm_log_dt, ssm_b_re, ssm_b_im,
                                       ssm_c_re, ssm_c_im, ssm_d)

    n_chunk_p = tp // S5_CHUNK
    seg_s = ts // S5_CHUNK
    n_s = LANES
    h0 = jnp.concatenate([state_ssm_re, state_ssm_im], axis=-1)
    h0rows = lax.pad(h0.transpose(0, 2, 1, 3), jnp.zeros((), F32),
                     ((0, 0, 0), (0, 0, 0), (0, n_s - (bs - 1) * seg_s - 1, seg_s - 1), (0, 0, 0)))

    yp, ys = x_prompt, x_sample
    kvp = None
    lfp, ssp, kss, vss, lfs, sss = [], [], [], [], [], []
    for l in range(depth):
        kv_in = (None, None) if kvp is None else kvp
        if kvp is None:
            kv_bufs = (jax.ShapeDtypeStruct((depth, bp, N_HEADS, tp, HEAD_DIM), F32),) * 2
        else:
            kv_bufs = kvp
        qa, ka, v, kbuf, vbuf, lf, u = _inproj_call(
            yp, modr, l, 0, g_pre_mix[l:l + 1], w_in_p[l], bf_p[l:l + 1], kv_bufs, False, 512)
        kvp = (kbuf, vbuf)
        lfp.append(lf)
        att = _attn_call(qa, ka, v, 512)
        ssm, hfin = _s5_call(u, toep, bpow, cpow, scm, w_glu_b, None, l, n_chunk_p, n_chunk_p,
                             n_chunk_p, n_chunk_p - 32)
        ssp.append(hfin[:, :, 31, :])
        yp = _out_ffn_call(yp, att, ssm, False, modr, l, 0, g_post_mix[l:l + 1], g_pre_ffn[l:l + 1],
                           g_post_ffn[l:l + 1], w_out_b[l], w_gate_b[l], w_up_b[l], w_down_b[l], 512)

        qa, ka, v, k_s, v_s, lf, u = _inproj_call(
            ys, modr, l, bp, g_pre_mix[l:l + 1], w_in_p[l], bf_p[l:l + 1], None, True, ts)
        kss.append(k_s)
        vss.append(v_s)
        lfs.append(lf)
        att = _attn_sample_call(qa, ka, v, cache_k, cache_v, cache_logf, l)
        ssm, hfin = _s5_call(u, toep, bpow, cpow, scm, w_glu_b, h0rows, l, n_s, bs * seg_s, seg_s, 0)
        sss.append(hfin[0, :, seg_s - 1:bs * seg_s:seg_s, :].transpose(1, 0, 2))
        ys = _out_ffn_call(ys, att, ssm, True, modr, l, bp, g_post_mix[l:l + 1], g_pre_ffn[l:l + 1],
                           g_post_ffn[l:l + 1], w_out_b[l], w_gate_b[l], w_up_b[l], w_down_b[l], ts)

    ssp, sss = jnp.stack(ssp), jnp.stack(sss)
    return (yp, ys, kvp[0], kvp[1], jnp.stack(lfp),
            ssp[..., :SSM_STATE], ssp[..., SSM_STATE:],
            jnp.stack(kss), jnp.stack(vss), jnp.stack(lfs),
            sss[..., :SSM_STATE], sss[..., SSM_STATE:])
```

```python
import functools
import math

import numpy as np
import jax
import jax.numpy as jnp
from jax import lax
from jax.experimental import pallas as pl
from jax.experimental.pallas import tpu as pltpu

F32 = jnp.float32
BF16 = jnp.bfloat16

D_MODEL = 1024
N_HEADS = 8
HEAD_DIM = 64
D_ATT = N_HEADS * HEAD_DIM
D_SSM = D_MODEL - D_ATT
SSM_GROUP = 16
N_GROUPS = D_SSM // SSM_GROUP
SSM_STATE = 64
D_FF = 2816
EPS = 1e-6

LANES = 128
S5_CHUNK = 16
S5_ROWS = S5_CHUNK * SSM_GROUP
GATE_PAD = LANES
W_IN_COLS = 3 * D_ATT + GATE_PAD + D_SSM
U_COL0 = 3 * D_ATT + GATE_PAD
N_U_SLABS = D_SSM // LANES
SCAN_PAD = 128
VMEM_LIMIT = 56 * 1024 * 1024
MASK_VALUE = -1e30
LOG2E = math.log2(math.e)
DEN_LANE = HEAD_DIM

NT_DIMS = (((1,), (1,)), ((), ()))
TN_DIMS = (((0,), (0,)), ((), ()))


def _sigmoid(x):
    return 1.0 / (1.0 + jnp.exp(-x))


def _split3(x):
    hi = x.astype(BF16)
    r = x - hi.astype(F32)
    mid = r.astype(BF16)
    lo = (r - mid.astype(F32)).astype(BF16)
    return hi, mid, lo


def _split2(x):
    hi = x.astype(BF16)
    return hi, (x - hi.astype(F32)).astype(BF16)


def _dot(a, b):
    return jnp.dot(a, b, preferred_element_type=F32)


def _dot_nt(a, b):
    return lax.dot_general(a, b, NT_DIMS, preferred_element_type=F32)


def _rms(x, g):
    ms = jnp.mean(x * x, axis=-1, keepdims=True)
    return x * lax.rsqrt(ms + EPS) * g


def _ada_kernel(c_ref, w_ref, b_ref, o_ref):
    c = c_ref[...]
    a_hi, a_lo = _split2(c * _sigmoid(c))
    w_hi, w_lo = _split2(w_ref[0])
    o_ref[0] = _dot(a_hi, w_hi) + _dot(a_lo, w_hi) + _dot(a_hi, w_lo) + b_ref[0]


def _ada_call(c_all, w_ada, b_ada):
    depth = w_ada.shape[0]
    nb = 1536
    n_out = w_ada.shape[2]
    return pl.pallas_call(
        _ada_kernel,
        grid=(depth, n_out // nb),
        in_specs=[
            pl.BlockSpec((16, D_MODEL), lambda l, j: (0, 0)),
            pl.BlockSpec((1, D_MODEL, nb), lambda l, j: (l, 0, j)),
            pl.BlockSpec((1, 1, nb), lambda l, j: (l, 0, j)),
        ],
        out_specs=pl.BlockSpec((1, 16, nb), lambda l, j: (l, 0, j)),
        out_shape=jax.ShapeDtypeStruct((depth, 16, n_out), F32),
        compiler_params=pltpu.CompilerParams(
            dimension_semantics=("arbitrary", "arbitrary"), vmem_limit_bytes=VMEM_LIMIT),
        name="ada",
    )(c_all, w_ada, b_ada.reshape(depth, 1, n_out))


def _inproj_kernel(x_ref, sh_ref, sc_ref, g_ref, wt_ref, bf_ref, tri_ref, sel_ref, one_ref, eye_ref,
                   qa_ref, ka_ref, va_ref, kc_ref, vc_ref, lf_ref, u_ref, carry_ref,
                   *, tb, kv_transposed):
    @pl.when(pl.program_id(1) == 0)
    def _():
        carry_ref[...] = jnp.zeros_like(carry_ref)

    x = x_ref[0]
    hm = _rms(x, g_ref[...]) * (1.0 + sc_ref[0]) + sh_ref[0]
    proj = _dot_nt(hm.astype(BF16), wt_ref[...])
    q = proj[:, 0:D_ATT] * (LOG2E * HEAD_DIM ** -0.5)
    k = proj[:, D_ATT:2 * D_ATT]
    v = proj[:, 2 * D_ATT:3 * D_ATT]
    gate = proj[:, 3 * D_ATT:U_COL0] + bf_ref[...]
    logf = jnp.minimum(gate, 0.0) - jnp.log1p(jnp.exp(-jnp.abs(gate)))

    lf_parts = _split3(logf)
    cs = _dot(tri_ref[...], jnp.concatenate(lf_parts, axis=1))
    fcum = cs[:, 0:LANES] + cs[:, LANES:2 * LANES] + cs[:, 2 * LANES:3 * LANES] + carry_ref[...]
    carry_ref[...] = fcum[tb - 1:tb, :]

    extra = _dot(jnp.concatenate(_split3(fcum * LOG2E), axis=1), sel_ref[...]) + one_ref[...]
    lane = lax.broadcasted_iota(jnp.int32, (tb, LANES), 1)
    lo_half = lane < HEAD_DIM
    den_one = jnp.where(lane == DEN_LANE, 1.0, 0.0)
    if kv_transposed:
        kt, vt = k.T, v.T
    for h in range(N_HEADS):
        pair = slice(LANES * (h // 2), LANES * (h // 2) + LANES)
        q2, k2, v2 = q[:, pair], k[:, pair], v[:, pair]
        if h % 2:
            q2 = pltpu.roll(q2, HEAD_DIM, axis=1)
            k2 = pltpu.roll(k2, HEAD_DIM, axis=1)
            v2 = pltpu.roll(v2, HEAD_DIM, axis=1)
        qa_ref[0, h] = jnp.where(lo_half, q2, extra[:, LANES * h:LANES * (h + 1)]).astype(BF16)
        ka_ref[0, h] = jnp.where(
            lo_half, k2, extra[:, D_MODEL + LANES * h:D_MODEL + LANES * (h + 1)]).astype(BF16)
        va_ref[0, h] = jnp.where(lo_half, v2, den_one).astype(BF16)
        if kv_transposed:
            kc_ref[h] = kt[HEAD_DIM * h:HEAD_DIM * (h + 1), :]
            vc_ref[h] = vt[HEAD_DIM * h:HEAD_DIM * (h + 1), :]
        else:
            kc_ref[h] = k2[:, 0:HEAD_DIM]
            vc_ref[h] = v2[:, 0:HEAD_DIM]
    lf_ref[0] = sum(_dot_nt(eye_ref[...], part) for part in lf_parts)
    for j in range(N_U_SLABS):
        u_ref[0, j] = proj[:, U_COL0 + LANES * j:U_COL0 + LANES * (j + 1)]


def _inproj_consts(tb):
    tri = np.tril(np.ones((tb, tb), np.float32))
    sel = np.zeros((3 * LANES, 2 * D_MODEL), np.float32)
    one = np.zeros((1, 2 * D_MODEL), np.float32)
    for h in range(N_HEADS):
        for p in range(3):
            sel[p * LANES + h, LANES * h + HEAD_DIM + p] = 1.0
            sel[p * LANES + h, D_MODEL + LANES * h + HEAD_DIM + 3 + p] = -1.0
            one[0, LANES * h + HEAD_DIM + 3 + p] = 1.0
            one[0, D_MODEL + LANES * h + HEAD_DIM + p] = 1.0
    eye = np.eye(N_HEADS, LANES, dtype=np.float32)
    return jnp.asarray(tri, BF16), jnp.asarray(sel, BF16), jnp.asarray(one, F32), jnp.asarray(eye, BF16)


def _mod_spec(layer, row0, chunk):
    return pl.BlockSpec((1, 1, D_MODEL), lambda b, i: ((layer * 16 + row0 + b) * 6 + chunk, 0, 0))


def _inproj_call(x, modr, layer, row0, g_pre, w_in_t, b_forget, kv_bufs, kv_depth, u_flat, tb):
    bsz, t, _ = x.shape
    nblk = t // tb
    tri, sel, one, eye = _inproj_consts(tb)
    if u_flat:
        u_shape = (1, N_U_SLABS, bsz * t, LANES)
        u_spec = pl.BlockSpec((1, N_U_SLABS, tb, LANES), lambda b, i: (0, 0, b * nblk + i, 0))
    else:
        u_shape = (bsz, N_U_SLABS, t, LANES)
        u_spec = pl.BlockSpec((1, N_U_SLABS, tb, LANES), lambda b, i: (b, 0, i, 0))
    const = lambda b, i: (0, 0)
    in_specs = [
        pl.BlockSpec((1, tb, D_MODEL), lambda b, i: (b, i, 0)),
        _mod_spec(layer, row0, 0), _mod_spec(layer, row0, 1),
        pl.BlockSpec((1, D_MODEL), const),
        pl.BlockSpec((W_IN_COLS, D_MODEL), const),
        pl.BlockSpec((1, GATE_PAD), const),
        pl.BlockSpec((tb, tb), const),
        pl.BlockSpec((3 * LANES, 2 * D_MODEL), const),
        pl.BlockSpec((1, 2 * D_MODEL), const),
        pl.BlockSpec((N_HEADS, LANES), const),
    ]
    args = [x, modr, modr, g_pre, w_in_t, b_forget, tri, sel, one, eye]
    n_main = len(args)
    head_blk = pl.BlockSpec((1, N_HEADS, tb, LANES), lambda b, i: (b, 0, i, 0))
    aliases = {}
    if kv_depth is None:
        kv_shape = jax.ShapeDtypeStruct((bsz, N_HEADS, t, HEAD_DIM), F32)
        kv_spec = pl.BlockSpec((None, N_HEADS, tb, HEAD_DIM), lambda b, i: (b, 0, i, 0))
    else:
        kv_shape = jax.ShapeDtypeStruct((kv_depth, bsz, N_HEADS, HEAD_DIM, t), F32)
        kv_spec = pl.BlockSpec((None, None, N_HEADS, HEAD_DIM, tb), lambda b, i: (layer, b, 0, 0, i))
        if kv_bufs is not None:
            in_specs += [pl.BlockSpec(memory_space=pl.ANY), pl.BlockSpec(memory_space=pl.ANY)]
            args += list(kv_bufs)
            aliases = {n_main: 3, n_main + 1: 4}
    out_shape = [
        jax.ShapeDtypeStruct((bsz, N_HEADS, t, LANES), BF16),
        jax.ShapeDtypeStruct((bsz, N_HEADS, t, LANES), BF16),
        jax.ShapeDtypeStruct((bsz, N_HEADS, t, LANES), BF16),
        kv_shape, kv_shape,
        jax.ShapeDtypeStruct((bsz, N_HEADS, t), F32),
        jax.ShapeDtypeStruct(u_shape, F32),
    ]
    out_specs = [
        head_blk, head_blk, head_blk,
        kv_spec, kv_spec,
        pl.BlockSpec((1, N_HEADS, tb), lambda b, i: (b, 0, i)),
        u_spec,
    ]
    n_args = len(args)

    def body(*refs):
        _inproj_kernel(*refs[:n_main], *refs[n_args:], tb=tb, kv_transposed=kv_depth is not None)

    return pl.pallas_call(
        body,
        grid=(bsz, nblk),
        in_specs=in_specs,
        out_specs=out_specs,
        out_shape=out_shape,
        scratch_shapes=[pltpu.VMEM((1, LANES), F32)],
        input_output_aliases=aliases,
        compiler_params=pltpu.CompilerParams(
            dimension_semantics=("arbitrary", "arbitrary"), vmem_limit_bytes=VMEM_LIMIT),
        name="inproj",
    )(*args)


def _finish_heads(accs, lane):
    outs = []
    for acc in accs:
        den = jnp.sum(jnp.where(lane == DEN_LANE, acc, 0.0), axis=1, keepdims=True)
        outs.append(acc * (1.0 / den))
    return jnp.where(lane < HEAD_DIM, outs[0], pltpu.roll(outs[1], HEAD_DIM, axis=1)).astype(BF16)


def _attn_kernel(qa_ref, ka_ref, va_ref, o_ref, *, tq, hps):
    i = pl.program_id(2)
    row = lax.broadcasted_iota(jnp.int32, (tq, tq), 0)
    col = lax.broadcasted_iota(jnp.int32, (tq, tq), 1)
    causal = col <= row
    qs = [qa_ref[0, hh] for hh in range(hps)]

    def tile(j, carry, masked):
        start = pl.multiple_of(j * tq, tq)
        new = []
        for hh in range(hps):
            m, acc = carry[hh]
            s = _dot_nt(qs[hh], ka_ref[0, hh, pl.ds(start, tq), :])
            if masked:
                s = jnp.where(causal, s, MASK_VALUE)
            m_new = jnp.maximum(m, jnp.max(s, axis=1, keepdims=True))
            p = jnp.exp2(s - m_new)
            acc = jnp.exp2(m - m_new) * acc + _dot(p.astype(BF16), va_ref[0, hh, pl.ds(start, tq), :])
            new.append((m_new, acc))
        return tuple(new)

    init = ((jnp.full((tq, 1), MASK_VALUE, F32), jnp.zeros((tq, LANES), F32)),) * hps
    carry = lax.fori_loop(0, i, functools.partial(tile, masked=False), init)
    carry = tile(i, carry, True)
    lane = lax.broadcasted_iota(jnp.int32, (tq, LANES), 1)
    for pr in range(hps // 2):
        o_ref[0, :, LANES * pr:LANES * (pr + 1)] = _finish_heads(
            [carry[2 * pr][1], carry[2 * pr + 1][1]], lane)


def _attn_call(qa, ka, va, tq, hps):
    bsz, _, t, _ = qa.shape
    whole = pl.BlockSpec((1, hps, t, LANES), lambda b, p, i: (b, p, 0, 0))
    return pl.pallas_call(
        functools.partial(_attn_kernel, tq=tq, hps=hps),
        grid=(bsz, N_HEADS // hps, t // tq),
        in_specs=[pl.BlockSpec((1, hps, tq, LANES), lambda b, p, i: (b, p, i, 0)), whole, whole],
        out_specs=pl.BlockSpec((1, tq, HEAD_DIM * hps), lambda b, p, i: (b, i, p)),
        out_shape=jax.ShapeDtypeStruct((bsz, t, D_ATT), BF16),
        compiler_params=pltpu.CompilerParams(
            dimension_semantics=("arbitrary", "arbitrary", "arbitrary"),
            vmem_limit_bytes=VMEM_LIMIT),
        name="fox_attn",
    )(qa, ka, va)


def _attn_sample_kernel(qa_ref, ka_ref, va_ref, ckt_ref, cvt_ref, clf_ref, o_ref, fp_scr, *, t, n_past):
    lane_p = lax.broadcasted_iota(jnp.int32, (N_HEADS, n_past), 1)
    c = clf_ref[...]
    total = jnp.sum(c, axis=1, keepdims=True)
    d = 1
    while d < n_past:
        c = c + jnp.where(lane_p >= d, pltpu.roll(c, d, axis=1), 0.0)
        d *= 2
    fp_scr[...] = (c - total) * LOG2E
    head0 = 2 * pl.program_id(1)
    lane = lax.broadcasted_iota(jnp.int32, (t, LANES), 1)
    causal = (lax.broadcasted_iota(jnp.int32, (t, t), 1) <= lax.broadcasted_iota(jnp.int32, (t, t), 0))
    den_rows = jnp.where(lax.broadcasted_iota(jnp.int32, (LANES - HEAD_DIM, n_past), 0) == 0, 1.0, 0.0)
    accs = []
    for hh in range(2):
        q = qa_ref[0, hh]
        fq = jnp.sum(jnp.where((lane >= HEAD_DIM) & (lane < HEAD_DIM + 3), q.astype(F32), 0.0),
                     axis=1, keepdims=True)
        s_past = _dot(q[:, 0:HEAD_DIM], ckt_ref[hh].astype(BF16))
        s_past = s_past + fq - fp_scr[pl.ds(head0 + hh, 1), :]
        s_new = jnp.where(causal, _dot_nt(q, ka_ref[0, hh]), MASK_VALUE)
        m = jnp.maximum(jnp.max(s_past, axis=1, keepdims=True), jnp.max(s_new, axis=1, keepdims=True))
        p_past = jnp.exp2(s_past - m).astype(BF16)
        p_new = jnp.exp2(s_new - m).astype(BF16)
        cvt_aug = jnp.concatenate([cvt_ref[hh], den_rows], axis=0).astype(BF16)
        accs.append(_dot_nt(p_past, cvt_aug) + _dot(p_new, va_ref[0, hh]))
    o_ref[0] = _finish_heads(accs, lane)


def _attn_sample_call(qa, ka, va, cache_kt, cache_vt, cache_logf, layer):
    bsz, _, t, _ = qa.shape
    n_past = cache_kt.shape[4]
    cache_spec = pl.BlockSpec((None, None, 2, HEAD_DIM, n_past), lambda b, p: (layer, b, p, 0, 0))
    new_spec = pl.BlockSpec((1, 2, t, LANES), lambda b, p: (b, p, 0, 0))
    return pl.pallas_call(
        functools.partial(_attn_sample_kernel, t=t, n_past=n_past),
        grid=(bsz, N_HEADS // 2),
        in_specs=[
            new_spec, new_spec, new_spec, cache_spec, cache_spec,
            pl.BlockSpec((None, None, N_HEADS, n_past), lambda b, p: (layer, b, 0, 0)),
        ],
        scratch_shapes=[pltpu.VMEM((N_HEADS, n_past), F32)],
        out_specs=pl.BlockSpec((1, t, LANES), lambda b, p: (b, 0, p)),
        out_shape=jax.ShapeDtypeStruct((bsz, t, D_ATT), BF16),
        compiler_params=pltpu.CompilerParams(
            dimension_semantics=("arbitrary", "arbitrary"), vmem_limit_bytes=VMEM_LIMIT),
        name="fox_attn_sample",
    )(qa, ka, va, cache_kt, cache_vt, cache_logf)


def _s5_table_kernel(ca_ref, cb_ref, ba_ref, bb_ref, prr_ref, pii_ref, dv_ref,
                     toep_ref, bpow_ref, cpow_ref, tt_scr, *, gb):
    row = lax.broadcasted_iota(jnp.int32, (SSM_GROUP, S5_ROWS), 0)
    lane = lax.broadcasted_iota(jnp.int32, (SSM_GROUP, S5_ROWS), 1)
    for gi in range(gb):
        ca, cb, ba, bb = ca_ref[gi], cb_ref[gi], ba_ref[gi], bb_ref[gi]
        power = lambda a, b, n: a * prr_ref[gi, n:n + 1, :] + b * pii_ref[gi, n:n + 1, :]
        xs = [power(ca, cb, n) for n in range(S5_CHUNK + 1)]
        cpow_ref[gi] = jnp.concatenate(xs[1:], axis=0).astype(BF16)
        bpow_ref[gi] = jnp.concatenate(
            [power(ba, bb, S5_CHUNK - 1 - s) for s in range(S5_CHUNK)], axis=0).astype(BF16)
        x_hi, x_lo = _split2(jnp.concatenate(xs[:S5_CHUNK], axis=0))
        b_hi, b_lo = _split2(ba)
        r0 = _dot_nt(b_hi, x_hi) + _dot_nt(b_lo, x_hi) + _dot_nt(b_hi, x_lo)
        r0 = r0 + jnp.where(lane == row, dv_ref[gi], 0.0)
        for s in range(S5_CHUNK):
            blk = r0
            if s:
                blk = jnp.where(lane >= SSM_GROUP * s, pltpu.roll(r0, SSM_GROUP * s, axis=1), 0.0)
            tt_scr[SSM_GROUP * s:SSM_GROUP * (s + 1), :] = blk
        toep_ref[gi] = tt_scr[...].T.astype(BF16)


def _s5_tables(a_re, a_im, log_dt, b_re_t, b_im_t, c_re, c_im, d_skip):
    depth = a_re.shape[0]
    dt = jnp.exp(log_dt)[..., None]
    x, y = a_re * dt, a_im * dt
    ex, cy, sy = jnp.exp(x), jnp.cos(y), jnp.sin(y)
    ar, ai = ex * cy, ex * sy
    sh = jnp.sin(0.5 * y)
    nr, ni = jnp.expm1(x) * cy - 2.0 * sh * sh, ai
    den = a_re * a_re + a_im * a_im
    fr = ((nr * a_re + ni * a_im) / den)[:, :, None, :]
    fi = ((ni * a_re - nr * a_im) / den)[:, :, None, :]
    bbr, bbi = fr * b_re_t - fi * b_im_t, fr * b_im_t + fi * b_re_t
    pr, pi = [jnp.ones_like(ar)], [jnp.zeros_like(ar)]
    for _ in range(S5_CHUNK):
        pr, pi = pr + [pr[-1] * ar - pi[-1] * ai], pi + [pr[-1] * ai + pi[-1] * ar]
    mr, mi = pr[S5_CHUNK], pi[S5_CHUNK]
    pr, pi = jnp.stack(pr, axis=2), jnp.stack(pi, axis=2)
    cat = lambda a, b: jnp.concatenate([a, b], axis=-1)
    ins = [cat(c_re, -c_im), cat(-c_im, -c_re), cat(bbr, bbi), cat(-bbi, bbr), cat(pr, pr), cat(pi, pi),
           jnp.pad(d_skip.reshape(depth, N_GROUPS, 1, SSM_GROUP), ((0, 0), (0, 0), (0, 0), (0, S5_ROWS - SSM_GROUP)))]
    gb = 8
    spec = lambda rows, cols: pl.BlockSpec((None, gb, rows, cols), lambda l, j: (l, j, 0, 0))
    tab = lambda cols: jax.ShapeDtypeStruct((depth, N_GROUPS, S5_ROWS, cols), BF16)
    toep, bpow, cpow = pl.pallas_call(
        functools.partial(_s5_table_kernel, gb=gb),
        grid=(depth, N_GROUPS // gb),
        in_specs=[spec(SSM_GROUP, LANES)] * 4 + [spec(S5_CHUNK + 1, LANES)] * 2 + [spec(1, S5_ROWS)],
        out_specs=[spec(S5_ROWS, S5_ROWS), spec(S5_ROWS, LANES), spec(S5_ROWS, LANES)],
        out_shape=[tab(S5_ROWS), tab(LANES), tab(LANES)],
        scratch_shapes=[pltpu.VMEM((S5_ROWS, S5_ROWS), F32)],
        compiler_params=pltpu.CompilerParams(
            dimension_semantics=("arbitrary", "arbitrary"), vmem_limit_bytes=VMEM_LIMIT),
        name="s5_tables",
    )(*ins)
    rows_r, rows_i = [], []
    for _ in range(8):
        rows_r.append(cat(mr, mr))
        rows_i.append(cat(-mi, mi))
        mr, mi = mr * mr - mi * mi, 2.0 * mr * mi
    scm = jnp.stack(rows_r + rows_i, axis=2)
    return toep, bpow, cpow, scm


def _gelu_tanh(y):
    return 0.5 * y * (1.0 + jnp.tanh(math.sqrt(2.0 / math.pi) * (y + 0.044715 * (y * y * y))))


def _s5_kernel(*refs, n, n_real, seglen, fin_row0, use_h0):
    if use_h0:
        u_ref, toep_ref, bpow_ref, cpow_ref, scm_ref, wglu_ref, h0_ref = refs[:7]
        o_ref, hfin_ref, z_scr, zt_scr, h_scr, h0_scr = refs[7:]
    else:
        u_ref, toep_ref, bpow_ref, cpow_ref, scm_ref, wglu_ref = refs[:6]
        o_ref, hfin_ref, z_scr, zt_scr, h_scr = refs[6:]
    gps = LANES // SSM_GROUP
    h_scr[0:SCAN_PAD, :] = jnp.zeros((SCAN_PAD, LANES), F32)
    if use_h0:
        h0_scr[...] = jnp.zeros((n, LANES), F32)

    for k in range(S5_CHUNK):
        for j in range(N_U_SLABS):
            vv = u_ref[0, j, pl.ds(k, n_real, stride=S5_CHUNK), :]
            if n_real < n:
                vv = jnp.concatenate([vv, jnp.zeros((n - n_real, LANES), F32)], axis=0)
            z_scr[gps * j:gps * (j + 1), SSM_GROUP * k:SSM_GROUP * (k + 1), :] = (
                vv.T.reshape(gps, SSM_GROUP, n).astype(BF16))

    chunk_in_seg = lax.broadcasted_iota(jnp.int32, (n, LANES), 0) % seglen

    def group(g, _):
        uct = z_scr[g]
        yt = _dot(toep_ref[g], uct)
        h = lax.dot_general(uct, bpow_ref[g], TN_DIMS, preferred_element_type=F32)
        if use_h0:
            h0_scr[pl.ds(0, h0_ref.shape[1], stride=seglen), :] = h0_ref[g]
            h0 = h0_scr[...]
            h = h + scm_ref[g, 0:1, :] * h0 + scm_ref[g, 8:9, :] * pltpu.roll(h0, SSM_STATE, axis=1)
        d, j = 1, 0
        while d < seglen:
            h_scr[SCAN_PAD:SCAN_PAD + n, :] = h
            hs = h_scr[pl.ds(SCAN_PAD - d, n), :]
            if seglen < n:
                hs = jnp.where(chunk_in_seg >= d, hs, 0.0)
            h = h + scm_ref[g, j:j + 1, :] * hs + scm_ref[g, 8 + j:9 + j, :] * pltpu.roll(
                hs, SSM_STATE, axis=1)
            d, j = 2 * d, j + 1
        h_scr[SCAN_PAD:SCAN_PAD + n, :] = h
        hp = h_scr[pl.ds(SCAN_PAD - 1, n), :]
        if seglen < n:
            hp = jnp.where(chunk_in_seg >= 1, hp, h0 if use_h0 else 0.0)
        yt = yt + _dot_nt(cpow_ref[g], hp.astype(BF16))
        zt_scr[g] = _gelu_tanh(yt)
        hfin_ref[0, g] = h[fin_row0:fin_row0 + 32, :]
        return 0

    lax.fori_loop(0, N_GROUPS, group, 0)

    for k in range(S5_CHUNK):
        slabs = []
        for j in range(N_U_SLABS):
            w = zt_scr[gps * j:gps * (j + 1), SSM_GROUP * k:SSM_GROUP * (k + 1), :]
            slabs.append(w.reshape(LANES, n).T)
        zk = jnp.concatenate(slabs, axis=1)
        out = zk * _sigmoid(_dot(zk.astype(BF16), wglu_ref[...]))
        for j in range(N_U_SLABS):
            o_ref[0, j, pl.ds(k, n_real, stride=S5_CHUNK), :] = out[0:n_real, LANES * j:LANES * (j + 1)]


def _s5_call(u, toep, bpow, cpow, scm, wglu, h0, layer, n, n_real, seglen, fin_row0):
    nb, _, tt, _ = u.shape
    use_h0 = h0 is not None
    one = pl.Buffered(1)
    tab = lambda rows, cols: pl.BlockSpec((None, N_GROUPS, rows, cols), lambda b: (layer, 0, 0, 0),
                                          pipeline_mode=one)
    in_specs = [
        pl.BlockSpec((1, N_U_SLABS, tt, LANES), lambda b: (b, 0, 0, 0), pipeline_mode=one),
        tab(S5_ROWS, S5_ROWS), tab(S5_ROWS, LANES), tab(S5_ROWS, LANES), tab(16, LANES),
        pl.BlockSpec((None, D_SSM, D_SSM), lambda b: (layer, 0, 0), pipeline_mode=one),
    ]
    args = [u, toep, bpow, cpow, scm, wglu]
    scratch = [
        pltpu.VMEM((N_GROUPS, S5_ROWS, n), BF16),
        pltpu.VMEM((N_GROUPS, S5_ROWS, n), F32),
        pltpu.VMEM((SCAN_PAD + n, LANES), F32),
    ]
    if use_h0:
        in_specs.append(tab(h0.shape[2], LANES))
        args.append(h0)
        scratch.append(pltpu.VMEM((n, LANES), F32))
    return pl.pallas_call(
        functools.partial(_s5_kernel, n=n, n_real=n_real, seglen=seglen, fin_row0=fin_row0,
                          use_h0=use_h0),
        grid=(nb,),
        in_specs=in_specs,
        out_specs=[
            pl.BlockSpec((1, N_U_SLABS, tt, LANES), lambda b: (b, 0, 0, 0), pipeline_mode=one),
            pl.BlockSpec((1, N_GROUPS, 32, LANES), lambda b: (b, 0, 0, 0)),
        ],
        out_shape=[
            jax.ShapeDtypeStruct((nb, N_U_SLABS, tt, LANES), F32),
            jax.ShapeDtypeStruct((nb, N_GROUPS, 32, LANES), F32),
        ],
        scratch_shapes=scratch,
        compiler_params=pltpu.CompilerParams(
            dimension_semantics=("arbitrary",), vmem_limit_bytes=VMEM_LIMIT),
        name="s5_mix",
    )(*args)


def _out_ffn_kernel(x_ref, att_ref, ssm_ref, g1_ref, sh2_ref, sc2_ref, g2_ref,
                    gpm_ref, gpf_ref, gpo_ref, wo_ref, wg_ref, wu_ref, wd_ref, o_ref):
    mix = jnp.concatenate(
        [att_ref[0]] + [ssm_ref[0, j].astype(BF16) for j in range(N_U_SLABS)], axis=1)
    o = _dot(mix, wo_ref[...])
    x1 = x_ref[0] + g1_ref[0] * _rms(o, gpm_ref[...])
    hf = (_rms(x1, gpf_ref[...]) * (1.0 + sc2_ref[0]) + sh2_ref[0]).astype(BF16)
    gt = _dot(hf, wg_ref[...])
    up = _dot(hf, wu_ref[...])
    hid = (gt * _sigmoid(gt) * up).astype(BF16)
    f = _dot(hid, wd_ref[...])
    o_ref[0] = x1 + g2_ref[0] * _rms(f, gpo_ref[...])


def _out_ffn_call(x, att, ssm, ssm_flat, modr, layer, row0, g_post_mix, g_pre_ffn, g_post_ffn,
                  w_out, w_gate, w_up, w_down, tb):
    bsz, t, _ = x.shape
    nblk = t // tb
    one = pl.Buffered(1)
    const = lambda b, i: (0, 0)
    if ssm_flat:
        ssm_spec = pl.BlockSpec((1, N_U_SLABS, tb, LANES), lambda b, i: (0, 0, b * nblk + i, 0))
    else:
        ssm_spec = pl.BlockSpec((1, N_U_SLABS, tb, LANES), lambda b, i: (b, 0, i, 0))
    row = pl.BlockSpec((1, D_MODEL), const)
    return pl.pallas_call(
        _out_ffn_kernel,
        grid=(bsz, nblk),
        in_specs=[
            pl.BlockSpec((1, tb, D_MODEL), lambda b, i: (b, i, 0)),
            pl.BlockSpec((1, tb, D_ATT), lambda b, i: (b, i, 0)),
            ssm_spec,
            _mod_spec(layer, row0, 2), _mod_spec(layer, row0, 3), _mod_spec(layer, row0, 4),
            _mod_spec(layer, row0, 5),
            row, row, row,
            pl.BlockSpec((D_MODEL, D_MODEL), const, pipeline_mode=one),
            pl.BlockSpec((D_MODEL, D_FF), const, pipeline_mode=one),
            pl.BlockSpec((D_MODEL, D_FF), const, pipeline_mode=one),
            pl.BlockSpec((D_FF, D_MODEL), const, pipeline_mode=one),
        ],
        out_specs=pl.BlockSpec((1, tb, D_MODEL), lambda b, i: (b, i, 0)),
        out_shape=jax.ShapeDtypeStruct((bsz, t, D_MODEL), F32),
        compiler_params=pltpu.CompilerParams(
            dimension_semantics=("arbitrary", "arbitrary"), vmem_limit_bytes=VMEM_LIMIT),
        name="out_ffn",
    )(x, att, ssm, modr, modr, modr, modr, g_post_mix, g_pre_ffn, g_post_ffn,
      w_out, w_gate, w_up, w_down)


def kernel(x_prompt, x_sample, c_prompt, c_sample, cache_k, cache_v, cache_logf, state_ssm_re,
           state_ssm_im, w_ada, b_ada, g_pre_mix, g_post_mix, g_pre_ffn, g_post_ffn, w_in, b_forget,
           ssm_a_re, ssm_a_im, ssm_log_dt, ssm_b_re, ssm_b_im, ssm_c_re, ssm_c_im, ssm_d, w_glu,
           w_out, w_gate, w_up, w_down):
    depth = w_in.shape[0]
    bp, tp, _ = x_prompt.shape
    bs, ts, _ = x_sample.shape
    n_att = 3 * D_ATT + N_HEADS

    c_all = jnp.concatenate([c_prompt, c_sample, jnp.zeros((16 - bp - bs, D_MODEL), F32)], axis=0)
    mod = _ada_call(c_all, w_ada, b_ada)
    modr = mod.reshape(depth * 16 * 6, 1, D_MODEL)

    w_in_t = jnp.swapaxes(w_in, 1, 2)
    w_in_t = jnp.concatenate(
        [w_in_t[:, :n_att], jnp.zeros((depth, GATE_PAD - N_HEADS, D_MODEL), F32), w_in_t[:, n_att:]],
        axis=1).astype(BF16)
    bf_p = jnp.pad(b_forget, ((0, 0), (0, GATE_PAD - N_HEADS)))
    w_glu_b, w_out_b = w_glu.astype(BF16), w_out.astype(BF16)
    w_gate_b, w_up_b, w_down_b = w_gate.astype(BF16), w_up.astype(BF16), w_down.astype(BF16)
    toep, bpow, cpow, scm = _s5_tables(ssm_a_re, ssm_a_im, ssm_log_dt, jnp.swapaxes(ssm_b_re, 2, 3),
                                       jnp.swapaxes(ssm_b_im, 2, 3), ssm_c_re, ssm_c_im, ssm_d)
    cache_kt, cache_vt = jnp.swapaxes(cache_k, 3, 4), jnp.swapaxes(cache_v, 3, 4)
    h0 = jnp.concatenate([state_ssm_re, state_ssm_im], axis=-1).transpose(0, 2, 1, 3)

    n_chunk_p = tp // S5_CHUNK
    seg_s = ts // S5_CHUNK
    yp, ys = x_prompt, x_sample
    kvp = None
    lfp, ssp, kss, vss, lfs, sss = [], [], [], [], [], []
    for l in range(depth):
        qa, ka, va, kbuf, vbuf, lf, u = _inproj_call(
            yp, modr, l, 0, g_pre_mix[l:l + 1], w_in_t[l], bf_p[l:l + 1], kvp, depth, False, 512)
        kvp = (kbuf, vbuf)
        lfp.append(lf)
        att = _attn_call(qa, ka, va, 512, 4)
        ssm, hfin = _s5_call(u, toep, bpow, cpow, scm, w_glu_b, None, l, n_chunk_p, n_chunk_p,
                             n_chunk_p, n_chunk_p - 32)
        ssp.append(hfin[:, :, 31, :])
        yp = _out_ffn_call(yp, att, ssm, False, modr, l, 0, g_post_mix[l:l + 1], g_pre_ffn[l:l + 1],
                           g_post_ffn[l:l + 1], w_out_b[l], w_gate_b[l], w_up_b[l], w_down_b[l], 512)

        qa, ka, va, k_s, v_s, lf, u = _inproj_call(
            ys, modr, l, bp, g_pre_mix[l:l + 1], w_in_t[l], bf_p[l:l + 1], None, None, True, ts)
        kss.append(k_s)
        vss.append(v_s)
        lfs.append(lf)
        att = _attn_sample_call(qa, ka, va, cache_kt, cache_vt, cache_logf, l)
        ssm, hfin = _s5_call(u, toep, bpow, cpow, scm, w_glu_b, h0, l, LANES, bs * seg_s, seg_s, 0)
        sss.append(hfin[0, :, seg_s - 1:bs * seg_s:seg_s, :].transpose(1, 0, 2))
        ys = _out_ffn_call(ys, att, ssm, True, modr, l, bp, g_post_mix[l:l + 1], g_pre_ffn[l:l + 1],
                           g_post_ffn[l:l + 1], w_out_b[l], w_gate_b[l], w_up_b[l], w_down_b[l], ts)

    ssp, sss = jnp.stack(ssp), jnp.stack(sss)
    return (yp, ys, jnp.swapaxes(kvp[0], 3, 4), jnp.swapaxes(kvp[1], 3, 4), jnp.stack(lfp),
            ssp[..., :SSM_STATE], ssp[..., SSM_STATE:],
            jnp.stack(kss), jnp.stack(vss), jnp.stack(lfs),
            sss[..., :SSM_STATE], sss[..., SSM_STATE:])
```

```python
import functools
import math

import numpy as np
import jax
import jax.numpy as jnp
from jax import lax
from jax.experimental import pallas as pl
from jax.experimental.pallas import tpu as pltpu

F32 = jnp.float32
BF16 = jnp.bfloat16

D_MODEL = 1024
N_HEADS = 8
HEAD_DIM = 64
D_ATT = N_HEADS * HEAD_DIM
D_SSM = D_MODEL - D_ATT
SSM_GROUP = 16
N_GROUPS = D_SSM // SSM_GROUP
SSM_STATE = 64
D_FF = 2816
EPS = 1e-6

LANES = 128
S5_CHUNK = 16
S5_ROWS = S5_CHUNK * SSM_GROUP
GATE_PAD = LANES
W_IN_COLS = 3 * D_ATT + GATE_PAD + D_SSM
U_COL0 = 3 * D_ATT + GATE_PAD
N_U_SLABS = D_SSM // LANES
SCAN_PAD = 128
VMEM_LIMIT = 56 * 1024 * 1024
MASK_VALUE = -1e30
LOG2E = math.log2(math.e)
DEN_LANE = HEAD_DIM
SCORE_LIMIT = 80.0
NORM_SLACK = 1.02

NT_DIMS = (((1,), (1,)), ((), ()))
TN_DIMS = (((0,), (0,)), ((), ()))


def _sigmoid(x):
    return 1.0 / (1.0 + jnp.exp(-x))


def _split3(x):
    hi = x.astype(BF16)
    r = x - hi.astype(F32)
    mid = r.astype(BF16)
    lo = (r - mid.astype(F32)).astype(BF16)
    return hi, mid, lo


def _split2(x):
    hi = x.astype(BF16)
    return hi, (x - hi.astype(F32)).astype(BF16)


def _dot(a, b):
    return jnp.dot(a, b, preferred_element_type=F32)


def _dot_nt(a, b):
    return lax.dot_general(a, b, NT_DIMS, preferred_element_type=F32)


def _rms(x, g):
    ms = jnp.mean(x * x, axis=-1, keepdims=True)
    return x * lax.rsqrt(ms + EPS) * g


def _ada_kernel(c_ref, w_ref, b_ref, o_ref):
    c = c_ref[...]
    a_hi, a_lo = _split2(c * _sigmoid(c))
    w_hi, w_lo = _split2(w_ref[0])
    o_ref[0] = _dot(a_hi, w_hi) + _dot(a_lo, w_hi) + _dot(a_hi, w_lo) + b_ref[0]


def _ada_call(c_all, w_ada, b_ada):
    depth = w_ada.shape[0]
    nb = 1536
    n_out = w_ada.shape[2]
    return pl.pallas_call(
        _ada_kernel,
        grid=(depth, n_out // nb),
        in_specs=[
            pl.BlockSpec((16, D_MODEL), lambda l, j: (0, 0)),
            pl.BlockSpec((1, D_MODEL, nb), lambda l, j: (l, 0, j)),
            pl.BlockSpec((1, 1, nb), lambda l, j: (l, 0, j)),
        ],
        out_specs=pl.BlockSpec((1, 16, nb), lambda l, j: (l, 0, j)),
        out_shape=jax.ShapeDtypeStruct((depth, 16, n_out), F32),
        compiler_params=pltpu.CompilerParams(
            dimension_semantics=("arbitrary", "arbitrary"), vmem_limit_bytes=VMEM_LIMIT),
        name="ada",
    )(c_all, w_ada, b_ada.reshape(depth, 1, n_out))


def _inproj_kernel(x_ref, sh_ref, sc_ref, g_ref, wt_ref, bf_ref, tri_ref, sel_ref, one_ref, eye_ref,
                   go_ref, qa_ref, ka_ref, va_ref, kc_ref, vc_ref, lf_ref, u_ref, st_ref, carry_ref,
                   *, tb, kv_transposed):
    @pl.when(pl.program_id(1) == 0)
    def _():
        carry_ref[...] = jnp.zeros_like(carry_ref)

    x = x_ref[0]
    hm = _rms(x, g_ref[...]) * (1.0 + sc_ref[0]) + sh_ref[0]
    proj = _dot_nt(hm.astype(BF16), wt_ref[...])
    q = proj[:, 0:D_ATT] * (LOG2E * HEAD_DIM ** -0.5)
    k = proj[:, D_ATT:2 * D_ATT]
    v = proj[:, 2 * D_ATT:3 * D_ATT]
    gate = proj[:, 3 * D_ATT:U_COL0] + bf_ref[...]
    logf = jnp.minimum(gate, 0.0) - jnp.log1p(jnp.exp(-jnp.abs(gate)))
    lane = lax.broadcasted_iota(jnp.int32, (tb, LANES), 1)

    def pack3(val):
        hi, mid, lo = (part.astype(F32) for part in _split3(val))
        packed = jnp.where(lane < 2 * N_HEADS, pltpu.roll(mid, N_HEADS, axis=1),
                           jnp.where(lane < 3 * N_HEADS, pltpu.roll(lo, 2 * N_HEADS, axis=1), 0.0))
        return jnp.where(lane < N_HEADS, hi, packed).astype(BF16)

    lf_pack = pack3(logf)
    cs = _dot(tri_ref[...], lf_pack)
    fcum = (cs + pltpu.roll(cs, LANES - N_HEADS, axis=1) + pltpu.roll(cs, LANES - 2 * N_HEADS, axis=1)
            + carry_ref[...])
    carry_ref[...] = fcum[tb - 1:tb, :]

    extra = _dot(pack3(fcum * LOG2E), sel_ref[...]) + one_ref[...]
    qn2 = _dot((q * q).astype(BF16), go_ref[...])
    kn2 = _dot((k * k).astype(BF16), go_ref[...])
    st_ref[0, 0] = jnp.concatenate(
        [jnp.max(qn2, axis=0, keepdims=True), jnp.max(kn2, axis=0, keepdims=True),
         jnp.zeros((6, LANES), F32)], axis=0)
    lo_half = lane < HEAD_DIM
    den_one = jnp.where(lane == DEN_LANE, 1.0, 0.0)
    if kv_transposed:
        kt, vt = k.T, v.T
    for h in range(N_HEADS):
        pair = slice(LANES * (h // 2), LANES * (h // 2) + LANES)
        q2, k2, v2 = q[:, pair], k[:, pair], v[:, pair]
        if h % 2:
            q2 = pltpu.roll(q2, HEAD_DIM, axis=1)
            k2 = pltpu.roll(k2, HEAD_DIM, axis=1)
            v2 = pltpu.roll(v2, HEAD_DIM, axis=1)
        qa_ref[0, h] = jnp.where(lo_half, q2, extra[:, LANES * h:LANES * (h + 1)]).astype(BF16)
        ka_ref[0, h] = jnp.where(
            lo_half, k2, extra[:, D_MODEL + LANES * h:D_MODEL + LANES * (h + 1)]).astype(BF16)
        va_ref[0, h] = jnp.where(lo_half, v2, den_one).astype(BF16)
        if kv_transposed:
            kc_ref[h] = kt[HEAD_DIM * h:HEAD_DIM * (h + 1), :]
            vc_ref[h] = vt[HEAD_DIM * h:HEAD_DIM * (h + 1), :]
        else:
            kc_ref[h] = k2[:, 0:HEAD_DIM]
            vc_ref[h] = v2[:, 0:HEAD_DIM]
    lf_t = _dot_nt(eye_ref[...], lf_pack)
    lf_ref[0] = lf_t[0:N_HEADS] + lf_t[N_HEADS:2 * N_HEADS] + lf_t[2 * N_HEADS:3 * N_HEADS]
    for j in range(N_U_SLABS):
        u_ref[0, j] = proj[:, U_COL0 + LANES * j:U_COL0 + LANES * (j + 1)]


def _inproj_consts(tb):
    tri = np.tril(np.ones((tb, tb), np.float32))
    sel = np.zeros((LANES, 2 * D_MODEL), np.float32)
    one = np.zeros((1, 2 * D_MODEL), np.float32)
    group_ones = np.zeros((D_ATT, LANES), np.float32)
    for h in range(N_HEADS):
        group_ones[HEAD_DIM * h:HEAD_DIM * (h + 1), h] = 1.0
        for p in range(3):
            sel[p * N_HEADS + h, LANES * h + HEAD_DIM + p] = 1.0
            sel[p * N_HEADS + h, D_MODEL + LANES * h + HEAD_DIM + 3 + p] = -1.0
            one[0, LANES * h + HEAD_DIM + 3 + p] = 1.0
            one[0, D_MODEL + LANES * h + HEAD_DIM + p] = 1.0
    eye = np.eye(4 * N_HEADS, LANES, dtype=np.float32)
    eye[3 * N_HEADS:] = 0.0
    return (jnp.asarray(tri, BF16), jnp.asarray(sel, BF16), jnp.asarray(one, F32), jnp.asarray(eye, BF16),
            jnp.asarray(group_ones, BF16))


def _mod_spec(layer, row0, chunk):
    return pl.BlockSpec((1, 1, D_MODEL), lambda b, i: ((layer * 16 + row0 + b) * 6 + chunk, 0, 0))


def _inproj_call(x, modr, layer, row0, g_pre, w_in_t, b_forget, kv_bufs, kv_depth, u_flat, tb):
    bsz, t, _ = x.shape
    nblk = t // tb
    tri, sel, one, eye, group_ones = _inproj_consts(tb)
    if u_flat:
        u_shape = (1, N_U_SLABS, bsz * t, LANES)
        u_spec = pl.BlockSpec((1, N_U_SLABS, tb, LANES), lambda b, i: (0, 0, b * nblk + i, 0))
    else:
        u_shape = (bsz, N_U_SLABS, t, LANES)
        u_spec = pl.BlockSpec((1, N_U_SLABS, tb, LANES), lambda b, i: (b, 0, i, 0))
    const = lambda b, i: (0, 0)
    in_specs = [
        pl.BlockSpec((1, tb, D_MODEL), lambda b, i: (b, i, 0)),
        _mod_spec(layer, row0, 0), _mod_spec(layer, row0, 1),
        pl.BlockSpec((1, D_MODEL), const),
        pl.BlockSpec((None, W_IN_COLS, D_MODEL), lambda b, i: (layer, 0, 0)),
        pl.BlockSpec((1, GATE_PAD), const),
        pl.BlockSpec((tb, tb), const),
        pl.BlockSpec((LANES, 2 * D_MODEL), const),
        pl.BlockSpec((1, 2 * D_MODEL), const),
        pl.BlockSpec((4 * N_HEADS, LANES), const),
        pl.BlockSpec((D_ATT, LANES), const),
    ]
    args = [x, modr, modr, g_pre, w_in_t, b_forget, tri, sel, one, eye, group_ones]
    n_main = len(args)
    head_blk = pl.BlockSpec((1, N_HEADS, tb, LANES), lambda b, i: (b, 0, i, 0))
    aliases = {}
    if kv_depth is None:
        kv_shape = jax.ShapeDtypeStruct((bsz, N_HEADS, t, HEAD_DIM), F32)
        kv_spec = pl.BlockSpec((None, N_HEADS, tb, HEAD_DIM), lambda b, i: (b, 0, i, 0))
    else:
        kv_shape = jax.ShapeDtypeStruct((kv_depth, bsz, N_HEADS, HEAD_DIM, t), F32)
        kv_spec = pl.BlockSpec((None, None, N_HEADS, HEAD_DIM, tb), lambda b, i: (layer, b, 0, 0, i))
        if kv_bufs is not None:
            in_specs += [pl.BlockSpec(memory_space=pl.ANY), pl.BlockSpec(memory_space=pl.ANY)]
            args += list(kv_bufs)
            aliases = {n_main: 3, n_main + 1: 4}
    out_shape = [
        jax.ShapeDtypeStruct((bsz, N_HEADS, t, LANES), BF16),
        jax.ShapeDtypeStruct((bsz, N_HEADS, t, LANES), BF16),
        jax.ShapeDtypeStruct((bsz, N_HEADS, t, LANES), BF16),
        kv_shape, kv_shape,
        jax.ShapeDtypeStruct((bsz, N_HEADS, t), F32),
        jax.ShapeDtypeStruct(u_shape, F32),
        jax.ShapeDtypeStruct((bsz, nblk, 8, LANES), F32),
    ]
    out_specs = [
        head_blk, head_blk, head_blk,
        kv_spec, kv_spec,
        pl.BlockSpec((1, N_HEADS, tb), lambda b, i: (b, 0, i)),
        u_spec,
        pl.BlockSpec((1, 1, 8, LANES), lambda b, i: (b, i, 0, 0)),
    ]
    n_args = len(args)

    def body(*refs):
        _inproj_kernel(*refs[:n_main], *refs[n_args:], tb=tb, kv_transposed=kv_depth is not None)

    return pl.pallas_call(
        body,
        grid=(bsz, nblk),
        in_specs=in_specs,
        out_specs=out_specs,
        out_shape=out_shape,
        scratch_shapes=[pltpu.VMEM((1, LANES), F32)],
        input_output_aliases=aliases,
        compiler_params=pltpu.CompilerParams(
            dimension_semantics=("arbitrary", "arbitrary"), vmem_limit_bytes=VMEM_LIMIT),
        name="inproj",
    )(*args)


def _finish_heads(accs, lane):
    outs = []
    for acc in accs:
        den = jnp.sum(jnp.where(lane == DEN_LANE, acc, 0.0), axis=1, keepdims=True)
        outs.append(acc * (1.0 / den))
    return jnp.where(lane < HEAD_DIM, outs[0], pltpu.roll(outs[1], HEAD_DIM, axis=1)).astype(BF16)


def _attn_kernel(small_ref, qa_ref, ka_ref, va_ref, o_ref, *, tq, hps):
    i = pl.program_id(2)
    row = lax.broadcasted_iota(jnp.int32, (tq, tq), 0)
    col = lax.broadcasted_iota(jnp.int32, (tq, tq), 1)
    causal = col <= row
    lane = lax.broadcasted_iota(jnp.int32, (tq, LANES), 1)
    qs = [qa_ref[0, hh] for hh in range(hps)]

    def score(j, hh, masked):
        start = pl.multiple_of(j * tq, tq)
        s = _dot_nt(qs[hh], ka_ref[0, hh, pl.ds(start, tq), :])
        return jnp.where(causal, s, MASK_VALUE) if masked else s

    def values(j, hh):
        return va_ref[0, hh, pl.ds(pl.multiple_of(j * tq, tq), tq), :]

    def store(accs):
        for pr in range(hps // 2):
            o_ref[0, :, LANES * pr:LANES * (pr + 1)] = _finish_heads(accs[2 * pr:2 * pr + 2], lane)

    def tile_plain(j, accs, masked):
        return tuple(accs[hh] + _dot(jnp.exp2(score(j, hh, masked)).astype(BF16), values(j, hh))
                     for hh in range(hps))

    def tile_shifted(j, carry, masked):
        new = []
        for hh in range(hps):
            m, acc = carry[hh]
            s = score(j, hh, masked)
            m_new = jnp.maximum(m, jnp.max(s, axis=1, keepdims=True))
            p = jnp.exp2(s - m_new)
            new.append((m_new, jnp.exp2(m - m_new) * acc + _dot(p.astype(BF16), values(j, hh))))
        return tuple(new)

    small = small_ref[pl.program_id(0), pl.program_id(1)] != 0
    zero = jnp.zeros((tq, LANES), F32)

    @pl.when(small)
    def _():
        accs = lax.fori_loop(0, i, functools.partial(tile_plain, masked=False), (zero,) * hps)
        store(tile_plain(i, accs, True))

    @pl.when(jnp.logical_not(small))
    def _():
        init = ((jnp.full((tq, 1), MASK_VALUE, F32), zero),) * hps
        carry = lax.fori_loop(0, i, functools.partial(tile_shifted, masked=False), init)
        store([c[1] for c in tile_shifted(i, carry, True)])


def _attn_call(qa, ka, va, stats, tq, hps):
    bsz, _, t, _ = qa.shape
    st = jnp.max(stats, axis=1)
    bound = jnp.sqrt(st[:, 0, :N_HEADS] * st[:, 1, :N_HEADS]) * NORM_SLACK
    small = (jnp.max(bound.reshape(bsz, N_HEADS // hps, hps), axis=-1) < SCORE_LIMIT).astype(jnp.int32)
    whole = pl.BlockSpec((1, hps, t, LANES), lambda b, p, i, flags: (b, p, 0, 0))
    return pl.pallas_call(
        functools.partial(_attn_kernel, tq=tq, hps=hps),
        grid_spec=pltpu.PrefetchScalarGridSpec(
            num_scalar_prefetch=1,
            grid=(bsz, N_HEADS // hps, t // tq),
            in_specs=[pl.BlockSpec((1, hps, tq, LANES), lambda b, p, i, flags: (b, p, i, 0)),
                      whole, whole],
            out_specs=pl.BlockSpec((1, tq, HEAD_DIM * hps), lambda b, p, i, flags: (b, i, p)),
        ),
        out_shape=jax.ShapeDtypeStruct((bsz, t, D_ATT), BF16),
        compiler_params=pltpu.CompilerParams(
            dimension_semantics=("arbitrary", "arbitrary", "arbitrary"),
            vmem_limit_bytes=VMEM_LIMIT),
        name="fox_attn",
    )(small, qa, ka, va)


def _attn_sample_kernel(qa_ref, ka_ref, va_ref, ckt_ref, cvt_ref, clf_ref, o_ref, fp_scr, *, t, n_past):
    lane_p = lax.broadcasted_iota(jnp.int32, (N_HEADS, n_past), 1)
    c = clf_ref[...]
    total = jnp.sum(c, axis=1, keepdims=True)
    d = 1
    while d < n_past:
        c = c + jnp.where(lane_p >= d, pltpu.roll(c, d, axis=1), 0.0)
        d *= 2
    fp_scr[...] = (c - total) * LOG2E
    head0 = 2 * pl.program_id(1)
    lane = lax.broadcasted_iota(jnp.int32, (t, LANES), 1)
    causal = (lax.broadcasted_iota(jnp.int32, (t, t), 1) <= lax.broadcasted_iota(jnp.int32, (t, t), 0))
    den_rows = jnp.where(lax.broadcasted_iota(jnp.int32, (LANES - HEAD_DIM, n_past), 0) == 0, 1.0, 0.0)
    accs = []
    for hh in range(2):
        q = qa_ref[0, hh]
        fq = jnp.sum(jnp.where((lane >= HEAD_DIM) & (lane < HEAD_DIM + 3), q.astype(F32), 0.0),
                     axis=1, keepdims=True)
        s_past = _dot(q[:, 0:HEAD_DIM], ckt_ref[hh].astype(BF16))
        s_past = s_past + fq - fp_scr[pl.ds(head0 + hh, 1), :]
        s_new = jnp.where(causal, _dot_nt(q, ka_ref[0, hh]), MASK_VALUE)
        m = jnp.maximum(jnp.max(s_past, axis=1, keepdims=True), jnp.max(s_new, axis=1, keepdims=True))
        p_past = jnp.exp2(s_past - m).astype(BF16)
        p_new = jnp.exp2(s_new - m).astype(BF16)
        cvt_aug = jnp.concatenate([cvt_ref[hh], den_rows], axis=0).astype(BF16)
        accs.append(_dot_nt(p_past, cvt_aug) + _dot(p_new, va_ref[0, hh]))
    o_ref[0] = _finish_heads(accs, lane)


def _attn_sample_call(qa, ka, va, cache_kt, cache_vt, cache_logf, layer):
    bsz, _, t, _ = qa.shape
    n_past = cache_kt.shape[4]
    cache_spec = pl.BlockSpec((None, None, 2, HEAD_DIM, n_past), lambda b, p: (layer, b, p, 0, 0))
    new_spec = pl.BlockSpec((1, 2, t, LANES), lambda b, p: (b, p, 0, 0))
    return pl.pallas_call(
        functools.partial(_attn_sample_kernel, t=t, n_past=n_past),
        grid=(bsz, N_HEADS // 2),
        in_specs=[
            new_spec, new_spec, new_spec, cache_spec, cache_spec,
            pl.BlockSpec((None, None, N_HEADS, n_past), lambda b, p: (layer, b, 0, 0)),
        ],
        scratch_shapes=[pltpu.VMEM((N_HEADS, n_past), F32)],
        out_specs=pl.BlockSpec((1, t, LANES), lambda b, p: (b, 0, p)),
        out_shape=jax.ShapeDtypeStruct((bsz, t, D_ATT), BF16),
        compiler_params=pltpu.CompilerParams(
            dimension_semantics=("arbitrary", "arbitrary"), vmem_limit_bytes=VMEM_LIMIT),
        name="fox_attn_sample",
    )(qa, ka, va, cache_kt, cache_vt, cache_logf)


def _s5_table_kernel(ca_ref, cb_ref, ba_ref, bb_ref, prr_ref, pii_ref, dv_ref,
                     toep_ref, bpow_ref, cpow_ref, tt_scr, *, gb):
    row = lax.broadcasted_iota(jnp.int32, (SSM_GROUP, S5_ROWS), 0)
    lane = lax.broadcasted_iota(jnp.int32, (SSM_GROUP, S5_ROWS), 1)
    for gi in range(gb):
        ca, cb, ba, bb = ca_ref[gi], cb_ref[gi], ba_ref[gi], bb_ref[gi]
        power = lambda a, b, n: a * prr_ref[gi, n:n + 1, :] + b * pii_ref[gi, n:n + 1, :]
        xs = [power(ca, cb, n) for n in range(S5_CHUNK + 1)]
        cpow_ref[gi] = jnp.concatenate(xs[1:], axis=0).astype(BF16)
        bpow_ref[gi] = jnp.concatenate(
            [power(ba, bb, S5_CHUNK - 1 - s) for s in range(S5_CHUNK)], axis=0).astype(BF16)
        x_hi, x_lo = _split2(jnp.concatenate(xs[:S5_CHUNK], axis=0))
        b_hi, b_lo = _split2(ba)
        r0 = _dot_nt(b_hi, x_hi) + _dot_nt(b_lo, x_hi) + _dot_nt(b_hi, x_lo)
        r0 = r0 + jnp.where(lane == row, dv_ref[gi], 0.0)
        for s in range(S5_CHUNK):
            blk = r0
            if s:
                blk = jnp.where(lane >= SSM_GROUP * s, pltpu.roll(r0, SSM_GROUP * s, axis=1), 0.0)
            tt_scr[SSM_GROUP * s:SSM_GROUP * (s + 1), :] = blk
        toep_ref[gi] = tt_scr[...].T.astype(BF16)


def _s5_tables(a_re, a_im, log_dt, b_re_t, b_im_t, c_re, c_im, d_skip):
    depth = a_re.shape[0]
    dt = jnp.exp(log_dt)[..., None]
    x, y = a_re * dt, a_im * dt
    ex, cy, sy = jnp.exp(x), jnp.cos(y), jnp.sin(y)
    ar, ai = ex * cy, ex * sy
    sh = jnp.sin(0.5 * y)
    nr, ni = jnp.expm1(x) * cy - 2.0 * sh * sh, ai
    den = a_re * a_re + a_im * a_im
    fr = ((nr * a_re + ni * a_im) / den)[:, :, None, :]
    fi = ((ni * a_re - nr * a_im) / den)[:, :, None, :]
    bbr, bbi = fr * b_re_t - fi * b_im_t, fr * b_im_t + fi * b_re_t
    pr, pi = [jnp.ones_like(ar)], [jnp.zeros_like(ar)]
    for _ in range(S5_CHUNK):
        pr, pi = pr + [pr[-1] * ar - pi[-1] * ai], pi + [pr[-1] * ai + pi[-1] * ar]
    mr, mi = pr[S5_CHUNK], pi[S5_CHUNK]
    pr, pi = jnp.stack(pr, axis=2), jnp.stack(pi, axis=2)
    cat = lambda a, b: jnp.concatenate([a, b], axis=-1)
    ins = [cat(c_re, -c_im), cat(-c_im, -c_re), cat(bbr, bbi), cat(-bbi, bbr), cat(pr, pr), cat(pi, pi),
           jnp.pad(d_skip.reshape(depth, N_GROUPS, 1, SSM_GROUP), ((0, 0), (0, 0), (0, 0), (0, S5_ROWS - SSM_GROUP)))]
    gb = 8
    spec = lambda rows, cols: pl.BlockSpec((None, gb, rows, cols), lambda l, j: (l, j, 0, 0))
    tab = lambda cols: jax.ShapeDtypeStruct((depth, N_GROUPS, S5_ROWS, cols), BF16)
    toep, bpow, cpow = pl.pallas_call(
        functools.partial(_s5_table_kernel, gb=gb),
        grid=(depth, N_GROUPS // gb),
        in_specs=[spec(SSM_GROUP, LANES)] * 4 + [spec(S5_CHUNK + 1, LANES)] * 2 + [spec(1, S5_ROWS)],
        out_specs=[spec(S5_ROWS, S5_ROWS), spec(S5_ROWS, LANES), spec(S5_ROWS, LANES)],
        out_shape=[tab(S5_ROWS), tab(LANES), tab(LANES)],
        scratch_shapes=[pltpu.VMEM((S5_ROWS, S5_ROWS), F32)],
        compiler_params=pltpu.CompilerParams(
            dimension_semantics=("arbitrary", "arbitrary"), vmem_limit_bytes=VMEM_LIMIT),
        name="s5_tables",
    )(*ins)
    rows_r, rows_i = [], []
    for _ in range(8):
        rows_r.append(cat(mr, mr))
        rows_i.append(cat(-mi, mi))
        mr, mi = mr * mr - mi * mi, 2.0 * mr * mi
    scm = jnp.stack(rows_r + rows_i, axis=2)
    return toep, bpow, cpow, scm


def _gelu_tanh(y):
    return 0.5 * y * (1.0 + jnp.tanh(math.sqrt(2.0 / math.pi) * (y + 0.044715 * (y * y * y))))


def _s5_kernel(*refs, n, n_real, seglen, fin_row0, use_h0):
    if use_h0:
        u_ref, toep_ref, bpow_ref, cpow_ref, scm_ref, wglu_ref, h0_ref = refs[:7]
        o_ref, hfin_ref, z_scr, zt_scr, h_scr, h0_scr = refs[7:]
    else:
        u_ref, toep_ref, bpow_ref, cpow_ref, scm_ref, wglu_ref = refs[:6]
        o_ref, hfin_ref, z_scr, zt_scr, h_scr = refs[6:]
    gps = LANES // SSM_GROUP
    h_scr[0:SCAN_PAD, :] = jnp.zeros((SCAN_PAD, LANES), F32)
    if use_h0:
        h0_scr[...] = jnp.zeros((n, LANES), F32)

    for k in range(S5_CHUNK):
        for j in range(N_U_SLABS):
            vv = u_ref[0, j, pl.ds(k, n_real, stride=S5_CHUNK), :]
            if n_real < n:
                vv = jnp.concatenate([vv, jnp.zeros((n - n_real, LANES), F32)], axis=0)
            z_scr[gps * j:gps * (j + 1), SSM_GROUP * k:SSM_GROUP * (k + 1), :] = (
                vv.T.reshape(gps, SSM_GROUP, n).astype(BF16))

    chunk_in_seg = lax.broadcasted_iota(jnp.int32, (n, LANES), 0) % seglen

    def group(g, _):
        uct = z_scr[g]
        yt = _dot(toep_ref[g], uct)
        h = lax.dot_general(uct, bpow_ref[g], TN_DIMS, preferred_element_type=F32)
        if use_h0:
            h0_scr[pl.ds(0, h0_ref.shape[1], stride=seglen), :] = h0_ref[g]
            h0 = h0_scr[...]
            h = h + scm_ref[g, 0:1, :] * h0 + scm_ref[g, 8:9, :] * pltpu.roll(h0, SSM_STATE, axis=1)
        d, j = 1, 0
        while d < seglen:
            h_scr[SCAN_PAD:SCAN_PAD + n, :] = h
            hs = h_scr[pl.ds(SCAN_PAD - d, n), :]
            if seglen < n:
                hs = jnp.where(chunk_in_seg >= d, hs, 0.0)
            h = h + scm_ref[g, j:j + 1, :] * hs + scm_ref[g, 8 + j:9 + j, :] * pltpu.roll(
                hs, SSM_STATE, axis=1)
            d, j = 2 * d, j + 1
        h_scr[SCAN_PAD:SCAN_PAD + n, :] = h
        hp = h_scr[pl.ds(SCAN_PAD - 1, n), :]
        if seglen < n:
            hp = jnp.where(chunk_in_seg >= 1, hp, h0 if use_h0 else 0.0)
        yt = yt + _dot_nt(cpow_ref[g], hp.astype(BF16))
        zt_scr[g] = _gelu_tanh(yt)
        hfin_ref[0, g] = h[fin_row0:fin_row0 + 32, :]
        return 0

    lax.fori_loop(0, N_GROUPS, group, 0)

    for k in range(S5_CHUNK):
        slabs = []
        for j in range(N_U_SLABS):
            w = zt_scr[gps * j:gps * (j + 1), SSM_GROUP * k:SSM_GROUP * (k + 1), :]
            slabs.append(w.reshape(LANES, n).T)
        zk = jnp.concatenate(slabs, axis=1)
        out = zk * _sigmoid(_dot(zk.astype(BF16), wglu_ref[...]))
        for j in range(N_U_SLABS):
            o_ref[0, j, pl.ds(k, n_real, stride=S5_CHUNK), :] = out[0:n_real, LANES * j:LANES * (j + 1)]


def _s5_call(u, toep, bpow, cpow, scm, wglu, h0, layer, n, n_real, seglen, fin_row0):
    nb, _, tt, _ = u.shape
    use_h0 = h0 is not None
    one = pl.Buffered(1)
    tab = lambda rows, cols: pl.BlockSpec((None, N_GROUPS, rows, cols), lambda b: (layer, 0, 0, 0),
                                          pipeline_mode=one)
    in_specs = [
        pl.BlockSpec((1, N_U_SLABS, tt, LANES), lambda b: (b, 0, 0, 0), pipeline_mode=one),
        tab(S5_ROWS, S5_ROWS), tab(S5_ROWS, LANES), tab(S5_ROWS, LANES), tab(16, LANES),
        pl.BlockSpec((None, D_SSM, D_SSM), lambda b: (layer, 0, 0), pipeline_mode=one),
    ]
    args = [u, toep, bpow, cpow, scm, wglu]
    scratch = [
        pltpu.VMEM((N_GROUPS, S5_ROWS, n), BF16),
        pltpu.VMEM((N_GROUPS, S5_ROWS, n), F32),
        pltpu.VMEM((SCAN_PAD + n, LANES), F32),
    ]
    if use_h0:
        in_specs.append(tab(h0.shape[2], LANES))
        args.append(h0)
        scratch.append(pltpu.VMEM((n, LANES), F32))
    return pl.pallas_call(
        functools.partial(_s5_kernel, n=n, n_real=n_real, seglen=seglen, fin_row0=fin_row0,
                          use_h0=use_h0),
        grid=(nb,),
        in_specs=in_specs,
        out_specs=[
            pl.BlockSpec((1, N_U_SLABS, tt, LANES), lambda b: (b, 0, 0, 0), pipeline_mode=one),
            pl.BlockSpec((1, N_GROUPS, 32, LANES), lambda b: (b, 0, 0, 0)),
        ],
        out_shape=[
            jax.ShapeDtypeStruct((nb, N_U_SLABS, tt, LANES), F32),
            jax.ShapeDtypeStruct((nb, N_GROUPS, 32, LANES), F32),
        ],
        scratch_shapes=scratch,
        compiler_params=pltpu.CompilerParams(
            dimension_semantics=("arbitrary",), vmem_limit_bytes=VMEM_LIMIT),
        name="s5_mix",
    )(*args)


def _out_ffn_kernel(x_ref, att_ref, ssm_ref, g1_ref, sh2_ref, sc2_ref, g2_ref,
                    gpm_ref, gpf_ref, gpo_ref, wo_ref, wg_ref, wu_ref, wd_ref, o_ref):
    mix = jnp.concatenate(
        [att_ref[0]] + [ssm_ref[0, j].astype(BF16) for j in range(N_U_SLABS)], axis=1)
    o = _dot(mix, wo_ref[...])
    x1 = x_ref[0] + g1_ref[0] * _rms(o, gpm_ref[...])
    hf = (_rms(x1, gpf_ref[...]) * (1.0 + sc2_ref[0]) + sh2_ref[0]).astype(BF16)
    gt = _dot(hf, wg_ref[...])
    up = _dot(hf, wu_ref[...])
    hid = (gt * _sigmoid(gt) * up).astype(BF16)
    f = _dot(hid, wd_ref[...])
    o_ref[0] = x1 + g2_ref[0] * _rms(f, gpo_ref[...])


def _out_ffn_call(x, att, ssm, ssm_flat, modr, layer, row0, g_post_mix, g_pre_ffn, g_post_ffn,
                  w_out, w_gate, w_up, w_down, tb):
    bsz, t, _ = x.shape
    nblk = t // tb
    one = pl.Buffered(1)
    const = lambda b, i: (0, 0)
    lay = lambda b, i: (layer, 0, 0)
    if ssm_flat:
        ssm_spec = pl.BlockSpec((1, N_U_SLABS, tb, LANES), lambda b, i: (0, 0, b * nblk + i, 0))
    else:
        ssm_spec = pl.BlockSpec((1, N_U_SLABS, tb, LANES), lambda b, i: (b, 0, i, 0))
    row = pl.BlockSpec((1, D_MODEL), const)
    return pl.pallas_call(
        _out_ffn_kernel,
        grid=(bsz, nblk),
        in_specs=[
            pl.BlockSpec((1, tb, D_MODEL), lambda b, i: (b, i, 0)),
            pl.BlockSpec((1, tb, D_ATT), lambda b, i: (b, i, 0)),
            ssm_spec,
            _mod_spec(layer, row0, 2), _mod_spec(layer, row0, 3), _mod_spec(layer, row0, 4),
            _mod_spec(layer, row0, 5),
            row, row, row,
            pl.BlockSpec((None, D_MODEL, D_MODEL), lay, pipeline_mode=one),
            pl.BlockSpec((None, D_MODEL, D_FF), lay, pipeline_mode=one),
            pl.BlockSpec((None, D_MODEL, D_FF), lay, pipeline_mode=one),
            pl.BlockSpec((None, D_FF, D_MODEL), lay, pipeline_mode=one),
        ],
        out_specs=pl.BlockSpec((1, tb, D_MODEL), lambda b, i: (b, i, 0)),
        out_shape=jax.ShapeDtypeStruct((bsz, t, D_MODEL), F32),
        compiler_params=pltpu.CompilerParams(
            dimension_semantics=("arbitrary", "arbitrary"), vmem_limit_bytes=VMEM_LIMIT),
        name="out_ffn",
    )(x, att, ssm, modr, modr, modr, modr, g_post_mix, g_pre_ffn, g_post_ffn,
      w_out, w_gate, w_up, w_down)


def kernel(x_prompt, x_sample, c_prompt, c_sample, cache_k, cache_v, cache_logf, state_ssm_re,
           state_ssm_im, w_ada, b_ada, g_pre_mix, g_post_mix, g_pre_ffn, g_post_ffn, w_in, b_forget,
           ssm_a_re, ssm_a_im, ssm_log_dt, ssm_b_re, ssm_b_im, ssm_c_re, ssm_c_im, ssm_d, w_glu,
           w_out, w_gate, w_up, w_down):
    depth = w_in.shape[0]
    bp, tp, _ = x_prompt.shape
    bs, ts, _ = x_sample.shape
    n_att = 3 * D_ATT + N_HEADS

    c_all = jnp.concatenate([c_prompt, c_sample, jnp.zeros((16 - bp - bs, D_MODEL), F32)], axis=0)
    mod = _ada_call(c_all, w_ada, b_ada)
    modr = mod.reshape(depth * 16 * 6, 1, D_MODEL)

    w_in_t = jnp.swapaxes(w_in, 1, 2)
    w_in_t = jnp.concatenate(
        [w_in_t[:, :n_att], jnp.zeros((depth, GATE_PAD - N_HEADS, D_MODEL), F32), w_in_t[:, n_att:]],
        axis=1).astype(BF16)
    bf_p = jnp.pad(b_forget, ((0, 0), (0, GATE_PAD - N_HEADS)))
    w_glu_b, w_out_b = w_glu.astype(BF16), w_out.astype(BF16)
    w_gate_b, w_up_b, w_down_b = w_gate.astype(BF16), w_up.astype(BF16), w_down.astype(BF16)
    toep, bpow, cpow, scm = _s5_tables(ssm_a_re, ssm_a_im, ssm_log_dt, jnp.swapaxes(ssm_b_re, 2, 3),
                                       jnp.swapaxes(ssm_b_im, 2, 3), ssm_c_re, ssm_c_im, ssm_d)
    cache_kt, cache_vt = jnp.swapaxes(cache_k, 3, 4), jnp.swapaxes(cache_v, 3, 4)
    h0 = jnp.concatenate([state_ssm_re, state_ssm_im], axis=-1).transpose(0, 2, 1, 3)

    n_chunk_p = tp // S5_CHUNK
    seg_s = ts // S5_CHUNK
    yp, ys = x_prompt, x_sample
    kvp = None
    lfp, ssp, kss, vss, lfs, sss = [], [], [], [], [], []
    for l in range(depth):
        qa, ka, va, kbuf, vbuf, lf, u, stats = _inproj_call(
            yp, modr, l, 0, g_pre_mix[l:l + 1], w_in_t, bf_p[l:l + 1], kvp, depth, False, 512)
        kvp = (kbuf, vbuf)
        lfp.append(lf)
        att = _attn_call(qa, ka, va, stats, 512, 4)
        ssm, hfin = _s5_call(u, toep, bpow, cpow, scm, w_glu_b, None, l, n_chunk_p, n_chunk_p,
                             n_chunk_p, n_chunk_p - 32)
        ssp.append(hfin[:, :, 31, :])
        yp = _out_ffn_call(yp, att, ssm, False, modr, l, 0, g_post_mix[l:l + 1], g_pre_ffn[l:l + 1],
                           g_post_ffn[l:l + 1], w_out_b, w_gate_b, w_up_b, w_down_b, 512)

        qa, ka, va, k_s, v_s, lf, u, _ = _inproj_call(
            ys, modr, l, bp, g_pre_mix[l:l + 1], w_in_t, bf_p[l:l + 1], None, None, True, ts)
        kss.append(k_s)
        vss.append(v_s)
        lfs.append(lf)
        att = _attn_sample_call(qa, ka, va, cache_kt, cache_vt, cache_logf, l)
        ssm, hfin = _s5_call(u, toep, bpow, cpow, scm, w_glu_b, h0, l, LANES, bs * seg_s, seg_s, 0)
        sss.append(hfin[0, :, seg_s - 1:bs * seg_s:seg_s, :].transpose(1, 0, 2))
        ys = _out_ffn_call(ys, att, ssm, True, modr, l, bp, g_post_mix[l:l + 1], g_pre_ffn[l:l + 1],
                           g_post_ffn[l:l + 1], w_out_b, w_gate_b, w_up_b, w_down_b, ts)

    ssp, sss = jnp.stack(ssp), jnp.stack(sss)
    return (yp, ys, jnp.swapaxes(kvp[0], 3, 4), jnp.swapaxes(kvp[1], 3, 4), jnp.stack(lfp),
            ssp[..., :SSM_STATE], ssp[..., SSM_STATE:],
            jnp.stack(kss), jnp.stack(vss), jnp.stack(lfs),
            sss[..., :SSM_STATE], sss[..., SSM_STATE:])
```

```python
import functools
import math

import numpy as np
import jax
import jax.numpy as jnp
from jax import lax
from jax.experimental import pallas as pl
from jax.experimental.pallas import tpu as pltpu

F32 = jnp.float32
BF16 = jnp.bfloat16

D_MODEL = 1024
N_HEADS = 8
HEAD_DIM = 64
D_ATT = N_HEADS * HEAD_DIM
D_SSM = D_MODEL - D_ATT
SSM_GROUP = 16
N_GROUPS = D_SSM // SSM_GROUP
SSM_STATE = 64
D_FF = 2816
EPS = 1e-6

LANES = 128
S5_CHUNK = 16
S5_ROWS = S5_CHUNK * SSM_GROUP
GATE_PAD = LANES
W_IN_COLS = 3 * D_ATT + GATE_PAD + D_SSM
U_COL0 = 3 * D_ATT + GATE_PAD
N_U_SLABS = D_SSM // LANES
SCAN_PAD = 128
S5_PAIRS_PER_ITER = 2
VMEM_LIMIT = 56 * 1024 * 1024
MASK_VALUE = -1e30
LOG2E = math.log2(math.e)
DEN_LANE = HEAD_DIM
SCORE_LIMIT = 80.0
NORM_SLACK = 1.02

NT_DIMS = (((1,), (1,)), ((), ()))
TN_DIMS = (((0,), (0,)), ((), ()))


def _sigmoid(x):
    return 1.0 / (1.0 + jnp.exp(-x))


def _split3(x):
    hi = x.astype(BF16)
    r = x - hi.astype(F32)
    mid = r.astype(BF16)
    lo = (r - mid.astype(F32)).astype(BF16)
    return hi, mid, lo


def _split2(x):
    hi = x.astype(BF16)
    return hi, (x - hi.astype(F32)).astype(BF16)


def _dot(a, b):
    return jnp.dot(a, b, preferred_element_type=F32)


def _dot_nt(a, b):
    return lax.dot_general(a, b, NT_DIMS, preferred_element_type=F32)


def _rms(x, g):
    ms = jnp.mean(x * x, axis=-1, keepdims=True)
    return x * lax.rsqrt(ms + EPS) * g


def _per_row(ref, nb, tb):
    m = ref[...]
    if nb == 1:
        return m.reshape(1, D_MODEL)
    return jnp.broadcast_to(m[:, None, :], (nb, tb, D_MODEL)).reshape(nb * tb, D_MODEL)


def _ada_kernel(c_ref, w_ref, b_ref, o_ref):
    c = c_ref[...]
    a_hi, a_lo = _split2(c * _sigmoid(c))
    w_hi, w_lo = _split2(w_ref[0])
    o_ref[0] = _dot(a_hi, w_hi) + _dot(a_lo, w_hi) + _dot(a_hi, w_lo) + b_ref[0]


def _ada_call(c_all, w_ada, b_ada):
    depth = w_ada.shape[0]
    nb = 1536
    n_out = w_ada.shape[2]
    return pl.pallas_call(
        _ada_kernel,
        grid=(depth, n_out // nb),
        in_specs=[
            pl.BlockSpec((16, D_MODEL), lambda l, j: (0, 0)),
            pl.BlockSpec((1, D_MODEL, nb), lambda l, j: (l, 0, j)),
            pl.BlockSpec((1, 1, nb), lambda l, j: (l, 0, j)),
        ],
        out_specs=pl.BlockSpec((1, 16, nb), lambda l, j: (l, 0, j)),
        out_shape=jax.ShapeDtypeStruct((depth, 16, n_out), F32),
        compiler_params=pltpu.CompilerParams(
            dimension_semantics=("arbitrary", "arbitrary"), vmem_limit_bytes=VMEM_LIMIT),
        name="ada",
    )(c_all, w_ada, b_ada.reshape(depth, 1, n_out))


def _inproj_kernel(x_ref, sh_ref, sc_ref, g_ref, wt_ref, bf_ref, tri_ref, sel_ref, one_ref, eye_ref,
                   go_ref, qa_ref, ka_ref, va_ref, kc_ref, vc_ref, lf_ref, u_ref, st_ref, carry_ref,
                   *, nb, tb, kv_transposed):
    rows = nb * tb

    @pl.when(pl.program_id(1) == 0)
    def _():
        carry_ref[...] = jnp.zeros_like(carry_ref)

    x = x_ref[...].reshape(rows, D_MODEL)
    hm = _rms(x, g_ref[...]) * (1.0 + _per_row(sc_ref, nb, tb)) + _per_row(sh_ref, nb, tb)
    proj = _dot_nt(hm.astype(BF16), wt_ref[...])
    q = proj[:, 0:D_ATT] * (LOG2E * HEAD_DIM ** -0.5)
    k = proj[:, D_ATT:2 * D_ATT]
    v = proj[:, 2 * D_ATT:3 * D_ATT]
    gate = proj[:, 3 * D_ATT:U_COL0] + bf_ref[...]
    logf = jnp.minimum(gate, 0.0) - jnp.log1p(jnp.exp(-jnp.abs(gate)))
    lane = lax.broadcasted_iota(jnp.int32, (rows, LANES), 1)

    def pack3(val):
        hi, mid, lo = (part.astype(F32) for part in _split3(val))
        packed = jnp.where(lane < 2 * N_HEADS, pltpu.roll(mid, N_HEADS, axis=1),
                           jnp.where(lane < 3 * N_HEADS, pltpu.roll(lo, 2 * N_HEADS, axis=1), 0.0))
        return jnp.where(lane < N_HEADS, hi, packed).astype(BF16)

    lf_pack = pack3(logf)
    cs = _dot(tri_ref[...], lf_pack)
    fcum = (cs + pltpu.roll(cs, LANES - N_HEADS, axis=1) + pltpu.roll(cs, LANES - 2 * N_HEADS, axis=1)
            + carry_ref[...])
    carry_ref[...] = fcum[rows - 1:rows, :]

    extra = _dot(pack3(fcum * LOG2E), sel_ref[...]) + one_ref[...]
    qn2 = _dot((q * q).astype(BF16), go_ref[...])
    kn2 = _dot((k * k).astype(BF16), go_ref[...])
    st_ref[0, 0] = jnp.concatenate(
        [jnp.max(qn2, axis=0, keepdims=True), jnp.max(kn2, axis=0, keepdims=True),
         jnp.zeros((6, LANES), F32)], axis=0)
    lo_half = lane < HEAD_DIM
    den_one = jnp.where(lane == DEN_LANE, 1.0, 0.0)
    if kv_transposed:
        kt, vt = k.T, v.T
    for h in range(N_HEADS):
        pair = slice(LANES * (h // 2), LANES * (h // 2) + LANES)
        q2, k2, v2 = q[:, pair], k[:, pair], v[:, pair]
        if h % 2:
            q2 = pltpu.roll(q2, HEAD_DIM, axis=1)
            k2 = pltpu.roll(k2, HEAD_DIM, axis=1)
            v2 = pltpu.roll(v2, HEAD_DIM, axis=1)
        per_stream = lambda val: val.reshape(nb, tb, val.shape[-1])
        qa_ref[:, h] = per_stream(
            jnp.where(lo_half, q2, extra[:, LANES * h:LANES * (h + 1)]).astype(BF16))
        ka_ref[:, h] = per_stream(jnp.where(
            lo_half, k2, extra[:, D_MODEL + LANES * h:D_MODEL + LANES * (h + 1)]).astype(BF16))
        va_ref[:, h] = per_stream(jnp.where(lo_half, v2, den_one).astype(BF16))
        if kv_transposed:
            kc_ref[h] = kt[HEAD_DIM * h:HEAD_DIM * (h + 1), :]
            vc_ref[h] = vt[HEAD_DIM * h:HEAD_DIM * (h + 1), :]
        else:
            kc_ref[:, h] = per_stream(k2[:, 0:HEAD_DIM])
            vc_ref[:, h] = per_stream(v2[:, 0:HEAD_DIM])
    lf_t = _dot_nt(eye_ref[...], lf_pack)
    lf_ref[0] = lf_t[0:N_HEADS] + lf_t[N_HEADS:2 * N_HEADS] + lf_t[2 * N_HEADS:3 * N_HEADS]
    for j in range(N_U_SLABS):
        u_ref[0, j] = proj[:, U_COL0 + LANES * j:U_COL0 + LANES * (j + 1)]


def _inproj_consts(nb, tb):
    tri = np.kron(np.eye(nb, dtype=np.float32), np.tril(np.ones((tb, tb), np.float32)))
    sel = np.zeros((LANES, 2 * D_MODEL), np.float32)
    one = np.zeros((1, 2 * D_MODEL), np.float32)
    group_ones = np.zeros((D_ATT, LANES), np.float32)
    for h in range(N_HEADS):
        group_ones[HEAD_DIM * h:HEAD_DIM * (h + 1), h] = 1.0
        for p in range(3):
            sel[p * N_HEADS + h, LANES * h + HEAD_DIM + p] = 1.0
            sel[p * N_HEADS + h, D_MODEL + LANES * h + HEAD_DIM + 3 + p] = -1.0
            one[0, LANES * h + HEAD_DIM + 3 + p] = 1.0
            one[0, D_MODEL + LANES * h + HEAD_DIM + p] = 1.0
    eye = np.eye(4 * N_HEADS, LANES, dtype=np.float32)
    eye[3 * N_HEADS:] = 0.0
    return (jnp.asarray(tri, BF16), jnp.asarray(sel, BF16), jnp.asarray(one, F32), jnp.asarray(eye, BF16),
            jnp.asarray(group_ones, BF16))


def _mod_spec(layer, row0, chunk, nb=1):
    if nb == 1:
        return pl.BlockSpec((1, 1, D_MODEL), lambda b, i: ((layer * 16 + row0 + b) * 6 + chunk, 0, 0))
    return pl.BlockSpec((None, None, nb, D_MODEL), lambda b, i: (layer, chunk, 0, 0))


def _inproj_call(x, mods, layer, row0, g_pre, w_in_t, b_forget, kv_bufs, kv_depth, nb, tb):
    bsz, t, _ = x.shape
    nblk = t // tb
    rows = nb * tb
    assert nb == 1 or (nblk == 1 and kv_depth is None)
    tri, sel, one, eye, group_ones = _inproj_consts(nb, tb)
    if nb > 1:
        u_shape = (1, N_U_SLABS, bsz * t, LANES)
        u_spec = pl.BlockSpec((1, N_U_SLABS, rows, LANES), lambda b, i: (0, 0, b, 0))
        lf_shape = (1, N_HEADS, bsz * t)
        lf_spec = pl.BlockSpec((1, N_HEADS, rows), lambda b, i: (0, 0, b))
    else:
        u_shape = (bsz, N_U_SLABS, t, LANES)
        u_spec = pl.BlockSpec((1, N_U_SLABS, tb, LANES), lambda b, i: (b, 0, i, 0))
        lf_shape = (bsz, N_HEADS, t)
        lf_spec = pl.BlockSpec((1, N_HEADS, tb), lambda b, i: (b, 0, i))
    const = lambda b, i: (0, 0)
    in_specs = [
        pl.BlockSpec((nb, tb, D_MODEL), lambda b, i: (b, i, 0)),
        _mod_spec(layer, row0, 0, nb), _mod_spec(layer, row0, 1, nb),
        pl.BlockSpec((1, D_MODEL), const),
        pl.BlockSpec((None, W_IN_COLS, D_MODEL), lambda b, i: (layer, 0, 0)),
        pl.BlockSpec((1, GATE_PAD), const),
        pl.BlockSpec((rows, rows), const),
        pl.BlockSpec((LANES, 2 * D_MODEL), const),
        pl.BlockSpec((1, 2 * D_MODEL), const),
        pl.BlockSpec((4 * N_HEADS, LANES), const),
        pl.BlockSpec((D_ATT, LANES), const),
    ]
    args = [x, mods, mods, g_pre, w_in_t, b_forget, tri, sel, one, eye, group_ones]
    n_main = len(args)
    head_blk = pl.BlockSpec((nb, N_HEADS, tb, LANES), lambda b, i: (b, 0, i, 0))
    aliases = {}
    if kv_depth is None:
        kv_shape = jax.ShapeDtypeStruct((bsz, N_HEADS, t, HEAD_DIM), F32)
        kv_spec = pl.BlockSpec((nb, N_HEADS, tb, HEAD_DIM), lambda b, i: (b, 0, i, 0))
    else:
        kv_shape = jax.ShapeDtypeStruct((kv_depth, bsz, N_HEADS, HEAD_DIM, t), F32)
        kv_spec = pl.BlockSpec((None, None, N_HEADS, HEAD_DIM, tb), lambda b, i: (layer, b, 0, 0, i))
        if kv_bufs is not None:
            in_specs += [pl.BlockSpec(memory_space=pl.ANY), pl.BlockSpec(memory_space=pl.ANY)]
            args += list(kv_bufs)
            aliases = {n_main: 3, n_main + 1: 4}
    out_shape = [
        jax.ShapeDtypeStruct((bsz, N_HEADS, t, LANES), BF16),
        jax.ShapeDtypeStruct((bsz, N_HEADS, t, LANES), BF16),
        jax.ShapeDtypeStruct((bsz, N_HEADS, t, LANES), BF16),
        kv_shape, kv_shape,
        jax.ShapeDtypeStruct(lf_shape, F32),
        jax.ShapeDtypeStruct(u_shape, F32),
        jax.ShapeDtypeStruct((bsz // nb, nblk, 8, LANES), F32),
    ]
    out_specs = [
        head_blk, head_blk, head_blk,
        kv_spec, kv_spec,
        lf_spec,
        u_spec,
        pl.BlockSpec((1, 1, 8, LANES), lambda b, i: (b, i, 0, 0)),
    ]
    n_args = len(args)

    def body(*refs):
        _inproj_kernel(*refs[:n_main], *refs[n_args:], nb=nb, tb=tb, kv_transposed=kv_depth is not None)

    return pl.pallas_call(
        body,
        grid=(bsz // nb, nblk),
        in_specs=in_specs,
        out_specs=out_specs,
        out_shape=out_shape,
        scratch_shapes=[pltpu.VMEM((1, LANES), F32)],
        input_output_aliases=aliases,
        compiler_params=pltpu.CompilerParams(
            dimension_semantics=("arbitrary", "arbitrary"), vmem_limit_bytes=VMEM_LIMIT),
        name="inproj",
    )(*args)


def _finish_heads(accs, lane):
    outs = []
    for acc in accs:
        den = jnp.sum(jnp.where(lane == DEN_LANE, acc, 0.0), axis=1, keepdims=True)
        outs.append(acc * (1.0 / den))
    return jnp.where(lane < HEAD_DIM, outs[0], pltpu.roll(outs[1], HEAD_DIM, axis=1)).astype(BF16)


def _attn_kernel(small_ref, qa_ref, ka_ref, va_ref, o_ref, *, tq, hps):
    i = pl.program_id(2)
    row = lax.broadcasted_iota(jnp.int32, (tq, tq), 0)
    col = lax.broadcasted_iota(jnp.int32, (tq, tq), 1)
    causal = col <= row
    lane = lax.broadcasted_iota(jnp.int32, (tq, LANES), 1)
    qs = [qa_ref[0, hh] for hh in range(hps)]

    def score(j, hh, masked):
        start = pl.multiple_of(j * tq, tq)
        s = _dot_nt(qs[hh], ka_ref[0, hh, pl.ds(start, tq), :])
        return jnp.where(causal, s, MASK_VALUE) if masked else s

    def values(j, hh):
        return va_ref[0, hh, pl.ds(pl.multiple_of(j * tq, tq), tq), :]

    def store(accs):
        for pr in range(hps // 2):
            o_ref[0, :, LANES * pr:LANES * (pr + 1)] = _finish_heads(accs[2 * pr:2 * pr + 2], lane)

    def tile_plain(j, accs, masked):
        return tuple(accs[hh] + _dot(jnp.exp2(score(j, hh, masked)).astype(BF16), values(j, hh))
                     for hh in range(hps))

    def tile_shifted(j, carry, masked):
        new = []
        for hh in range(hps):
            m, acc = carry[hh]
            s = score(j, hh, masked)
            m_new = jnp.maximum(m, jnp.max(s, axis=1, keepdims=True))
            p = jnp.exp2(s - m_new)
            new.append((m_new, jnp.exp2(m - m_new) * acc + _dot(p.astype(BF16), values(j, hh))))
        return tuple(new)

    small = small_ref[pl.program_id(0), pl.program_id(1)] != 0
    zero = jnp.zeros((tq, LANES), F32)

    @pl.when(small)
    def _():
        accs = lax.fori_loop(0, i, functools.partial(tile_plain, masked=False), (zero,) * hps)
        store(tile_plain(i, accs, True))

    @pl.when(jnp.logical_not(small))
    def _():
        init = ((jnp.full((tq, 1), MASK_VALUE, F32), zero),) * hps
        carry = lax.fori_loop(0, i, functools.partial(tile_shifted, masked=False), init)
        store([c[1] for c in tile_shifted(i, carry, True)])


def _attn_call(qa, ka, va, stats, tq, hps):
    bsz, _, t, _ = qa.shape
    st = jnp.max(stats, axis=1)
    bound = jnp.sqrt(st[:, 0, :N_HEADS] * st[:, 1, :N_HEADS]) * NORM_SLACK
    small = (jnp.max(bound.reshape(bsz, N_HEADS // hps, hps), axis=-1) < SCORE_LIMIT).astype(jnp.int32)
    whole = pl.BlockSpec((1, hps, t, LANES), lambda b, p, i, flags: (b, p, 0, 0))
    return pl.pallas_call(
        functools.partial(_attn_kernel, tq=tq, hps=hps),
        grid_spec=pltpu.PrefetchScalarGridSpec(
            num_scalar_prefetch=1,
            grid=(bsz, N_HEADS // hps, t // tq),
            in_specs=[pl.BlockSpec((1, hps, tq, LANES), lambda b, p, i, flags: (b, p, i, 0)),
                      whole, whole],
            out_specs=pl.BlockSpec((1, tq, HEAD_DIM * hps), lambda b, p, i, flags: (b, i, p)),
        ),
        out_shape=jax.ShapeDtypeStruct((bsz, t, D_ATT), BF16),
        compiler_params=pltpu.CompilerParams(
            dimension_semantics=("arbitrary", "arbitrary", "arbitrary"),
            vmem_limit_bytes=VMEM_LIMIT),
        name="fox_attn",
    )(small, qa, ka, va)


def _attn_sample_kernel(qa_ref, ka_ref, va_ref, ckt_ref, cvt_ref, clf_ref, o_ref, *, t, n_past):
    lane_p = lax.broadcasted_iota(jnp.int32, (N_HEADS, n_past), 1)
    c = clf_ref[...]
    total = jnp.sum(c, axis=1, keepdims=True)
    d = 1
    while d < n_past:
        c = c + jnp.where(lane_p >= d, pltpu.roll(c, d, axis=1), 0.0)
        d *= 2
    f_past = (c - total) * LOG2E
    lane = lax.broadcasted_iota(jnp.int32, (t, LANES), 1)
    causal = (lax.broadcasted_iota(jnp.int32, (t, t), 1) <= lax.broadcasted_iota(jnp.int32, (t, t), 0))
    den_rows = jnp.where(lax.broadcasted_iota(jnp.int32, (LANES - HEAD_DIM, n_past), 0) == 0, 1.0, 0.0)
    accs = []
    for h in range(N_HEADS):
        q = qa_ref[0, h]
        fq = jnp.sum(jnp.where((lane >= HEAD_DIM) & (lane < HEAD_DIM + 3), q.astype(F32), 0.0),
                     axis=1, keepdims=True)
        s_past = _dot(q[:, 0:HEAD_DIM], ckt_ref[h].astype(BF16)) + fq - f_past[h:h + 1, :]
        s_new = jnp.where(causal, _dot_nt(q, ka_ref[0, h]), MASK_VALUE)
        m = jnp.maximum(jnp.max(s_past, axis=1, keepdims=True), jnp.max(s_new, axis=1, keepdims=True))
        p_past = jnp.exp2(s_past - m).astype(BF16)
        p_new = jnp.exp2(s_new - m).astype(BF16)
        cvt_aug = jnp.concatenate([cvt_ref[h], den_rows], axis=0).astype(BF16)
        accs.append(_dot_nt(p_past, cvt_aug) + _dot(p_new, va_ref[0, h]))
    for pr in range(N_HEADS // 2):
        o_ref[0, :, LANES * pr:LANES * (pr + 1)] = _finish_heads(accs[2 * pr:2 * pr + 2], lane)


def _attn_sample_call(qa, ka, va, cache_kt, cache_vt, cache_logf, layer):
    bsz, _, t, _ = qa.shape
    n_past = cache_kt.shape[4]
    cache_spec = pl.BlockSpec((None, None, N_HEADS, HEAD_DIM, n_past), lambda b: (layer, b, 0, 0, 0))
    new_spec = pl.BlockSpec((1, N_HEADS, t, LANES), lambda b: (b, 0, 0, 0))
    return pl.pallas_call(
        functools.partial(_attn_sample_kernel, t=t, n_past=n_past),
        grid=(bsz,),
        in_specs=[
            new_spec, new_spec, new_spec, cache_spec, cache_spec,
            pl.BlockSpec((None, None, N_HEADS, n_past), lambda b: (layer, b, 0, 0)),
        ],
        out_specs=pl.BlockSpec((1, t, D_ATT), lambda b: (b, 0, 0)),
        out_shape=jax.ShapeDtypeStruct((bsz, t, D_ATT), BF16),
        compiler_params=pltpu.CompilerParams(
            dimension_semantics=("arbitrary",), vmem_limit_bytes=VMEM_LIMIT),
        name="fox_attn_sample",
    )(qa, ka, va, cache_kt, cache_vt, cache_logf)


def _s5_table_kernel(ca_ref, cb_ref, ba_ref, bb_ref, prr_ref, pii_ref, dv_ref,
                     toep_ref, bpow_ref, cpow_ref, tt_scr, *, gb):
    row = lax.broadcasted_iota(jnp.int32, (SSM_GROUP, S5_ROWS), 0)
    lane = lax.broadcasted_iota(jnp.int32, (SSM_GROUP, S5_ROWS), 1)
    for gi in range(gb):
        ca, cb, ba, bb = ca_ref[gi], cb_ref[gi], ba_ref[gi], bb_ref[gi]
        power = lambda a, b, n: a * prr_ref[gi, n:n + 1, :] + b * pii_ref[gi, n:n + 1, :]
        xs = [power(ca, cb, n) for n in range(S5_CHUNK + 1)]
        cpow_ref[gi] = jnp.concatenate(xs[1:], axis=0).astype(BF16)
        bpow_ref[gi] = jnp.concatenate(
            [power(ba, bb, S5_CHUNK - 1 - s) for s in range(S5_CHUNK)], axis=0).T.astype(BF16)
        x_hi, x_lo = _split2(jnp.concatenate(xs[:S5_CHUNK], axis=0))
        b_hi, b_lo = _split2(ba)
        r0 = _dot_nt(b_hi, x_hi) + _dot_nt(b_lo, x_hi) + _dot_nt(b_hi, x_lo)
        r0 = r0 + jnp.where(lane == row, dv_ref[gi], 0.0)
        for s in range(S5_CHUNK):
            blk = r0
            if s:
                blk = jnp.where(lane >= SSM_GROUP * s, pltpu.roll(r0, SSM_GROUP * s, axis=1), 0.0)
            tt_scr[SSM_GROUP * s:SSM_GROUP * (s + 1), :] = blk
        toep_ref[gi] = tt_scr[...].T.astype(BF16)


def _s5_tables(a_re, a_im, log_dt, b_re_t, b_im_t, c_re, c_im, d_skip):
    depth = a_re.shape[0]
    dt = jnp.exp(log_dt)[..., None]
    x, y = a_re * dt, a_im * dt
    ex, cy, sy = jnp.exp(x), jnp.cos(y), jnp.sin(y)
    ar, ai = ex * cy, ex * sy
    sh = jnp.sin(0.5 * y)
    nr, ni = jnp.expm1(x) * cy - 2.0 * sh * sh, ai
    den = a_re * a_re + a_im * a_im
    fr = ((nr * a_re + ni * a_im) / den)[:, :, None, :]
    fi = ((ni * a_re - nr * a_im) / den)[:, :, None, :]
    bbr, bbi = fr * b_re_t - fi * b_im_t, fr * b_im_t + fi * b_re_t
    pr, pi = [jnp.ones_like(ar)], [jnp.zeros_like(ar)]
    for _ in range(S5_CHUNK):
        pr, pi = pr + [pr[-1] * ar - pi[-1] * ai], pi + [pr[-1] * ai + pi[-1] * ar]
    mr, mi = pr[S5_CHUNK], pi[S5_CHUNK]
    pr, pi = jnp.stack(pr, axis=2), jnp.stack(pi, axis=2)
    cat = lambda a, b: jnp.concatenate([a, b], axis=-1)
    ins = [cat(c_re, -c_im), cat(-c_im, -c_re), cat(bbr, bbi), cat(-bbi, bbr), cat(pr, pr), cat(pi, pi),
           jnp.pad(d_skip.reshape(depth, N_GROUPS, 1, SSM_GROUP), ((0, 0), (0, 0), (0, 0), (0, S5_ROWS - SSM_GROUP)))]
    gb = 8
    spec = lambda rows, cols: pl.BlockSpec((None, gb, rows, cols), lambda l, j: (l, j, 0, 0))
    tab = lambda rows, cols: jax.ShapeDtypeStruct((depth, N_GROUPS, rows, cols), BF16)
    toep, bpow, cpow = pl.pallas_call(
        functools.partial(_s5_table_kernel, gb=gb),
        grid=(depth, N_GROUPS // gb),
        in_specs=[spec(SSM_GROUP, LANES)] * 4 + [spec(S5_CHUNK + 1, LANES)] * 2 + [spec(1, S5_ROWS)],
        out_specs=[spec(S5_ROWS, S5_ROWS), spec(LANES, S5_ROWS), spec(S5_ROWS, LANES)],
        out_shape=[tab(S5_ROWS, S5_ROWS), tab(LANES, S5_ROWS), tab(S5_ROWS, LANES)],
        scratch_shapes=[pltpu.VMEM((S5_ROWS, S5_ROWS), F32)],
        compiler_params=pltpu.CompilerParams(
            dimension_semantics=("arbitrary", "arbitrary"), vmem_limit_bytes=VMEM_LIMIT),
        name="s5_tables",
    )(*ins)
    rows_r, rows_i = [], []
    for _ in range(8):
        rows_r.append(cat(mr, mr))
        rows_i.append(cat(-mi, mi))
        mr, mi = mr * mr - mi * mi, 2.0 * mr * mi
    scm = jnp.stack(rows_r + rows_i, axis=2)
    return toep, bpow, cpow, scm


def _gelu_tanh(y):
    return 0.5 * y * (1.0 + jnp.tanh(math.sqrt(2.0 / math.pi) * (y + 0.044715 * (y * y * y))))


def _s5_kernel(*refs, n, n_real, seglen, fin_row0, use_h0):
    if use_h0:
        u_ref, toep_ref, bpow_ref, cpow_ref, scm_ref, wglu_ref, h0_ref = refs[:7]
        o_ref, hfin_ref, z_scr, zt_scr, h_scr, h0_scr = refs[7:]
    else:
        u_ref, toep_ref, bpow_ref, cpow_ref, scm_ref, wglu_ref = refs[:6]
        o_ref, hfin_ref, z_scr, zt_scr, h_scr = refs[6:]
    gps = LANES // SSM_GROUP
    for pp in range(S5_PAIRS_PER_ITER):
        for part in range(2):
            h_scr[pp, part, 0:SCAN_PAD, :] = jnp.zeros((SCAN_PAD, LANES), F32)
    if use_h0:
        h0_scr[...] = jnp.zeros(h0_scr.shape, F32)

    for k in range(S5_CHUNK):
        for j in range(N_U_SLABS):
            vv = u_ref[0, j, pl.ds(k, n_real, stride=S5_CHUNK), :]
            if n_real < n:
                vv = jnp.concatenate([vv, jnp.zeros((n - n_real, LANES), F32)], axis=0)
            z_scr[gps * j:gps * (j + 1), SSM_GROUP * k:SSM_GROUP * (k + 1), :] = (
                vv.T.reshape(gps, SSM_GROUP, n).astype(BF16))

    chunk_in_seg = lax.broadcasted_iota(jnp.int32, (n, LANES), 0) % seglen
    lo = lax.broadcasted_iota(jnp.int32, (n, LANES), 1) < SSM_STATE
    lo_row = lax.broadcasted_iota(jnp.int32, (1, LANES), 1) < SSM_STATE
    lo_fin = lax.broadcasted_iota(jnp.int32, (32, LANES), 1) < SSM_STATE
    swap = lambda a: pltpu.roll(a, SSM_STATE, axis=1)

    def pair_step(it, _):
        for pp in range(S5_PAIRS_PER_ITER):
            gs = [2 * (it * S5_PAIRS_PER_ITER + pp), 2 * (it * S5_PAIRS_PER_ITER + pp) + 1]
            ucts = [z_scr[g] for g in gs]
            yts = [_dot(toep_ref[g], u) for g, u in zip(gs, ucts)]
            hs = [_dot(bpow_ref[g], u).T for g, u in zip(gs, ucts)]
            h0s = [0.0, 0.0]
            if use_h0:
                for idx, g in enumerate(gs):
                    h0_scr[idx, pl.ds(0, h0_ref.shape[1], stride=seglen), :] = h0_ref[g]
                    h0s[idx] = h0_scr[idx]
                    hs[idx] = hs[idx] + scm_ref[g, 0:1, :] * h0s[idx] + scm_ref[g, 8:9, :] * swap(h0s[idx])
            hre = jnp.where(lo, hs[0], swap(hs[1]))
            him = jnp.where(lo, swap(hs[0]), hs[1])

            def shifted(by):
                h_scr[pp, 0, SCAN_PAD:SCAN_PAD + n, :] = hre
                h_scr[pp, 1, SCAN_PAD:SCAN_PAD + n, :] = him
                return (h_scr[pp, 0, pl.ds(SCAN_PAD - by, n), :], h_scr[pp, 1, pl.ds(SCAN_PAD - by, n), :])

            d, j = 1, 0
            while d < seglen:
                sre, sim = shifted(d)
                if seglen < n:
                    sre = jnp.where(chunk_in_seg >= d, sre, 0.0)
                    sim = jnp.where(chunk_in_seg >= d, sim, 0.0)
                ar = jnp.where(lo_row, scm_ref[gs[0], j:j + 1, :], scm_ref[gs[1], j:j + 1, :])
                ai = jnp.where(lo_row, -scm_ref[gs[0], 8 + j:9 + j, :], scm_ref[gs[1], 8 + j:9 + j, :])
                hre, him = hre + ar * sre - ai * sim, him + ar * sim + ai * sre
                d, j = 2 * d, j + 1
            pre, pim = shifted(1)
            hps = [jnp.where(lo, pre, swap(pim)), jnp.where(lo, swap(pre), pim)]
            fre, fim = hre[fin_row0:fin_row0 + 32, :], him[fin_row0:fin_row0 + 32, :]
            fins = [jnp.where(lo_fin, fre, swap(fim)), jnp.where(lo_fin, swap(fre), fim)]
            for idx, g in enumerate(gs):
                hp = hps[idx]
                if seglen < n:
                    hp = jnp.where(chunk_in_seg >= 1, hp, h0s[idx])
                zt_scr[g] = _gelu_tanh(yts[idx] + _dot_nt(cpow_ref[g], hp.astype(BF16)))
                hfin_ref[0, g] = fins[idx]
        return 0

    lax.fori_loop(0, N_GROUPS // (2 * S5_PAIRS_PER_ITER), pair_step, 0)

    for k in range(S5_CHUNK):
        slabs = []
        for j in range(N_U_SLABS):
            w = zt_scr[gps * j:gps * (j + 1), SSM_GROUP * k:SSM_GROUP * (k + 1), :]
            slabs.append(w.reshape(LANES, n).T)
        zk = jnp.concatenate(slabs, axis=1)
        out = zk * _sigmoid(_dot(zk.astype(BF16), wglu_ref[...]))
        for j in range(N_U_SLABS):
            o_ref[0, j, pl.ds(k, n_real, stride=S5_CHUNK), :] = out[0:n_real, LANES * j:LANES * (j + 1)]


def _s5_call(u, toep, bpow, cpow, scm, wglu, h0, layer, n, n_real, seglen, fin_row0):
    nb, _, tt, _ = u.shape
    use_h0 = h0 is not None
    one = pl.Buffered(1)
    tab = lambda rows, cols: pl.BlockSpec((None, N_GROUPS, rows, cols), lambda b: (layer, 0, 0, 0),
                                          pipeline_mode=one)
    in_specs = [
        pl.BlockSpec((1, N_U_SLABS, tt, LANES), lambda b: (b, 0, 0, 0), pipeline_mode=one),
        tab(S5_ROWS, S5_ROWS), tab(LANES, S5_ROWS), tab(S5_ROWS, LANES), tab(16, LANES),
        pl.BlockSpec((None, D_SSM, D_SSM), lambda b: (layer, 0, 0), pipeline_mode=one),
    ]
    args = [u, toep, bpow, cpow, scm, wglu]
    scratch = [
        pltpu.VMEM((N_GROUPS, S5_ROWS, n), BF16),
        pltpu.VMEM((N_GROUPS, S5_ROWS, n), F32),
        pltpu.VMEM((S5_PAIRS_PER_ITER, 2, SCAN_PAD + n, LANES), F32),
    ]
    if use_h0:
        in_specs.append(tab(h0.shape[2], LANES))
        args.append(h0)
        scratch.append(pltpu.VMEM((2, n, LANES), F32))
    return pl.pallas_call(
        functools.partial(_s5_kernel, n=n, n_real=n_real, seglen=seglen, fin_row0=fin_row0,
                          use_h0=use_h0),
        grid=(nb,),
        in_specs=in_specs,
        out_specs=[
            pl.BlockSpec((1, N_U_SLABS, tt, LANES), lambda b: (b, 0, 0, 0), pipeline_mode=one),
            pl.BlockSpec((1, N_GROUPS, 32, LANES), lambda b: (b, 0, 0, 0)),
        ],
        out_shape=[
            jax.ShapeDtypeStruct((nb, N_U_SLABS, tt, LANES), F32),
            jax.ShapeDtypeStruct((nb, N_GROUPS, 32, LANES), F32),
        ],
        scratch_shapes=scratch,
        compiler_params=pltpu.CompilerParams(
            dimension_semantics=("arbitrary",), vmem_limit_bytes=VMEM_LIMIT),
        name="s5_mix",
    )(*args)


def _out_ffn_kernel(x_ref, att_ref, ssm_ref, g1_ref, sh2_ref, sc2_ref, g2_ref,
                    gpm_ref, gpf_ref, gpo_ref, wo_ref, wg_ref, wu_ref, wd_ref, o_ref, *, nb, tb):
    rows = nb * tb
    mix = jnp.concatenate(
        [att_ref[...].reshape(rows, D_ATT)] + [ssm_ref[0, j].astype(BF16) for j in range(N_U_SLABS)],
        axis=1)
    o = _dot(mix, wo_ref[...])
    x1 = x_ref[...].reshape(rows, D_MODEL) + _per_row(g1_ref, nb, tb) * _rms(o, gpm_ref[...])
    hf = (_rms(x1, gpf_ref[...]) * (1.0 + _per_row(sc2_ref, nb, tb)) + _per_row(sh2_ref, nb, tb)).astype(BF16)
    gt = _dot(hf, wg_ref[...])
    up = _dot(hf, wu_ref[...])
    hid = (gt * _sigmoid(gt) * up).astype(BF16)
    f = _dot(hid, wd_ref[...])
    o_ref[...] = (x1 + _per_row(g2_ref, nb, tb) * _rms(f, gpo_ref[...])).reshape(nb, tb, D_MODEL)


def _out_ffn_call(x, att, ssm, mods, layer, row0, g_post_mix, g_pre_ffn, g_post_ffn,
                  w_out, w_gate, w_up, w_down, nb, tb):
    bsz, t, _ = x.shape
    nblk = t // tb
    one = pl.Buffered(1)
    const = lambda b, i: (0, 0)
    lay = lambda b, i: (layer, 0, 0)
    if nb > 1:
        ssm_spec = pl.BlockSpec((1, N_U_SLABS, nb * tb, LANES), lambda b, i: (0, 0, b, 0))
    else:
        ssm_spec = pl.BlockSpec((1, N_U_SLABS, tb, LANES), lambda b, i: (b, 0, i, 0))
    row = pl.BlockSpec((1, D_MODEL), const)
    return pl.pallas_call(
        functools.partial(_out_ffn_kernel, nb=nb, tb=tb),
        grid=(bsz // nb, nblk),
        in_specs=[
            pl.BlockSpec((nb, tb, D_MODEL), lambda b, i: (b, i, 0)),
            pl.BlockSpec((nb, tb, D_ATT), lambda b, i: (b, i, 0)),
            ssm_spec,
            _mod_spec(layer, row0, 2, nb), _mod_spec(layer, row0, 3, nb), _mod_spec(layer, row0, 4, nb),
            _mod_spec(layer, row0, 5, nb),
            row, row, row,
            pl.BlockSpec((None, D_MODEL, D_MODEL), lay, pipeline_mode=one),
            pl.BlockSpec((None, D_MODEL, D_FF), lay, pipeline_mode=one),
            pl.BlockSpec((None, D_MODEL, D_FF), lay, pipeline_mode=one),
            pl.BlockSpec((None, D_FF, D_MODEL), lay, pipeline_mode=one),
        ],
        out_specs=pl.BlockSpec((nb, tb, D_MODEL), lambda b, i: (b, i, 0)),
        out_shape=jax.ShapeDtypeStruct((bsz, t, D_MODEL), F32),
        compiler_params=pltpu.CompilerParams(
            dimension_semantics=("arbitrary", "arbitrary"), vmem_limit_bytes=VMEM_LIMIT),
        name="out_ffn",
    )(x, att, ssm, mods, mods, mods, mods, g_post_mix, g_pre_ffn, g_post_ffn,
      w_out, w_gate, w_up, w_down)


def kernel(x_prompt, x_sample, c_prompt, c_sample, cache_k, cache_v, cache_logf, state_ssm_re,
           state_ssm_im, w_ada, b_ada, g_pre_mix, g_post_mix, g_pre_ffn, g_post_ffn, w_in, b_forget,
           ssm_a_re, ssm_a_im, ssm_log_dt, ssm_b_re, ssm_b_im, ssm_c_re, ssm_c_im, ssm_d, w_glu,
           w_out, w_gate, w_up, w_down):
    depth = w_in.shape[0]
    bp, tp, _ = x_prompt.shape
    bs, ts, _ = x_sample.shape
    n_att = 3 * D_ATT + N_HEADS

    c_all = jnp.concatenate([c_prompt, c_sample, jnp.zeros((16 - bp - bs, D_MODEL), F32)], axis=0)
    mod = _ada_call(c_all, w_ada, b_ada)
    modr = mod.reshape(depth * 16 * 6, 1, D_MODEL)
    mod_s = mod[:, bp:bp + bs].reshape(depth, bs, 6, D_MODEL).transpose(0, 2, 1, 3)

    w_in_t = jnp.swapaxes(w_in, 1, 2)
    w_in_t = jnp.concatenate(
        [w_in_t[:, :n_att], jnp.zeros((depth, GATE_PAD - N_HEADS, D_MODEL), F32), w_in_t[:, n_att:]],
        axis=1).astype(BF16)
    bf_p = jnp.pad(b_forget, ((0, 0), (0, GATE_PAD - N_HEADS)))
    w_glu_b, w_out_b = w_glu.astype(BF16), w_out.astype(BF16)
    w_gate_b, w_up_b, w_down_b = w_gate.astype(BF16), w_up.astype(BF16), w_down.astype(BF16)
    toep, bpow, cpow, scm = _s5_tables(ssm_a_re, ssm_a_im, ssm_log_dt, jnp.swapaxes(ssm_b_re, 2, 3),
                                       jnp.swapaxes(ssm_b_im, 2, 3), ssm_c_re, ssm_c_im, ssm_d)
    cache_kt, cache_vt = jnp.swapaxes(cache_k, 3, 4), jnp.swapaxes(cache_v, 3, 4)
    h0 = jnp.concatenate([state_ssm_re, state_ssm_im], axis=-1).transpose(0, 2, 1, 3)

    n_chunk_p = tp // S5_CHUNK
    seg_s = ts // S5_CHUNK
    yp, ys = x_prompt, x_sample
    kvp = None
    lfp, ssp, kss, vss, lfs, sss = [], [], [], [], [], []
    for l in range(depth):
        qa, ka, va, kbuf, vbuf, lf, u, stats = _inproj_call(
            yp, modr, l, 0, g_pre_mix[l:l + 1], w_in_t, bf_p[l:l + 1], kvp, depth, 1, 512)
        kvp = (kbuf, vbuf)
        lfp.append(lf)
        att = _attn_call(qa, ka, va, stats, 512, 4)
        ssm, hfin = _s5_call(u, toep, bpow, cpow, scm, w_glu_b, None, l, n_chunk_p, n_chunk_p,
                             n_chunk_p, n_chunk_p - 32)
        ssp.append(hfin[:, :, 31, :])
        yp = _out_ffn_call(yp, att, ssm, modr, l, 0, g_post_mix[l:l + 1], g_pre_ffn[l:l + 1],
                           g_post_ffn[l:l + 1], w_out_b, w_gate_b, w_up_b, w_down_b, 1, 512)

        qa, ka, va, k_s, v_s, lf, u, _ = _inproj_call(
            ys, mod_s, l, bp, g_pre_mix[l:l + 1], w_in_t, bf_p[l:l + 1], None, None, bs, ts)
        kss.append(k_s)
        vss.append(v_s)
        lfs.append(lf.reshape(N_HEADS, bs, ts).transpose(1, 0, 2))
        att = _attn_sample_call(qa, ka, va, cache_kt, cache_vt, cache_logf, l)
        ssm, hfin = _s5_call(u, toep, bpow, cpow, scm, w_glu_b, h0, l, LANES, bs * seg_s, seg_s, 0)
        sss.append(hfin[0, :, seg_s - 1:bs * seg_s:seg_s, :].transpose(1, 0, 2))
        ys = _out_ffn_call(ys, att, ssm, mod_s, l, bp, g_post_mix[l:l + 1], g_pre_ffn[l:l + 1],
                           g_post_ffn[l:l + 1], w_out_b, w_gate_b, w_up_b, w_down_b, bs, ts)

    ssp, sss = jnp.stack(ssp), jnp.stack(sss)
    return (yp, ys, jnp.swapaxes(kvp[0], 3, 4), jnp.swapaxes(kvp[1], 3, 4), jnp.stack(lfp),
            ssp[..., :SSM_STATE], ssp[..., SSM_STATE:],
            jnp.stack(kss), jnp.stack(vss), jnp.stack(lfs),
            sss[..., :SSM_STATE], sss[..., SSM_STATE:])
```

```python
import functools
import math

import numpy as np
import jax
import jax.numpy as jnp
from jax import lax
from jax.experimental import pallas as pl
from jax.experimental.pallas import tpu as pltpu

F32 = jnp.float32
BF16 = jnp.bfloat16

D_MODEL = 1024
N_HEADS = 8
HEAD_DIM = 64
D_ATT = N_HEADS * HEAD_DIM
D_SSM = D_MODEL - D_ATT
SSM_GROUP = 16
N_GROUPS = D_SSM // SSM_GROUP
SSM_STATE = 64
D_FF = 2816
EPS = 1e-6

LANES = 128
S5_CHUNK = 16
S5_ROWS = S5_CHUNK * SSM_GROUP
GATE_PAD = LANES
W_IN_COLS = 3 * D_ATT + GATE_PAD + D_SSM
U_COL0 = 3 * D_ATT + GATE_PAD
N_U_SLABS = D_SSM // LANES
SCAN_PAD = 128
S5_PAIRS_PER_ITER = 2
VMEM_LIMIT = 56 * 1024 * 1024
MASK_VALUE = -1e30
LOG2E = math.log2(math.e)
DEN_LANE = HEAD_DIM
SCORE_LIMIT = 80.0
NORM_SLACK = 1.02
EXP2_ZERO_BELOW = -150.0

NT_DIMS = (((1,), (1,)), ((), ()))
TN_DIMS = (((0,), (0,)), ((), ()))


def _sigmoid(x):
    return 1.0 / (1.0 + jnp.exp(-x))


def _split3(x):
    hi = x.astype(BF16)
    r = x - hi.astype(F32)
    mid = r.astype(BF16)
    lo = (r - mid.astype(F32)).astype(BF16)
    return hi, mid, lo


def _split2(x):
    hi = x.astype(BF16)
    return hi, (x - hi.astype(F32)).astype(BF16)


def _dot(a, b):
    return jnp.dot(a, b, preferred_element_type=F32)


def _dot_nt(a, b):
    return lax.dot_general(a, b, NT_DIMS, preferred_element_type=F32)


def _rms(x, g):
    ms = jnp.mean(x * x, axis=-1, keepdims=True)
    return x * lax.rsqrt(ms + EPS) * g


def _per_row(ref, nb, tb):
    m = ref[...]
    if nb == 1:
        return m.reshape(1, D_MODEL)
    return jnp.broadcast_to(m[:, None, :], (nb, tb, D_MODEL)).reshape(nb * tb, D_MODEL)


def _ada_kernel(c_ref, w_ref, b_ref, o_ref):
    c = c_ref[...]
    a_hi, a_lo = _split2(c * _sigmoid(c))
    w_hi, w_lo = _split2(w_ref[0])
    o_ref[0] = _dot(a_hi, w_hi) + _dot(a_lo, w_hi) + _dot(a_hi, w_lo) + b_ref[0]


def _ada_call(c_all, w_ada, b_ada):
    depth = w_ada.shape[0]
    nb = 1536
    n_out = w_ada.shape[2]
    return pl.pallas_call(
        _ada_kernel,
        grid=(depth, n_out // nb),
        in_specs=[
            pl.BlockSpec((16, D_MODEL), lambda l, j: (0, 0)),
            pl.BlockSpec((1, D_MODEL, nb), lambda l, j: (l, 0, j)),
            pl.BlockSpec((1, 1, nb), lambda l, j: (l, 0, j)),
        ],
        out_specs=pl.BlockSpec((1, 16, nb), lambda l, j: (l, 0, j)),
        out_shape=jax.ShapeDtypeStruct((depth, 16, n_out), F32),
        compiler_params=pltpu.CompilerParams(
            dimension_semantics=("arbitrary", "arbitrary"), vmem_limit_bytes=VMEM_LIMIT),
        name="ada",
    )(c_all, w_ada, b_ada.reshape(depth, 1, n_out))


def _inproj_kernel(x_ref, sh_ref, sc_ref, g_ref, wt_ref, bf_ref, tri_ref, sel_ref, one_ref, eye_ref,
                   go_ref, qa_ref, ka_ref, va_ref, kc_ref, vc_ref, lf_ref, u_ref, st_ref, carry_ref,
                   *, nb, tb, kv_transposed):
    rows = nb * tb

    @pl.when(pl.program_id(1) == 0)
    def _():
        carry_ref[...] = jnp.zeros_like(carry_ref)

    x = x_ref[...].reshape(rows, D_MODEL)
    hm = _rms(x, g_ref[...]) * (1.0 + _per_row(sc_ref, nb, tb)) + _per_row(sh_ref, nb, tb)
    proj = _dot_nt(hm.astype(BF16), wt_ref[...])
    q = proj[:, 0:D_ATT] * (LOG2E * HEAD_DIM ** -0.5)
    k = proj[:, D_ATT:2 * D_ATT]
    v = proj[:, 2 * D_ATT:3 * D_ATT]
    gate = proj[:, 3 * D_ATT:U_COL0] + bf_ref[...]
    logf = jnp.minimum(gate, 0.0) - jnp.log1p(jnp.exp(-jnp.abs(gate)))
    lane = lax.broadcasted_iota(jnp.int32, (rows, LANES), 1)

    def pack3(val):
        hi, mid, lo = (part.astype(F32) for part in _split3(val))
        packed = jnp.where(lane < 2 * N_HEADS, pltpu.roll(mid, N_HEADS, axis=1),
                           jnp.where(lane < 3 * N_HEADS, pltpu.roll(lo, 2 * N_HEADS, axis=1), 0.0))
        return jnp.where(lane < N_HEADS, hi, packed).astype(BF16)

    lf_pack = pack3(logf)
    cs = _dot(tri_ref[...], lf_pack)
    fcum = (cs + pltpu.roll(cs, LANES - N_HEADS, axis=1) + pltpu.roll(cs, LANES - 2 * N_HEADS, axis=1)
            + carry_ref[...])
    carry_ref[...] = fcum[rows - 1:rows, :]

    extra = _dot(pack3(fcum * LOG2E), sel_ref[...]) + one_ref[...]
    qn2 = _dot((q * q).astype(BF16), go_ref[...])
    kn2 = _dot((k * k).astype(BF16), go_ref[...])
    st_ref[0, 0] = jnp.concatenate(
        [jnp.max(qn2, axis=0, keepdims=True), jnp.max(kn2, axis=0, keepdims=True),
         fcum[0:1, :], fcum[rows - 1:rows, :], jnp.zeros((4, LANES), F32)], axis=0)
    lo_half = lane < HEAD_DIM
    den_one = jnp.where(lane == DEN_LANE, 1.0, 0.0)
    if kv_transposed:
        kt, vt = k.T, v.T
    for h in range(N_HEADS):
        pair = slice(LANES * (h // 2), LANES * (h // 2) + LANES)
        q2, k2, v2 = q[:, pair], k[:, pair], v[:, pair]
        if h % 2:
            q2 = pltpu.roll(q2, HEAD_DIM, axis=1)
            k2 = pltpu.roll(k2, HEAD_DIM, axis=1)
            v2 = pltpu.roll(v2, HEAD_DIM, axis=1)
        per_stream = lambda val: val.reshape(nb, tb, val.shape[-1])
        qa_ref[:, h] = per_stream(
            jnp.where(lo_half, q2, extra[:, LANES * h:LANES * (h + 1)]).astype(BF16))
        ka_ref[:, h] = per_stream(jnp.where(
            lo_half, k2, extra[:, D_MODEL + LANES * h:D_MODEL + LANES * (h + 1)]).astype(BF16))
        va_ref[:, h] = per_stream(jnp.where(lo_half, v2, den_one).astype(BF16))
        if kv_transposed:
            kc_ref[h] = kt[HEAD_DIM * h:HEAD_DIM * (h + 1), :]
            vc_ref[h] = vt[HEAD_DIM * h:HEAD_DIM * (h + 1), :]
        else:
            kc_ref[:, h] = per_stream(k2[:, 0:HEAD_DIM])
            vc_ref[:, h] = per_stream(v2[:, 0:HEAD_DIM])
    lf_t = _dot_nt(eye_ref[...], lf_pack)
    lf_ref[0] = lf_t[0:N_HEADS] + lf_t[N_HEADS:2 * N_HEADS] + lf_t[2 * N_HEADS:3 * N_HEADS]
    for j in range(N_U_SLABS):
        u_ref[0, j] = proj[:, U_COL0 + LANES * j:U_COL0 + LANES * (j + 1)]


def _inproj_consts(nb, tb):
    tri = np.kron(np.eye(nb, dtype=np.float32), np.tril(np.ones((tb, tb), np.float32)))
    sel = np.zeros((LANES, 2 * D_MODEL), np.float32)
    one = np.zeros((1, 2 * D_MODEL), np.float32)
    group_ones = np.zeros((D_ATT, LANES), np.float32)
    for h in range(N_HEADS):
        group_ones[HEAD_DIM * h:HEAD_DIM * (h + 1), h] = 1.0
        for p in range(3):
            sel[p * N_HEADS + h, LANES * h + HEAD_DIM + p] = 1.0
            sel[p * N_HEADS + h, D_MODEL + LANES * h + HEAD_DIM + 3 + p] = -1.0
            one[0, LANES * h + HEAD_DIM + 3 + p] = 1.0
            one[0, D_MODEL + LANES * h + HEAD_DIM + p] = 1.0
    eye = np.eye(4 * N_HEADS, LANES, dtype=np.float32)
    eye[3 * N_HEADS:] = 0.0
    return (jnp.asarray(tri, BF16), jnp.asarray(sel, BF16), jnp.asarray(one, F32), jnp.asarray(eye, BF16),
            jnp.asarray(group_ones, BF16))


def _mod_spec(layer, row0, chunk, nb=1):
    if nb == 1:
        return pl.BlockSpec((1, 1, D_MODEL), lambda b, i: ((layer * 16 + row0 + b) * 6 + chunk, 0, 0))
    return pl.BlockSpec((None, None, nb, D_MODEL), lambda b, i: (layer, chunk, 0, 0))


def _inproj_call(x, mods, layer, row0, g_pre, w_in_t, b_forget, kv_bufs, kv_depth, nb, tb):
    bsz, t, _ = x.shape
    nblk = t // tb
    rows = nb * tb
    assert nb == 1 or (nblk == 1 and kv_depth is None)
    tri, sel, one, eye, group_ones = _inproj_consts(nb, tb)
    if nb > 1:
        u_shape = (1, N_U_SLABS, bsz * t, LANES)
        u_spec = pl.BlockSpec((1, N_U_SLABS, rows, LANES), lambda b, i: (0, 0, b, 0))
        lf_shape = (1, N_HEADS, bsz * t)
        lf_spec = pl.BlockSpec((1, N_HEADS, rows), lambda b, i: (0, 0, b))
    else:
        u_shape = (bsz, N_U_SLABS, t, LANES)
        u_spec = pl.BlockSpec((1, N_U_SLABS, tb, LANES), lambda b, i: (b, 0, i, 0))
        lf_shape = (bsz, N_HEADS, t)
        lf_spec = pl.BlockSpec((1, N_HEADS, tb), lambda b, i: (b, 0, i))
    const = lambda b, i: (0, 0)
    in_specs = [
        pl.BlockSpec((nb, tb, D_MODEL), lambda b, i: (b, i, 0)),
        _mod_spec(layer, row0, 0, nb), _mod_spec(layer, row0, 1, nb),
        pl.BlockSpec((1, D_MODEL), const),
        pl.BlockSpec((None, W_IN_COLS, D_MODEL), lambda b, i: (layer, 0, 0)),
        pl.BlockSpec((1, GATE_PAD), const),
        pl.BlockSpec((rows, rows), const),
        pl.BlockSpec((LANES, 2 * D_MODEL), const),
        pl.BlockSpec((1, 2 * D_MODEL), const),
        pl.BlockSpec((4 * N_HEADS, LANES), const),
        pl.BlockSpec((D_ATT, LANES), const),
    ]
    args = [x, mods, mods, g_pre, w_in_t, b_forget, tri, sel, one, eye, group_ones]
    n_main = len(args)
    head_blk = pl.BlockSpec((nb, N_HEADS, tb, LANES), lambda b, i: (b, 0, i, 0))
    aliases = {}
    if kv_depth is None:
        kv_shape = jax.ShapeDtypeStruct((bsz, N_HEADS, t, HEAD_DIM), F32)
        kv_spec = pl.BlockSpec((nb, N_HEADS, tb, HEAD_DIM), lambda b, i: (b, 0, i, 0))
    else:
        kv_shape = jax.ShapeDtypeStruct((kv_depth, bsz, N_HEADS, HEAD_DIM, t), F32)
        kv_spec = pl.BlockSpec((None, None, N_HEADS, HEAD_DIM, tb), lambda b, i: (layer, b, 0, 0, i))
        if kv_bufs is not None:
            in_specs += [pl.BlockSpec(memory_space=pl.ANY), pl.BlockSpec(memory_space=pl.ANY)]
            args += list(kv_bufs)
            aliases = {n_main: 3, n_main + 1: 4}
    out_shape = [
        jax.ShapeDtypeStruct((bsz, N_HEADS, t, LANES), BF16),
        jax.ShapeDtypeStruct((bsz, N_HEADS, t, LANES), BF16),
        jax.ShapeDtypeStruct((bsz, N_HEADS, t, LANES), BF16),
        kv_shape, kv_shape,
        jax.ShapeDtypeStruct(lf_shape, F32),
        jax.ShapeDtypeStruct(u_shape, F32),
        jax.ShapeDtypeStruct((bsz // nb, nblk, 8, LANES), F32),
    ]
    out_specs = [
        head_blk, head_blk, head_blk,
        kv_spec, kv_spec,
        lf_spec,
        u_spec,
        pl.BlockSpec((1, 1, 8, LANES), lambda b, i: (b, i, 0, 0)),
    ]
    n_args = len(args)

    def body(*refs):
        _inproj_kernel(*refs[:n_main], *refs[n_args:], nb=nb, tb=tb, kv_transposed=kv_depth is not None)

    return pl.pallas_call(
        body,
        grid=(bsz // nb, nblk),
        in_specs=in_specs,
        out_specs=out_specs,
        out_shape=out_shape,
        scratch_shapes=[pltpu.VMEM((1, LANES), F32)],
        input_output_aliases=aliases,
        compiler_params=pltpu.CompilerParams(
            dimension_semantics=("arbitrary", "arbitrary"), vmem_limit_bytes=VMEM_LIMIT),
        name="inproj",
    )(*args)


def _finish_heads(accs, lane):
    outs = []
    for acc in accs:
        den = jnp.sum(jnp.where(lane == DEN_LANE, acc, 0.0), axis=1, keepdims=True)
        outs.append(acc * (1.0 / den))
    return jnp.where(lane < HEAD_DIM, outs[0], pltpu.roll(outs[1], HEAD_DIM, axis=1)).astype(BF16)


def _attn_kernel(small_ref, first_ref, qa_ref, ka_ref, va_ref, o_ref, *, tq, hps):
    i = pl.program_id(2)
    row = lax.broadcasted_iota(jnp.int32, (tq, tq), 0)
    col = lax.broadcasted_iota(jnp.int32, (tq, tq), 1)
    causal = col <= row
    lane = lax.broadcasted_iota(jnp.int32, (tq, LANES), 1)
    qs = [qa_ref[0, hh] for hh in range(hps)]

    def score(j, hh, masked):
        start = pl.multiple_of(j * tq, tq)
        s = _dot_nt(qs[hh], ka_ref[0, hh, pl.ds(start, tq), :])
        return jnp.where(causal, s, MASK_VALUE) if masked else s

    def values(j, hh):
        return va_ref[0, hh, pl.ds(pl.multiple_of(j * tq, tq), tq), :]

    def store(accs):
        for pr in range(hps // 2):
            o_ref[0, :, LANES * pr:LANES * (pr + 1)] = _finish_heads(accs[2 * pr:2 * pr + 2], lane)

    def tile_plain(j, accs, masked):
        return tuple(accs[hh] + _dot(jnp.exp2(score(j, hh, masked)).astype(BF16), values(j, hh))
                     for hh in range(hps))

    def tile_shifted(j, carry, masked):
        new = []
        for hh in range(hps):
            m, acc = carry[hh]
            s = score(j, hh, masked)
            m_new = jnp.maximum(m, jnp.max(s, axis=1, keepdims=True))
            p = jnp.exp2(s - m_new)
            new.append((m_new, jnp.exp2(m - m_new) * acc + _dot(p.astype(BF16), values(j, hh))))
        return tuple(new)

    small = small_ref[pl.program_id(0), pl.program_id(1)] != 0
    zero = jnp.zeros((tq, LANES), F32)

    @pl.when(small)
    def _():
        first = first_ref[pl.program_id(0), pl.program_id(1), i]
        accs = lax.fori_loop(0, i - first, lambda it, accs: tile_plain(first + it, accs, False),
                             (zero,) * hps)
        store(tile_plain(i, accs, True))

    @pl.when(jnp.logical_not(small))
    def _():
        init = ((jnp.full((tq, 1), MASK_VALUE, F32), zero),) * hps
        carry = lax.fori_loop(0, i, functools.partial(tile_shifted, masked=False), init)
        store([c[1] for c in tile_shifted(i, carry, True)])


def _attn_call(qa, ka, va, stats, tq, hps):
    bsz, _, t, _ = qa.shape
    nt, ng = t // tq, N_HEADS // hps
    assert stats.shape[1] == nt
    bound = jnp.sqrt(jnp.max(stats[:, :, 0, :N_HEADS], axis=1) *
                     jnp.max(stats[:, :, 1, :N_HEADS], axis=1)) * NORM_SLACK
    small = jnp.max(bound.reshape(bsz, ng, hps), axis=-1) < SCORE_LIMIT
    f_first, f_last = stats[:, :, 2, :N_HEADS] * LOG2E, stats[:, :, 3, :N_HEADS] * LOG2E
    top = bound[:, None, None, :] + f_first[:, :, None, :] - f_last[:, None, :, :]
    needed = jnp.max((top >= EXP2_ZERO_BELOW).astype(jnp.int32).reshape(bsz, nt, nt, ng, hps), axis=-1)
    first = jnp.where(small[:, :, None], jnp.sum(1 - needed, axis=2).transpose(0, 2, 1), 0)
    whole = pl.BlockSpec((1, hps, t, LANES), lambda b, p, i, *_: (b, p, 0, 0))
    return pl.pallas_call(
        functools.partial(_attn_kernel, tq=tq, hps=hps),
        grid_spec=pltpu.PrefetchScalarGridSpec(
            num_scalar_prefetch=2,
            grid=(bsz, ng, nt),
            in_specs=[pl.BlockSpec((1, hps, tq, LANES), lambda b, p, i, *_: (b, p, i, 0)), whole, whole],
            out_specs=pl.BlockSpec((1, tq, HEAD_DIM * hps), lambda b, p, i, *_: (b, i, p)),
        ),
        out_shape=jax.ShapeDtypeStruct((bsz, t, D_ATT), BF16),
        compiler_params=pltpu.CompilerParams(
            dimension_semantics=("arbitrary", "arbitrary", "arbitrary"),
            vmem_limit_bytes=VMEM_LIMIT),
        name="fox_attn",
    )(small.astype(jnp.int32), first.astype(jnp.int32), qa, ka, va)


def _attn_sample_kernel(qa_ref, ka_ref, va_ref, ckt_ref, cvt_ref, clf_ref, o_ref, *, t, n_past):
    lane_p = lax.broadcasted_iota(jnp.int32, (N_HEADS, n_past), 1)
    c = clf_ref[...]
    total = jnp.sum(c, axis=1, keepdims=True)
    d = 1
    while d < n_past:
        c = c + jnp.where(lane_p >= d, pltpu.roll(c, d, axis=1), 0.0)
        d *= 2
    f_past = (c - total) * LOG2E
    lane = lax.broadcasted_iota(jnp.int32, (t, LANES), 1)
    causal = (lax.broadcasted_iota(jnp.int32, (t, t), 1) <= lax.broadcasted_iota(jnp.int32, (t, t), 0))
    den_rows = jnp.where(lax.broadcasted_iota(jnp.int32, (LANES - HEAD_DIM, n_past), 0) == 0, 1.0, 0.0)
    accs = []
    for h in range(N_HEADS):
        q = qa_ref[0, h]
        fq = jnp.sum(jnp.where((lane >= HEAD_DIM) & (lane < HEAD_DIM + 3), q.astype(F32), 0.0),
                     axis=1, keepdims=True)
        s_past = _dot(q[:, 0:HEAD_DIM], ckt_ref[h].astype(BF16)) + fq - f_past[h:h + 1, :]
        s_new = jnp.where(causal, _dot_nt(q, ka_ref[0, h]), MASK_VALUE)
        m = jnp.maximum(jnp.max(s_past, axis=1, keepdims=True), jnp.max(s_new, axis=1, keepdims=True))
        p_past = jnp.exp2(s_past - m).astype(BF16)
        p_new = jnp.exp2(s_new - m).astype(BF16)
        cvt_aug = jnp.concatenate([cvt_ref[h], den_rows], axis=0).astype(BF16)
        accs.append(_dot_nt(p_past, cvt_aug) + _dot(p_new, va_ref[0, h]))
    for pr in range(N_HEADS // 2):
        o_ref[0, :, LANES * pr:LANES * (pr + 1)] = _finish_heads(accs[2 * pr:2 * pr + 2], lane)


def _attn_sample_call(qa, ka, va, cache_kt, cache_vt, cache_logf, layer):
    bsz, _, t, _ = qa.shape
    n_past = cache_kt.shape[4]
    cache_spec = pl.BlockSpec((None, None, N_HEADS, HEAD_DIM, n_past), lambda b: (layer, b, 0, 0, 0))
    new_spec = pl.BlockSpec((1, N_HEADS, t, LANES), lambda b: (b, 0, 0, 0))
    return pl.pallas_call(
        functools.partial(_attn_sample_kernel, t=t, n_past=n_past),
        grid=(bsz,),
        in_specs=[
            new_spec, new_spec, new_spec, cache_spec, cache_spec,
            pl.BlockSpec((None, None, N_HEADS, n_past), lambda b: (layer, b, 0, 0)),
        ],
        out_specs=pl.BlockSpec((1, t, D_ATT), lambda b: (b, 0, 0)),
        out_shape=jax.ShapeDtypeStruct((bsz, t, D_ATT), BF16),
        compiler_params=pltpu.CompilerParams(
            dimension_semantics=("arbitrary",), vmem_limit_bytes=VMEM_LIMIT),
        name="fox_attn_sample",
    )(qa, ka, va, cache_kt, cache_vt, cache_logf)


def _s5_table_kernel(ca_ref, cb_ref, ba_ref, bb_ref, prr_ref, pii_ref, dv_ref,
                     toep_ref, bpow_ref, cpow_ref, tt_scr, *, gb):
    row = lax.broadcasted_iota(jnp.int32, (SSM_GROUP, S5_ROWS), 0)
    lane = lax.broadcasted_iota(jnp.int32, (SSM_GROUP, S5_ROWS), 1)
    for gi in range(gb):
        ca, cb, ba, bb = ca_ref[gi], cb_ref[gi], ba_ref[gi], bb_ref[gi]
        power = lambda a, b, n: a * prr_ref[gi, n:n + 1, :] + b * pii_ref[gi, n:n + 1, :]
        xs = [power(ca, cb, n) for n in range(S5_CHUNK + 1)]
        cpow_ref[gi] = jnp.concatenate(xs[1:], axis=0).astype(BF16)
        bpow_ref[gi] = jnp.concatenate(
            [power(ba, bb, S5_CHUNK - 1 - s) for s in range(S5_CHUNK)], axis=0).T.astype(BF16)
        x_hi, x_lo = _split2(jnp.concatenate(xs[:S5_CHUNK], axis=0))
        b_hi, b_lo = _split2(ba)
        r0 = _dot_nt(b_hi, x_hi) + _dot_nt(b_lo, x_hi) + _dot_nt(b_hi, x_lo)
        r0 = r0 + jnp.where(lane == row, dv_ref[gi], 0.0)
        for s in range(S5_CHUNK):
            blk = r0
            if s:
                blk = jnp.where(lane >= SSM_GROUP * s, pltpu.roll(r0, SSM_GROUP * s, axis=1), 0.0)
            tt_scr[SSM_GROUP * s:SSM_GROUP * (s + 1), :] = blk
        toep_ref[gi] = tt_scr[...].T.astype(BF16)


def _s5_tables(a_re, a_im, log_dt, b_re_t, b_im_t, c_re, c_im, d_skip):
    depth = a_re.shape[0]
    dt = jnp.exp(log_dt)[..., None]
    x, y = a_re * dt, a_im * dt
    ex, cy, sy = jnp.exp(x), jnp.cos(y), jnp.sin(y)
    ar, ai = ex * cy, ex * sy
    sh = jnp.sin(0.5 * y)
    nr, ni = jnp.expm1(x) * cy - 2.0 * sh * sh, ai
    den = a_re * a_re + a_im * a_im
    fr = ((nr * a_re + ni * a_im) / den)[:, :, None, :]
    fi = ((ni * a_re - nr * a_im) / den)[:, :, None, :]
    bbr, bbi = fr * b_re_t - fi * b_im_t, fr * b_im_t + fi * b_re_t
    pr, pi = [jnp.ones_like(ar)], [jnp.zeros_like(ar)]
    for _ in range(S5_CHUNK):
        pr, pi = pr + [pr[-1] * ar - pi[-1] * ai], pi + [pr[-1] * ai + pi[-1] * ar]
    mr, mi = pr[S5_CHUNK], pi[S5_CHUNK]
    pr, pi = jnp.stack(pr, axis=2), jnp.stack(pi, axis=2)
    cat = lambda a, b: jnp.concatenate([a, b], axis=-1)
    ins = [cat(c_re, -c_im), cat(-c_im, -c_re), cat(bbr, bbi), cat(-bbi, bbr), cat(pr, pr), cat(pi, pi),
           jnp.pad(d_skip.reshape(depth, N_GROUPS, 1, SSM_GROUP), ((0, 0), (0, 0), (0, 0), (0, S5_ROWS - SSM_GROUP)))]
    gb = 8
    spec = lambda rows, cols: pl.BlockSpec((None, gb, rows, cols), lambda l, j: (l, j, 0, 0))
    tab = lambda rows, cols: jax.ShapeDtypeStruct((depth, N_GROUPS, rows, cols), BF16)
    toep, bpow, cpow = pl.pallas_call(
        functools.partial(_s5_table_kernel, gb=gb),
        grid=(depth, N_GROUPS // gb),
        in_specs=[spec(SSM_GROUP, LANES)] * 4 + [spec(S5_CHUNK + 1, LANES)] * 2 + [spec(1, S5_ROWS)],
        out_specs=[spec(S5_ROWS, S5_ROWS), spec(LANES, S5_ROWS), spec(S5_ROWS, LANES)],
        out_shape=[tab(S5_ROWS, S5_ROWS), tab(LANES, S5_ROWS), tab(S5_ROWS, LANES)],
        scratch_shapes=[pltpu.VMEM((S5_ROWS, S5_ROWS), F32)],
        compiler_params=pltpu.CompilerParams(
            dimension_semantics=("arbitrary", "arbitrary"), vmem_limit_bytes=VMEM_LIMIT),
        name="s5_tables",
    )(*ins)
    rows_r, rows_i = [], []
    for _ in range(8):
        rows_r.append(cat(mr, mr))
        rows_i.append(cat(-mi, mi))
        mr, mi = mr * mr - mi * mi, 2.0 * mr * mi
    scm = jnp.stack(rows_r + rows_i, axis=2)
    return toep, bpow, cpow, scm


def _gelu_tanh(y):
    return 0.5 * y * (1.0 + jnp.tanh(math.sqrt(2.0 / math.pi) * (y + 0.044715 * (y * y * y))))


def _s5_kernel(*refs, n, n_real, seglen, fin_row0, use_h0):
    if use_h0:
        u_ref, toep_ref, bpow_ref, cpow_ref, scm_ref, wglu_ref, h0_ref = refs[:7]
        o_ref, hfin_ref, z_scr, zt_scr, h_scr, h0_scr = refs[7:]
    else:
        u_ref, toep_ref, bpow_ref, cpow_ref, scm_ref, wglu_ref = refs[:6]
        o_ref, hfin_ref, z_scr, zt_scr, h_scr = refs[6:]
    gps = LANES // SSM_GROUP
    for pp in range(S5_PAIRS_PER_ITER):
        for part in range(2):
            h_scr[pp, part, 0:SCAN_PAD, :] = jnp.zeros((SCAN_PAD, LANES), F32)
    if use_h0:
        h0_scr[...] = jnp.zeros(h0_scr.shape, F32)

    for k in range(S5_CHUNK):
        for j in range(N_U_SLABS):
            vv = u_ref[0, j, pl.ds(k, n_real, stride=S5_CHUNK), :]
            if n_real < n:
                vv = jnp.concatenate([vv, jnp.zeros((n - n_real, LANES), F32)], axis=0)
            z_scr[gps * j:gps * (j + 1), SSM_GROUP * k:SSM_GROUP * (k + 1), :] = (
                vv.T.reshape(gps, SSM_GROUP, n).astype(BF16))

    chunk_in_seg = lax.broadcasted_iota(jnp.int32, (n, LANES), 0) % seglen
    lo = lax.broadcasted_iota(jnp.int32, (n, LANES), 1) < SSM_STATE
    lo_row = lax.broadcasted_iota(jnp.int32, (1, LANES), 1) < SSM_STATE
    lo_fin = lax.broadcasted_iota(jnp.int32, (32, LANES), 1) < SSM_STATE
    swap = lambda a: pltpu.roll(a, SSM_STATE, axis=1)

    def pair_step(it, _):
        for pp in range(S5_PAIRS_PER_ITER):
            gs = [2 * (it * S5_PAIRS_PER_ITER + pp), 2 * (it * S5_PAIRS_PER_ITER + pp) + 1]
            ucts = [z_scr[g] for g in gs]
            yts = [_dot(toep_ref[g], u) for g, u in zip(gs, ucts)]
            hs = [_dot(bpow_ref[g], u).T for g, u in zip(gs, ucts)]
            h0s = [0.0, 0.0]
            if use_h0:
                for idx, g in enumerate(gs):
                    h0_scr[idx, pl.ds(0, h0_ref.shape[1], stride=seglen), :] = h0_ref[g]
                    h0s[idx] = h0_scr[idx]
                    hs[idx] = hs[idx] + scm_ref[g, 0:1, :] * h0s[idx] + scm_ref[g, 8:9, :] * swap(h0s[idx])
            hre = jnp.where(lo, hs[0], swap(hs[1]))
            him = jnp.where(lo, swap(hs[0]), hs[1])

            def shifted(by):
                h_scr[pp, 0, SCAN_PAD:SCAN_PAD + n, :] = hre
                h_scr[pp, 1, SCAN_PAD:SCAN_PAD + n, :] = him
                return (h_scr[pp, 0, pl.ds(SCAN_PAD - by, n), :], h_scr[pp, 1, pl.ds(SCAN_PAD - by, n), :])

            d, j = 1, 0
            while d < seglen:
                sre, sim = shifted(d)
                if seglen < n:
                    sre = jnp.where(chunk_in_seg >= d, sre, 0.0)
                    sim = jnp.where(chunk_in_seg >= d, sim, 0.0)
                ar = jnp.where(lo_row, scm_ref[gs[0], j:j + 1, :], scm_ref[gs[1], j:j + 1, :])
                ai = jnp.where(lo_row, -scm_ref[gs[0], 8 + j:9 + j, :], scm_ref[gs[1], 8 + j:9 + j, :])
                hre, him = hre + ar * sre - ai * sim, him + ar * sim + ai * sre
                d, j = 2 * d, j + 1
            pre, pim = shifted(1)
            hps = [jnp.where(lo, pre, swap(pim)), jnp.where(lo, swap(pre), pim)]
            fre, fim = hre[fin_row0:fin_row0 + 32, :], him[fin_row0:fin_row0 + 32, :]
            fins = [jnp.where(lo_fin, fre, swap(fim)), jnp.where(lo_fin, swap(fre), fim)]
            for idx, g in enumerate(gs):
                hp = hps[idx]
                if seglen < n:
                    hp = jnp.where(chunk_in_seg >= 1, hp, h0s[idx])
                zt_scr[g] = _gelu_tanh(yts[idx] + _dot_nt(cpow_ref[g], hp.astype(BF16)))
                hfin_ref[0, g] = fins[idx]
        return 0

    lax.fori_loop(0, N_GROUPS // (2 * S5_PAIRS_PER_ITER), pair_step, 0)

    for k in range(S5_CHUNK):
        slabs = []
        for j in range(N_U_SLABS):
            w = zt_scr[gps * j:gps * (j + 1), SSM_GROUP * k:SSM_GROUP * (k + 1), :]
            slabs.append(w.reshape(LANES, n).T)
        zk = jnp.concatenate(slabs, axis=1)
        out = zk * _sigmoid(_dot(zk.astype(BF16), wglu_ref[...]))
        for j in range(N_U_SLABS):
            o_ref[0, j, pl.ds(k, n_real, stride=S5_CHUNK), :] = out[0:n_real, LANES * j:LANES * (j + 1)]


def _s5_call(u, toep, bpow, cpow, scm, wglu, h0, layer, n, n_real, seglen, fin_row0):
    nb, _, tt, _ = u.shape
    use_h0 = h0 is not None
    one = pl.Buffered(1)
    tab = lambda rows, cols: pl.BlockSpec((None, N_GROUPS, rows, cols), lambda b: (layer, 0, 0, 0),
                                          pipeline_mode=one)
    in_specs = [
        pl.BlockSpec((1, N_U_SLABS, tt, LANES), lambda b: (b, 0, 0, 0), pipeline_mode=one),
        tab(S5_ROWS, S5_ROWS), tab(LANES, S5_ROWS), tab(S5_ROWS, LANES), tab(16, LANES),
        pl.BlockSpec((None, D_SSM, D_SSM), lambda b: (layer, 0, 0), pipeline_mode=one),
    ]
    args = [u, toep, bpow, cpow, scm, wglu]
    scratch = [
        pltpu.VMEM((N_GROUPS, S5_ROWS, n), BF16),
        pltpu.VMEM((N_GROUPS, S5_ROWS, n), F32),
        pltpu.VMEM((S5_PAIRS_PER_ITER, 2, SCAN_PAD + n, LANES), F32),
    ]
    if use_h0:
        in_specs.append(tab(h0.shape[2], LANES))
        args.append(h0)
        scratch.append(pltpu.VMEM((2, n, LANES), F32))
    return pl.pallas_call(
        functools.partial(_s5_kernel, n=n, n_real=n_real, seglen=seglen, fin_row0=fin_row0,
                          use_h0=use_h0),
        grid=(nb,),
        in_specs=in_specs,
        out_specs=[
            pl.BlockSpec((1, N_U_SLABS, tt, LANES), lambda b: (b, 0, 0, 0), pipeline_mode=one),
            pl.BlockSpec((1, N_GROUPS, 32, LANES), lambda b: (b, 0, 0, 0)),
        ],
        out_shape=[
            jax.ShapeDtypeStruct((nb, N_U_SLABS, tt, LANES), F32),
            jax.ShapeDtypeStruct((nb, N_GROUPS, 32, LANES), F32),
        ],
        scratch_shapes=scratch,
        compiler_params=pltpu.CompilerParams(
            dimension_semantics=("arbitrary",), vmem_limit_bytes=VMEM_LIMIT),
        name="s5_mix",
    )(*args)


def _out_ffn_kernel(x_ref, att_ref, ssm_ref, g1_ref, sh2_ref, sc2_ref, g2_ref,
                    gpm_ref, gpf_ref, gpo_ref, wo_ref, wg_ref, wu_ref, wd_ref, o_ref, *, nb, tb):
    rows = nb * tb
    mix = jnp.concatenate(
        [att_ref[...].reshape(rows, D_ATT)] + [ssm_ref[0, j].astype(BF16) for j in range(N_U_SLABS)],
        axis=1)
    o = _dot(mix, wo_ref[...])
    x1 = x_ref[...].reshape(rows, D_MODEL) + _per_row(g1_ref, nb, tb) * _rms(o, gpm_ref[...])
    hf = (_rms(x1, gpf_ref[...]) * (1.0 + _per_row(sc2_ref, nb, tb)) + _per_row(sh2_ref, nb, tb)).astype(BF16)
    gt = _dot(hf, wg_ref[...])
    up = _dot(hf, wu_ref[...])
    hid = (gt * _sigmoid(gt) * up).astype(BF16)
    f = _dot(hid, wd_ref[...])
    o_ref[...] = (x1 + _per_row(g2_ref, nb, tb) * _rms(f, gpo_ref[...])).reshape(nb, tb, D_MODEL)


def _out_ffn_call(x, att, ssm, mods, layer, row0, g_post_mix, g_pre_ffn, g_post_ffn,
                  w_out, w_gate, w_up, w_down, nb, tb):
    bsz, t, _ = x.shape
    nblk = t // tb
    one = pl.Buffered(1)
    const = lambda b, i: (0, 0)
    lay = lambda b, i: (layer, 0, 0)
    if nb > 1:
        ssm_spec = pl.BlockSpec((1, N_U_SLABS, nb * tb, LANES), lambda b, i: (0, 0, b, 0))
    else:
        ssm_spec = pl.BlockSpec((1, N_U_SLABS, tb, LANES), lambda b, i: (b, 0, i, 0))
    row = pl.BlockSpec((1, D_MODEL), const)
    return pl.pallas_call(
        functools.partial(_out_ffn_kernel, nb=nb, tb=tb),
        grid=(bsz // nb, nblk),
        in_specs=[
            pl.BlockSpec((nb, tb, D_MODEL), lambda b, i: (b, i, 0)),
            pl.BlockSpec((nb, tb, D_ATT), lambda b, i: (b, i, 0)),
            ssm_spec,
            _mod_spec(layer, row0, 2, nb), _mod_spec(layer, row0, 3, nb), _mod_spec(layer, row0, 4, nb),
            _mod_spec(layer, row0, 5, nb),
            row, row, row,
            pl.BlockSpec((None, D_MODEL, D_MODEL), lay, pipeline_mode=one),
            pl.BlockSpec((None, D_MODEL, D_FF), lay, pipeline_mode=one),
            pl.BlockSpec((None, D_MODEL, D_FF), lay, pipeline_mode=one),
            pl.BlockSpec((None, D_FF, D_MODEL), lay, pipeline_mode=one),
        ],
        out_specs=pl.BlockSpec((nb, tb, D_MODEL), lambda b, i: (b, i, 0)),
        out_shape=jax.ShapeDtypeStruct((bsz, t, D_MODEL), F32),
        compiler_params=pltpu.CompilerParams(
            dimension_semantics=("arbitrary", "arbitrary"), vmem_limit_bytes=VMEM_LIMIT),
        name="out_ffn",
    )(x, att, ssm, mods, mods, mods, mods, g_post_mix, g_pre_ffn, g_post_ffn,
      w_out, w_gate, w_up, w_down)


def kernel(x_prompt, x_sample, c_prompt, c_sample, cache_k, cache_v, cache_logf, state_ssm_re,
           state_ssm_im, w_ada, b_ada, g_pre_mix, g_post_mix, g_pre_ffn, g_post_ffn, w_in, b_forget,
           ssm_a_re, ssm_a_im, ssm_log_dt, ssm_b_re, ssm_b_im, ssm_c_re, ssm_c_im, ssm_d, w_glu,
           w_out, w_gate, w_up, w_down):
    depth = w_in.shape[0]
    bp, tp, _ = x_prompt.shape
    bs, ts, _ = x_sample.shape
    n_att = 3 * D_ATT + N_HEADS

    c_all = jnp.concatenate([c_prompt, c_sample, jnp.zeros((16 - bp - bs, D_MODEL), F32)], axis=0)
    mod = _ada_call(c_all, w_ada, b_ada)
    modr = mod.reshape(depth * 16 * 6, 1, D_MODEL)
    mod_s = mod[:, bp:bp + bs].reshape(depth, bs, 6, D_MODEL).transpose(0, 2, 1, 3)

    w_in_t = jnp.swapaxes(w_in, 1, 2)
    w_in_t = jnp.concatenate(
        [w_in_t[:, :n_att], jnp.zeros((depth, GATE_PAD - N_HEADS, D_MODEL), F32), w_in_t[:, n_att:]],
        axis=1).astype(BF16)
    bf_p = jnp.pad(b_forget, ((0, 0), (0, GATE_PAD - N_HEADS)))
    w_glu_b, w_out_b = w_glu.astype(BF16), w_out.astype(BF16)
    w_gate_b, w_up_b, w_down_b = w_gate.astype(BF16), w_up.astype(BF16), w_down.astype(BF16)
    toep, bpow, cpow, scm = _s5_tables(ssm_a_re, ssm_a_im, ssm_log_dt, jnp.swapaxes(ssm_b_re, 2, 3),
                                       jnp.swapaxes(ssm_b_im, 2, 3), ssm_c_re, ssm_c_im, ssm_d)
    cache_kt, cache_vt = jnp.swapaxes(cache_k, 3, 4), jnp.swapaxes(cache_v, 3, 4)
    h0 = jnp.concatenate([state_ssm_re, state_ssm_im], axis=-1).transpose(0, 2, 1, 3)

    n_chunk_p = tp // S5_CHUNK
    seg_s = ts // S5_CHUNK
    yp, ys = x_prompt, x_sample
    kvp = None
    lfp, ssp, kss, vss, lfs, sss = [], [], [], [], [], []
    for l in range(depth):
        qa, ka, va, kbuf, vbuf, lf, u, stats = _inproj_call(
            yp, modr, l, 0, g_pre_mix[l:l + 1], w_in_t, bf_p[l:l + 1], kvp, depth, 1, 512)
        kvp = (kbuf, vbuf)
        lfp.append(lf)
        att = _attn_call(qa, ka, va, stats, 512, 4)
        ssm, hfin = _s5_call(u, toep, bpow, cpow, scm, w_glu_b, None, l, n_chunk_p, n_chunk_p,
                             n_chunk_p, n_chunk_p - 32)
        ssp.append(hfin[:, :, 31, :])
        yp = _out_ffn_call(yp, att, ssm, modr, l, 0, g_post_mix[l:l + 1], g_pre_ffn[l:l + 1],
                           g_post_ffn[l:l + 1], w_out_b, w_gate_b, w_up_b, w_down_b, 1, 512)

        qa, ka, va, k_s, v_s, lf, u, _ = _inproj_call(
            ys, mod_s, l, bp, g_pre_mix[l:l + 1], w_in_t, bf_p[l:l + 1], None, None, bs, ts)
        kss.append(k_s)
        vss.append(v_s)
        lfs.append(lf.reshape(N_HEADS, bs, ts).transpose(1, 0, 2))
        att = _attn_sample_call(qa, ka, va, cache_kt, cache_vt, cache_logf, l)
        ssm, hfin = _s5_call(u, toep, bpow, cpow, scm, w_glu_b, h0, l, LANES, bs * seg_s, seg_s, 0)
        sss.append(hfin[0, :, seg_s - 1:bs * seg_s:seg_s, :].transpose(1, 0, 2))
        ys = _out_ffn_call(ys, att, ssm, mod_s, l, bp, g_post_mix[l:l + 1], g_pre_ffn[l:l + 1],
                           g_post_ffn[l:l + 1], w_out_b, w_gate_b, w_up_b, w_down_b, bs, ts)

    ssp, sss = jnp.stack(ssp), jnp.stack(sss)
    return (yp, ys, jnp.swapaxes(kvp[0], 3, 4), jnp.swapaxes(kvp[1], 3, 4), jnp.stack(lfp),
            ssp[..., :SSM_STATE], ssp[..., SSM_STATE:],
            jnp.stack(kss), jnp.stack(vss), jnp.stack(lfs),
            sss[..., :SSM_STATE], sss[..., SSM_STATE:])
```

```python
import functools
import math

import numpy as np
import jax
import jax.numpy as jnp
from jax import lax
from jax.experimental import pallas as pl
from jax.experimental.pallas import tpu as pltpu

F32 = jnp.float32
BF16 = jnp.bfloat16

D_MODEL = 1024
N_HEADS = 8
HEAD_DIM = 64
D_ATT = N_HEADS * HEAD_DIM
D_SSM = D_MODEL - D_ATT
SSM_GROUP = 16
N_GROUPS = D_SSM // SSM_GROUP
SSM_STATE = 64
D_FF = 2816
EPS = 1e-6

LANES = 128
S5_CHUNK = 16
S5_ROWS = S5_CHUNK * SSM_GROUP
GATE_PAD = LANES
W_IN_COLS = 3 * D_ATT + GATE_PAD + D_SSM
U_COL0 = 3 * D_ATT + GATE_PAD
N_U_SLABS = D_SSM // LANES
SCAN_PAD = 128
S5_PAIRS_PER_ITER = 2
VMEM_LIMIT = 56 * 1024 * 1024
MASK_VALUE = -1e30
LOG2E = math.log2(math.e)
DEN_LANE = HEAD_DIM
N_EXTRA = 6
SCORE_LIMIT = 80.0
NORM_SLACK = 1.02
EXP2_ZERO_BELOW = -150.0

NT_DIMS = (((1,), (1,)), ((), ()))
TN_DIMS = (((0,), (0,)), ((), ()))


def _sigmoid(x):
    return 1.0 / (1.0 + jnp.exp(-x))


def _split3(x):
    hi = x.astype(BF16)
    r = x - hi.astype(F32)
    mid = r.astype(BF16)
    lo = (r - mid.astype(F32)).astype(BF16)
    return hi, mid, lo


def _split2(x):
    hi = x.astype(BF16)
    return hi, (x - hi.astype(F32)).astype(BF16)


def _dot(a, b):
    return jnp.dot(a, b, preferred_element_type=F32)


def _dot_nt(a, b):
    return lax.dot_general(a, b, NT_DIMS, preferred_element_type=F32)


def _rms(x, g):
    ms = jnp.mean(x * x, axis=-1, keepdims=True)
    return x * lax.rsqrt(ms + EPS) * g


def _per_row(ref, nb, tb):
    m = ref[...]
    if nb == 1:
        return m.reshape(1, D_MODEL)
    return jnp.broadcast_to(m[:, None, :], (nb, tb, D_MODEL)).reshape(nb * tb, D_MODEL)


def _ada_kernel(c_ref, w_ref, b_ref, o_ref):
    c = c_ref[...]
    a_hi, a_lo = _split2(c * _sigmoid(c))
    w_hi, w_lo = _split2(w_ref[0])
    o_ref[0] = _dot(a_hi, w_hi) + _dot(a_lo, w_hi) + _dot(a_hi, w_lo) + b_ref[0]


def _ada_call(c_all, w_ada, b_ada):
    depth = w_ada.shape[0]
    nb = 1536
    n_out = w_ada.shape[2]
    return pl.pallas_call(
        _ada_kernel,
        grid=(depth, n_out // nb),
        in_specs=[
            pl.BlockSpec((16, D_MODEL), lambda l, j: (0, 0)),
            pl.BlockSpec((1, D_MODEL, nb), lambda l, j: (l, 0, j)),
            pl.BlockSpec((1, 1, nb), lambda l, j: (l, 0, j)),
        ],
        out_specs=pl.BlockSpec((1, 16, nb), lambda l, j: (l, 0, j)),
        out_shape=jax.ShapeDtypeStruct((depth, 16, n_out), F32),
        compiler_params=pltpu.CompilerParams(
            dimension_semantics=("arbitrary", "arbitrary"), vmem_limit_bytes=VMEM_LIMIT),
        name="ada",
    )(c_all, w_ada, b_ada.reshape(depth, 1, n_out))


def _inproj_kernel(x_ref, sh_ref, sc_ref, g_ref, wt_ref, bf_ref, tri_ref, sel_ref, one_ref, eye_ref,
                   go_ref, qa_ref, ka_ref, va_ref, kc_ref, vc_ref, lf_ref, u_ref, st_ref, carry_ref,
                   *, nb, tb, kv_transposed):
    rows = nb * tb

    @pl.when(pl.program_id(1) == 0)
    def _():
        carry_ref[...] = jnp.zeros_like(carry_ref)

    x = x_ref[...].reshape(rows, D_MODEL)
    hm = _rms(x, g_ref[...]) * (1.0 + _per_row(sc_ref, nb, tb)) + _per_row(sh_ref, nb, tb)
    proj = _dot_nt(hm.astype(BF16), wt_ref[...])
    q = proj[:, 0:D_ATT] * (LOG2E * HEAD_DIM ** -0.5)
    k = proj[:, D_ATT:2 * D_ATT]
    v = proj[:, 2 * D_ATT:3 * D_ATT]
    gate = proj[:, 3 * D_ATT:U_COL0] + bf_ref[...]
    logf = jnp.minimum(gate, 0.0) - jnp.log1p(jnp.exp(-jnp.abs(gate)))
    lane = lax.broadcasted_iota(jnp.int32, (rows, LANES), 1)

    def pack3(val):
        hi, mid, lo = (part.astype(F32) for part in _split3(val))
        packed = jnp.where(lane < 2 * N_HEADS, pltpu.roll(mid, N_HEADS, axis=1),
                           jnp.where(lane < 3 * N_HEADS, pltpu.roll(lo, 2 * N_HEADS, axis=1), 0.0))
        return jnp.where(lane < N_HEADS, hi, packed).astype(BF16)

    lf_pack = pack3(logf)
    cs = _dot(tri_ref[...], lf_pack)
    fcum = (cs + pltpu.roll(cs, LANES - N_HEADS, axis=1) + pltpu.roll(cs, LANES - 2 * N_HEADS, axis=1)
            + carry_ref[...])
    carry_ref[...] = fcum[rows - 1:rows, :]

    extra = _dot(pack3(fcum * LOG2E), sel_ref[...]) + one_ref[...]
    in_extra = (lane >= HEAD_DIM) & (lane < HEAD_DIM + N_EXTRA)
    qn2 = _dot((q * q).astype(BF16), go_ref[...])
    kn2 = _dot((k * k).astype(BF16), go_ref[...])
    st_ref[0, 0] = jnp.concatenate(
        [jnp.max(qn2, axis=0, keepdims=True), jnp.max(kn2, axis=0, keepdims=True),
         fcum[0:1, :], fcum[rows - 1:rows, :], jnp.zeros((4, LANES), F32)], axis=0)
    lo_half = lane < HEAD_DIM
    den_one = jnp.where(lane == DEN_LANE, 1.0, 0.0)
    if kv_transposed:
        kt, vt = k.T, v.T
    for h in range(N_HEADS):
        pair = slice(LANES * (h // 2), LANES * (h // 2) + LANES)
        q2, k2, v2 = q[:, pair], k[:, pair], v[:, pair]
        if h % 2:
            q2 = pltpu.roll(q2, HEAD_DIM, axis=1)
            k2 = pltpu.roll(k2, HEAD_DIM, axis=1)
            v2 = pltpu.roll(v2, HEAD_DIM, axis=1)
        per_stream = lambda val: val.reshape(nb, tb, val.shape[-1])
        to_extra = lambda side: jnp.where(in_extra, pltpu.roll(
            extra[:, LANES * side:LANES * (side + 1)], (HEAD_DIM - 8 * h) % LANES, axis=1), 0.0)
        qa_ref[:, h] = per_stream(jnp.where(lo_half, q2, to_extra(0)).astype(BF16))
        ka_ref[:, h] = per_stream(jnp.where(lo_half, k2, to_extra(1)).astype(BF16))
        va_ref[:, h] = per_stream(jnp.where(lo_half, v2, den_one).astype(BF16))
        if kv_transposed:
            kc_ref[h] = kt[HEAD_DIM * h:HEAD_DIM * (h + 1), :]
            vc_ref[h] = vt[HEAD_DIM * h:HEAD_DIM * (h + 1), :]
        else:
            kc_ref[:, h] = per_stream(k2[:, 0:HEAD_DIM])
            vc_ref[:, h] = per_stream(v2[:, 0:HEAD_DIM])
    lf_t = _dot_nt(eye_ref[...], lf_pack)
    lf_ref[0] = lf_t[0:N_HEADS] + lf_t[N_HEADS:2 * N_HEADS] + lf_t[2 * N_HEADS:3 * N_HEADS]
    for j in range(N_U_SLABS):
        u_ref[0, j] = proj[:, U_COL0 + LANES * j:U_COL0 + LANES * (j + 1)]


def _inproj_consts(nb, tb):
    tri = np.kron(np.eye(nb, dtype=np.float32), np.tril(np.ones((tb, tb), np.float32)))
    sel = np.zeros((LANES, 2 * LANES), np.float32)
    one = np.zeros((1, 2 * LANES), np.float32)
    group_ones = np.zeros((D_ATT, LANES), np.float32)
    for h in range(N_HEADS):
        group_ones[HEAD_DIM * h:HEAD_DIM * (h + 1), h] = 1.0
        for p in range(3):
            sel[p * N_HEADS + h, 8 * h + p] = 1.0
            one[0, 8 * h + 3 + p] = 1.0
            one[0, LANES + 8 * h + p] = 1.0
            sel[p * N_HEADS + h, LANES + 8 * h + 3 + p] = -1.0
    eye = np.eye(4 * N_HEADS, LANES, dtype=np.float32)
    eye[3 * N_HEADS:] = 0.0
    return (jnp.asarray(tri, BF16), jnp.asarray(sel, BF16), jnp.asarray(one, F32), jnp.asarray(eye, BF16),
            jnp.asarray(group_ones, BF16))


def _mod_spec(layer, row0, chunk, nb=1):
    if nb == 1:
        return pl.BlockSpec((1, 1, D_MODEL), lambda b, i: ((layer * 16 + row0 + b) * 6 + chunk, 0, 0))
    return pl.BlockSpec((None, None, nb, D_MODEL), lambda b, i: (layer, chunk, 0, 0))


def _inproj_call(x, mods, layer, row0, g_pre, w_in_t, b_forget, kv_bufs, kv_depth, nb, tb):
    bsz, t, _ = x.shape
    nblk = t // tb
    rows = nb * tb
    assert nb == 1 or (nblk == 1 and kv_depth is None)
    tri, sel, one, eye, group_ones = _inproj_consts(nb, tb)
    if nb > 1:
        u_shape = (1, N_U_SLABS, bsz * t, LANES)
        u_spec = pl.BlockSpec((1, N_U_SLABS, rows, LANES), lambda b, i: (0, 0, b, 0))
        lf_shape = (1, N_HEADS, bsz * t)
        lf_spec = pl.BlockSpec((1, N_HEADS, rows), lambda b, i: (0, 0, b))
    else:
        u_shape = (bsz, N_U_SLABS, t, LANES)
        u_spec = pl.BlockSpec((1, N_U_SLABS, tb, LANES), lambda b, i: (b, 0, i, 0))
        lf_shape = (bsz, N_HEADS, t)
        lf_spec = pl.BlockSpec((1, N_HEADS, tb), lambda b, i: (b, 0, i))
    const = lambda b, i: (0, 0)
    in_specs = [
        pl.BlockSpec((nb, tb, D_MODEL), lambda b, i: (b, i, 0)),
        _mod_spec(layer, row0, 0, nb), _mod_spec(layer, row0, 1, nb),
        pl.BlockSpec((1, D_MODEL), const),
        pl.BlockSpec((None, W_IN_COLS, D_MODEL), lambda b, i: (layer, 0, 0)),
        pl.BlockSpec((1, GATE_PAD), const),
        pl.BlockSpec((rows, rows), const),
        pl.BlockSpec((LANES, 2 * LANES), const),
        pl.BlockSpec((1, 2 * LANES), const),
        pl.BlockSpec((4 * N_HEADS, LANES), const),
        pl.BlockSpec((D_ATT, LANES), const),
    ]
    args = [x, mods, mods, g_pre, w_in_t, b_forget, tri, sel, one, eye, group_ones]
    n_main = len(args)
    head_blk = pl.BlockSpec((nb, N_HEADS, tb, LANES), lambda b, i: (b, 0, i, 0))
    aliases = {}
    if kv_depth is None:
        kv_shape = jax.ShapeDtypeStruct((bsz, N_HEADS, t, HEAD_DIM), F32)
        kv_spec = pl.BlockSpec((nb, N_HEADS, tb, HEAD_DIM), lambda b, i: (b, 0, i, 0))
    else:
        kv_shape = jax.ShapeDtypeStruct((kv_depth, bsz, N_HEADS, HEAD_DIM, t), F32)
        kv_spec = pl.BlockSpec((None, None, N_HEADS, HEAD_DIM, tb), lambda b, i: (layer, b, 0, 0, i))
        if kv_bufs is not None:
            in_specs += [pl.BlockSpec(memory_space=pl.ANY), pl.BlockSpec(memory_space=pl.ANY)]
            args += list(kv_bufs)
            aliases = {n_main: 3, n_main + 1: 4}
    out_shape = [
        jax.ShapeDtypeStruct((bsz, N_HEADS, t, LANES), BF16),
        jax.ShapeDtypeStruct((bsz, N_HEADS, t, LANES), BF16),
        jax.ShapeDtypeStruct((bsz, N_HEADS, t, LANES), BF16),
        kv_shape, kv_shape,
        jax.ShapeDtypeStruct(lf_shape, F32),
        jax.ShapeDtypeStruct(u_shape, F32),
        jax.ShapeDtypeStruct((bsz // nb, nblk, 8, LANES), F32),
    ]
    out_specs = [
        head_blk, head_blk, head_blk,
        kv_spec, kv_spec,
        lf_spec,
        u_spec,
        pl.BlockSpec((1, 1, 8, LANES), lambda b, i: (b, i, 0, 0)),
    ]
    n_args = len(args)

    def body(*refs):
        _inproj_kernel(*refs[:n_main], *refs[n_args:], nb=nb, tb=tb, kv_transposed=kv_depth is not None)

    return pl.pallas_call(
        body,
        grid=(bsz // nb, nblk),
        in_specs=in_specs,
        out_specs=out_specs,
        out_shape=out_shape,
        scratch_shapes=[pltpu.VMEM((1, LANES), F32)],
        input_output_aliases=aliases,
        compiler_params=pltpu.CompilerParams(
            dimension_semantics=("arbitrary", "arbitrary"), vmem_limit_bytes=VMEM_LIMIT),
        name="inproj",
    )(*args)


def _finish_heads(accs, lane):
    outs = []
    for acc in accs:
        den = jnp.sum(jnp.where(lane == DEN_LANE, acc, 0.0), axis=1, keepdims=True)
        outs.append(acc * (1.0 / den))
    return jnp.where(lane < HEAD_DIM, outs[0], pltpu.roll(outs[1], HEAD_DIM, axis=1)).astype(BF16)


def _attn_kernel(small_ref, first_ref, qa_ref, ka_ref, va_ref, o_ref, *, tq, hps):
    i = pl.program_id(2)
    row = lax.broadcasted_iota(jnp.int32, (tq, tq), 0)
    col = lax.broadcasted_iota(jnp.int32, (tq, tq), 1)
    causal = col <= row
    lane = lax.broadcasted_iota(jnp.int32, (tq, LANES), 1)
    qs = [qa_ref[0, hh] for hh in range(hps)]

    def score(j, hh, masked):
        start = pl.multiple_of(j * tq, tq)
        s = _dot_nt(qs[hh], ka_ref[0, hh, pl.ds(start, tq), :])
        return jnp.where(causal, s, MASK_VALUE) if masked else s

    def values(j, hh):
        return va_ref[0, hh, pl.ds(pl.multiple_of(j * tq, tq), tq), :]

    def store(accs):
        for pr in range(hps // 2):
            o_ref[0, :, LANES * pr:LANES * (pr + 1)] = _finish_heads(accs[2 * pr:2 * pr + 2], lane)

    def tile_plain(j, accs, masked):
        return tuple(accs[hh] + _dot(jnp.exp2(score(j, hh, masked)).astype(BF16), values(j, hh))
                     for hh in range(hps))

    def diagonal_plain(accs):
        half = tq // 2
        start = pl.multiple_of(i * tq, tq)
        upper = (lax.broadcasted_iota(jnp.int32, (half, half), 1) <=
                 lax.broadcasted_iota(jnp.int32, (half, half), 0))
        lower = (lax.broadcasted_iota(jnp.int32, (half, tq), 1) <=
                 lax.broadcasted_iota(jnp.int32, (half, tq), 0) + half)
        new = []
        for hh in range(hps):
            k_t = ka_ref[0, hh, pl.ds(start, tq), :]
            v_t = va_ref[0, hh, pl.ds(start, tq), :]
            s_top = jnp.where(upper, _dot_nt(qs[hh][0:half], k_t[0:half]), MASK_VALUE)
            s_bot = jnp.where(lower, _dot_nt(qs[hh][half:tq], k_t), MASK_VALUE)
            o_top = _dot(jnp.exp2(s_top).astype(BF16), v_t[0:half])
            o_bot = _dot(jnp.exp2(s_bot).astype(BF16), v_t)
            new.append(accs[hh] + jnp.concatenate([o_top, o_bot], axis=0))
        return new

    def tile_shifted(j, carry, masked):
        new = []
        for hh in range(hps):
            m, acc = carry[hh]
            s = score(j, hh, masked)
            m_new = jnp.maximum(m, jnp.max(s, axis=1, keepdims=True))
            p = jnp.exp2(s - m_new)
            new.append((m_new, jnp.exp2(m - m_new) * acc + _dot(p.astype(BF16), values(j, hh))))
        return tuple(new)

    small = small_ref[pl.program_id(0), pl.program_id(1)] != 0
    zero = jnp.zeros((tq, LANES), F32)

    @pl.when(small)
    def _():
        first = first_ref[pl.program_id(0), pl.program_id(1), i]
        accs = lax.fori_loop(0, i - first, lambda it, accs: tile_plain(first + it, accs, False),
                             (zero,) * hps)
        store(diagonal_plain(accs))

    @pl.when(jnp.logical_not(small))
    def _():
        init = ((jnp.full((tq, 1), MASK_VALUE, F32), zero),) * hps
        carry = lax.fori_loop(0, i, functools.partial(tile_shifted, masked=False), init)
        store([c[1] for c in tile_shifted(i, carry, True)])


def _attn_call(qa, ka, va, stats, tq, hps):
    bsz, _, t, _ = qa.shape
    nt, ng = t // tq, N_HEADS // hps
    assert stats.shape[1] == nt
    bound = jnp.sqrt(jnp.max(stats[:, :, 0, :N_HEADS], axis=1) *
                     jnp.max(stats[:, :, 1, :N_HEADS], axis=1)) * NORM_SLACK
    small = jnp.max(bound.reshape(bsz, ng, hps), axis=-1) < SCORE_LIMIT
    f_first, f_last = stats[:, :, 2, :N_HEADS] * LOG2E, stats[:, :, 3, :N_HEADS] * LOG2E
    top = bound[:, None, None, :] + f_first[:, :, None, :] - f_last[:, None, :, :]
    needed = jnp.max((top >= EXP2_ZERO_BELOW).astype(jnp.int32).reshape(bsz, nt, nt, ng, hps), axis=-1)
    first = jnp.where(small[:, :, None], jnp.sum(1 - needed, axis=2).transpose(0, 2, 1), 0)
    whole = pl.BlockSpec((1, hps, t, LANES), lambda b, p, i, *_: (b, p, 0, 0))
    return pl.pallas_call(
        functools.partial(_attn_kernel, tq=tq, hps=hps),
        grid_spec=pltpu.PrefetchScalarGridSpec(
            num_scalar_prefetch=2,
            grid=(bsz, ng, nt),
            in_specs=[pl.BlockSpec((1, hps, tq, LANES), lambda b, p, i, *_: (b, p, i, 0)), whole, whole],
            out_specs=pl.BlockSpec((1, tq, HEAD_DIM * hps), lambda b, p, i, *_: (b, i, p)),
        ),
        out_shape=jax.ShapeDtypeStruct((bsz, t, D_ATT), BF16),
        compiler_params=pltpu.CompilerParams(
            dimension_semantics=("arbitrary", "arbitrary", "arbitrary"),
            vmem_limit_bytes=VMEM_LIMIT),
        name="fox_attn",
    )(small.astype(jnp.int32), first.astype(jnp.int32), qa, ka, va)


def _attn_sample_kernel(qa_ref, ka_ref, va_ref, ckt_ref, cvt_ref, clf_ref, o_ref, *, t, n_past):
    lane_p = lax.broadcasted_iota(jnp.int32, (N_HEADS, n_past), 1)
    c = clf_ref[...]
    total = jnp.sum(c, axis=1, keepdims=True)
    d = 1
    while d < n_past:
        c = c + jnp.where(lane_p >= d, pltpu.roll(c, d, axis=1), 0.0)
        d *= 2
    f_past = (c - total) * LOG2E
    lane = lax.broadcasted_iota(jnp.int32, (t, LANES), 1)
    causal = (lax.broadcasted_iota(jnp.int32, (t, t), 1) <= lax.broadcasted_iota(jnp.int32, (t, t), 0))
    den_rows = jnp.where(lax.broadcasted_iota(jnp.int32, (LANES - HEAD_DIM, n_past), 0) == 0, 1.0, 0.0)
    accs = []
    for h in range(N_HEADS):
        q = qa_ref[0, h]
        fq = jnp.sum(jnp.where((lane >= HEAD_DIM) & (lane < HEAD_DIM + 3), q.astype(F32), 0.0),
                     axis=1, keepdims=True)
        s_past = _dot(q[:, 0:HEAD_DIM], ckt_ref[h].astype(BF16)) + fq - f_past[h:h + 1, :]
        s_new = jnp.where(causal, _dot_nt(q, ka_ref[0, h]), MASK_VALUE)
        m = jnp.maximum(jnp.max(s_past, axis=1, keepdims=True), jnp.max(s_new, axis=1, keepdims=True))
        p_past = jnp.exp2(s_past - m).astype(BF16)
        p_new = jnp.exp2(s_new - m).astype(BF16)
        cvt_aug = jnp.concatenate([cvt_ref[h], den_rows], axis=0).astype(BF16)
        accs.append(_dot_nt(p_past, cvt_aug) + _dot(p_new, va_ref[0, h]))
    for pr in range(N_HEADS // 2):
        o_ref[0, :, LANES * pr:LANES * (pr + 1)] = _finish_heads(accs[2 * pr:2 * pr + 2], lane)


def _attn_sample_call(qa, ka, va, cache_kt, cache_vt, cache_logf, layer):
    bsz, _, t, _ = qa.shape
    n_past = cache_kt.shape[4]
    cache_spec = pl.BlockSpec((None, None, N_HEADS, HEAD_DIM, n_past), lambda b: (layer, b, 0, 0, 0))
    new_spec = pl.BlockSpec((1, N_HEADS, t, LANES), lambda b: (b, 0, 0, 0))
    return pl.pallas_call(
        functools.partial(_attn_sample_kernel, t=t, n_past=n_past),
        grid=(bsz,),
        in_specs=[
            new_spec, new_spec, new_spec, cache_spec, cache_spec,
            pl.BlockSpec((None, None, N_HEADS, n_past), lambda b: (layer, b, 0, 0)),
        ],
        out_specs=pl.BlockSpec((1, t, D_ATT), lambda b: (b, 0, 0)),
        out_shape=jax.ShapeDtypeStruct((bsz, t, D_ATT), BF16),
        compiler_params=pltpu.CompilerParams(
            dimension_semantics=("arbitrary",), vmem_limit_bytes=VMEM_LIMIT),
        name="fox_attn_sample",
    )(qa, ka, va, cache_kt, cache_vt, cache_logf)


def _s5_table_kernel(ca_ref, cb_ref, ba_ref, bb_ref, prr_ref, pii_ref, dv_ref,
                     toep_ref, bpow_ref, cpow_ref, tt_scr, *, gb):
    row = lax.broadcasted_iota(jnp.int32, (SSM_GROUP, S5_ROWS), 0)
    lane = lax.broadcasted_iota(jnp.int32, (SSM_GROUP, S5_ROWS), 1)
    for gi in range(gb):
        ca, cb, ba, bb = ca_ref[gi], cb_ref[gi], ba_ref[gi], bb_ref[gi]
        power = lambda a, b, n: a * prr_ref[gi, n:n + 1, :] + b * pii_ref[gi, n:n + 1, :]
        xs = [power(ca, cb, n) for n in range(S5_CHUNK + 1)]
        cpow_ref[gi] = jnp.concatenate(xs[1:], axis=0).astype(BF16)
        bpow_ref[gi] = jnp.concatenate(
            [power(ba, bb, S5_CHUNK - 1 - s) for s in range(S5_CHUNK)], axis=0).T.astype(BF16)
        x_hi, x_lo = _split2(jnp.concatenate(xs[:S5_CHUNK], axis=0))
        b_hi, b_lo = _split2(ba)
        r0 = _dot_nt(b_hi, x_hi) + _dot_nt(b_lo, x_hi) + _dot_nt(b_hi, x_lo)
        r0 = r0 + jnp.where(lane == row, dv_ref[gi], 0.0)
        for s in range(S5_CHUNK):
            blk = r0
            if s:
                blk = jnp.where(lane >= SSM_GROUP * s, pltpu.roll(r0, SSM_GROUP * s, axis=1), 0.0)
            tt_scr[SSM_GROUP * s:SSM_GROUP * (s + 1), :] = blk
        toep_ref[gi] = tt_scr[...].T.astype(BF16)


def _s5_tables(a_re, a_im, log_dt, b_re_t, b_im_t, c_re, c_im, d_skip):
    depth = a_re.shape[0]
    dt = jnp.exp(log_dt)[..., None]
    x, y = a_re * dt, a_im * dt
    ex, cy, sy = jnp.exp(x), jnp.cos(y), jnp.sin(y)
    ar, ai = ex * cy, ex * sy
    sh = jnp.sin(0.5 * y)
    nr, ni = jnp.expm1(x) * cy - 2.0 * sh * sh, ai
    den = a_re * a_re + a_im * a_im
    fr = ((nr * a_re + ni * a_im) / den)[:, :, None, :]
    fi = ((ni * a_re - nr * a_im) / den)[:, :, None, :]
    bbr, bbi = fr * b_re_t - fi * b_im_t, fr * b_im_t + fi * b_re_t
    pr, pi = [jnp.ones_like(ar)], [jnp.zeros_like(ar)]
    for _ in range(S5_CHUNK):
        pr, pi = pr + [pr[-1] * ar - pi[-1] * ai], pi + [pr[-1] * ai + pi[-1] * ar]
    mr, mi = pr[S5_CHUNK], pi[S5_CHUNK]
    pr, pi = jnp.stack(pr, axis=2), jnp.stack(pi, axis=2)
    cat = lambda a, b: jnp.concatenate([a, b], axis=-1)
    ins = [cat(c_re, -c_im), cat(-c_im, -c_re), cat(bbr, bbi), cat(-bbi, bbr), cat(pr, pr), cat(pi, pi),
           jnp.pad(d_skip.reshape(depth, N_GROUPS, 1, SSM_GROUP), ((0, 0), (0, 0), (0, 0), (0, S5_ROWS - SSM_GROUP)))]
    gb = 8
    spec = lambda rows, cols: pl.BlockSpec((None, gb, rows, cols), lambda l, j: (l, j, 0, 0))
    tab = lambda rows, cols: jax.ShapeDtypeStruct((depth, N_GROUPS, rows, cols), BF16)
    toep, bpow, cpow = pl.pallas_call(
        functools.partial(_s5_table_kernel, gb=gb),
        grid=(depth, N_GROUPS // gb),
        in_specs=[spec(SSM_GROUP, LANES)] * 4 + [spec(S5_CHUNK + 1, LANES)] * 2 + [spec(1, S5_ROWS)],
        out_specs=[spec(S5_ROWS, S5_ROWS), spec(LANES, S5_ROWS), spec(S5_ROWS, LANES)],
        out_shape=[tab(S5_ROWS, S5_ROWS), tab(LANES, S5_ROWS), tab(S5_ROWS, LANES)],
        scratch_shapes=[pltpu.VMEM((S5_ROWS, S5_ROWS), F32)],
        compiler_params=pltpu.CompilerParams(
            dimension_semantics=("arbitrary", "arbitrary"), vmem_limit_bytes=VMEM_LIMIT),
        name="s5_tables",
    )(*ins)
    rows_r, rows_i = [], []
    for _ in range(8):
        rows_r.append(cat(mr, mr))
        rows_i.append(cat(-mi, mi))
        mr, mi = mr * mr - mi * mi, 2.0 * mr * mi
    scm = jnp.stack(rows_r + rows_i, axis=2)
    return toep, bpow, cpow, scm


def _gelu_tanh(y):
    return 0.5 * y * (1.0 + jnp.tanh(math.sqrt(2.0 / math.pi) * (y + 0.044715 * (y * y * y))))


def _s5_kernel(*refs, n, n_real, seglen, fin_row0, use_h0):
    if use_h0:
        u_ref, toep_ref, bpow_ref, cpow_ref, scm_ref, wglu_ref, h0_ref = refs[:7]
        o_ref, hfin_ref, z_scr, zt_scr, h_scr, h0_scr = refs[7:]
    else:
        u_ref, toep_ref, bpow_ref, cpow_ref, scm_ref, wglu_ref = refs[:6]
        o_ref, hfin_ref, z_scr, zt_scr, h_scr = refs[6:]
    gps = LANES // SSM_GROUP
    for pp in range(S5_PAIRS_PER_ITER):
        for part in range(2):
            h_scr[pp, part, 0:SCAN_PAD, :] = jnp.zeros((SCAN_PAD, LANES), F32)
    if use_h0:
        h0_scr[...] = jnp.zeros(h0_scr.shape, F32)

    for k in range(S5_CHUNK):
        for j in range(N_U_SLABS):
            vv = u_ref[0, j, pl.ds(k, n_real, stride=S5_CHUNK), :]
            if n_real < n:
                vv = jnp.concatenate([vv, jnp.zeros((n - n_real, LANES), F32)], axis=0)
            z_scr[gps * j:gps * (j + 1), SSM_GROUP * k:SSM_GROUP * (k + 1), :] = (
                vv.T.reshape(gps, SSM_GROUP, n).astype(BF16))

    chunk_in_seg = lax.broadcasted_iota(jnp.int32, (n, LANES), 0) % seglen
    lo = lax.broadcasted_iota(jnp.int32, (n, LANES), 1) < SSM_STATE
    lo_row = lax.broadcasted_iota(jnp.int32, (1, LANES), 1) < SSM_STATE
    lo_fin = lax.broadcasted_iota(jnp.int32, (32, LANES), 1) < SSM_STATE
    swap = lambda a: pltpu.roll(a, SSM_STATE, axis=1)

    def pair_step(it, _):
        for pp in range(S5_PAIRS_PER_ITER):
            gs = [2 * (it * S5_PAIRS_PER_ITER + pp), 2 * (it * S5_PAIRS_PER_ITER + pp) + 1]
            ucts = [z_scr[g] for g in gs]
            yts = [_dot(toep_ref[g], u) for g, u in zip(gs, ucts)]
            hs = [_dot(bpow_ref[g], u).T for g, u in zip(gs, ucts)]
            h0s = [0.0, 0.0]
            if use_h0:
                for idx, g in enumerate(gs):
                    h0_scr[idx, pl.ds(0, h0_ref.shape[1], stride=seglen), :] = h0_ref[g]
                    h0s[idx] = h0_scr[idx]
                    hs[idx] = hs[idx] + scm_ref[g, 0:1, :] * h0s[idx] + scm_ref[g, 8:9, :] * swap(h0s[idx])
            hre = jnp.where(lo, hs[0], swap(hs[1]))
            him = jnp.where(lo, swap(hs[0]), hs[1])

            def shifted(by):
                h_scr[pp, 0, SCAN_PAD:SCAN_PAD + n, :] = hre
                h_scr[pp, 1, SCAN_PAD:SCAN_PAD + n, :] = him
                return (h_scr[pp, 0, pl.ds(SCAN_PAD - by, n), :], h_scr[pp, 1, pl.ds(SCAN_PAD - by, n), :])

            d, j = 1, 0
            while d < seglen:
                sre, sim = shifted(d)
                if seglen < n:
                    sre = jnp.where(chunk_in_seg >= d, sre, 0.0)
                    sim = jnp.where(chunk_in_seg >= d, sim, 0.0)
                ar = jnp.where(lo_row, scm_ref[gs[0], j:j + 1, :], scm_ref[gs[1], j:j + 1, :])
                ai = jnp.where(lo_row, -scm_ref[gs[0], 8 + j:9 + j, :], scm_ref[gs[1], 8 + j:9 + j, :])
                hre, him = hre + ar * sre - ai * sim, him + ar * sim + ai * sre
                d, j = 2 * d, j + 1
            pre, pim = shifted(1)
            hps = [jnp.where(lo, pre, swap(pim)), jnp.where(lo, swap(pre), pim)]
            fre, fim = hre[fin_row0:fin_row0 + 32, :], him[fin_row0:fin_row0 + 32, :]
            fins = [jnp.where(lo_fin, fre, swap(fim)), jnp.where(lo_fin, swap(fre), fim)]
            for idx, g in enumerate(gs):
                hp = hps[idx]
                if seglen < n:
                    hp = jnp.where(chunk_in_seg >= 1, hp, h0s[idx])
                zt_scr[g] = _gelu_tanh(yts[idx] + _dot_nt(cpow_ref[g], hp.astype(BF16)))
                hfin_ref[0, g] = fins[idx]
        return 0

    lax.fori_loop(0, N_GROUPS // (2 * S5_PAIRS_PER_ITER), pair_step, 0)

    for k in range(S5_CHUNK):
        slabs = []
        for j in range(N_U_SLABS):
            w = zt_scr[gps * j:gps * (j + 1), SSM_GROUP * k:SSM_GROUP * (k + 1), :]
            slabs.append(w.reshape(LANES, n).T)
        zk = jnp.concatenate(slabs, axis=1)
        out = zk * _sigmoid(_dot(zk.astype(BF16), wglu_ref[...]))
        for j in range(N_U_SLABS):
            o_ref[0, j, pl.ds(k, n_real, stride=S5_CHUNK), :] = out[0:n_real, LANES * j:LANES * (j + 1)]


def _s5_call(u, toep, bpow, cpow, scm, wglu, h0, layer, n, n_real, seglen, fin_row0):
    nb, _, tt, _ = u.shape
    use_h0 = h0 is not None
    one = pl.Buffered(1)
    tab = lambda rows, cols: pl.BlockSpec((None, N_GROUPS, rows, cols), lambda b: (layer, 0, 0, 0),
                                          pipeline_mode=one)
    in_specs = [
        pl.BlockSpec((1, N_U_SLABS, tt, LANES), lambda b: (b, 0, 0, 0), pipeline_mode=one),
        tab(S5_ROWS, S5_ROWS), tab(LANES, S5_ROWS), tab(S5_ROWS, LANES), tab(16, LANES),
        pl.BlockSpec((None, D_SSM, D_SSM), lambda b: (layer, 0, 0), pipeline_mode=one),
    ]
    args = [u, toep, bpow, cpow, scm, wglu]
    scratch = [
        pltpu.VMEM((N_GROUPS, S5_ROWS, n), BF16),
        pltpu.VMEM((N_GROUPS, S5_ROWS, n), F32),
        pltpu.VMEM((S5_PAIRS_PER_ITER, 2, SCAN_PAD + n, LANES), F32),
    ]
    if use_h0:
        in_specs.append(tab(h0.shape[2], LANES))
        args.append(h0)
        scratch.append(pltpu.VMEM((2, n, LANES), F32))
    return pl.pallas_call(
        functools.partial(_s5_kernel, n=n, n_real=n_real, seglen=seglen, fin_row0=fin_row0,
                          use_h0=use_h0),
        grid=(nb,),
        in_specs=in_specs,
        out_specs=[
            pl.BlockSpec((1, N_U_SLABS, tt, LANES), lambda b: (b, 0, 0, 0), pipeline_mode=one),
            pl.BlockSpec((1, N_GROUPS, 32, LANES), lambda b: (b, 0, 0, 0)),
        ],
        out_shape=[
            jax.ShapeDtypeStruct((nb, N_U_SLABS, tt, LANES), F32),
            jax.ShapeDtypeStruct((nb, N_GROUPS, 32, LANES), F32),
        ],
        scratch_shapes=scratch,
        compiler_params=pltpu.CompilerParams(
            dimension_semantics=("arbitrary",), vmem_limit_bytes=VMEM_LIMIT),
        name="s5_mix",
    )(*args)


def _out_ffn_kernel(x_ref, att_ref, ssm_ref, g1_ref, sh2_ref, sc2_ref, g2_ref,
                    gpm_ref, gpf_ref, gpo_ref, wo_ref, wg_ref, wu_ref, wd_ref, o_ref, *, nb, tb):
    rows = nb * tb
    mix = jnp.concatenate(
        [att_ref[...].reshape(rows, D_ATT)] + [ssm_ref[0, j].astype(BF16) for j in range(N_U_SLABS)],
        axis=1)
    o = _dot(mix, wo_ref[...])
    x1 = x_ref[...].reshape(rows, D_MODEL) + _per_row(g1_ref, nb, tb) * _rms(o, gpm_ref[...])
    hf = (_rms(x1, gpf_ref[...]) * (1.0 + _per_row(sc2_ref, nb, tb)) + _per_row(sh2_ref, nb, tb)).astype(BF16)
    gt = _dot(hf, wg_ref[...])
    up = _dot(hf, wu_ref[...])
    hid = (gt * _sigmoid(gt) * up).astype(BF16)
    f = _dot(hid, wd_ref[...])
    o_ref[...] = (x1 + _per_row(g2_ref, nb, tb) * _rms(f, gpo_ref[...])).reshape(nb, tb, D_MODEL)


def _out_ffn_call(x, att, ssm, mods, layer, row0, g_post_mix, g_pre_ffn, g_post_ffn,
                  w_out, w_gate, w_up, w_down, nb, tb):
    bsz, t, _ = x.shape
    nblk = t // tb
    one = pl.Buffered(1)
    const = lambda b, i: (0, 0)
    lay = lambda b, i: (layer, 0, 0)
    if nb > 1:
        ssm_spec = pl.BlockSpec((1, N_U_SLABS, nb * tb, LANES), lambda b, i: (0, 0, b, 0))
    else:
        ssm_spec = pl.BlockSpec((1, N_U_SLABS, tb, LANES), lambda b, i: (b, 0, i, 0))
    row = pl.BlockSpec((1, D_MODEL), const)
    return pl.pallas_call(
        functools.partial(_out_ffn_kernel, nb=nb, tb=tb),
        grid=(bsz // nb, nblk),
        in_specs=[
            pl.BlockSpec((nb, tb, D_MODEL), lambda b, i: (b, i, 0)),
            pl.BlockSpec((nb, tb, D_ATT), lambda b, i: (b, i, 0)),
            ssm_spec,
            _mod_spec(layer, row0, 2, nb), _mod_spec(layer, row0, 3, nb), _mod_spec(layer, row0, 4, nb),
            _mod_spec(layer, row0, 5, nb),
            row, row, row,
            pl.BlockSpec((None, D_MODEL, D_MODEL), lay, pipeline_mode=one),
            pl.BlockSpec((None, D_MODEL, D_FF), lay, pipeline_mode=one),
            pl.BlockSpec((None, D_MODEL, D_FF), lay, pipeline_mode=one),
            pl.BlockSpec((None, D_FF, D_MODEL), lay, pipeline_mode=one),
        ],
        out_specs=pl.BlockSpec((nb, tb, D_MODEL), lambda b, i: (b, i, 0)),
        out_shape=jax.ShapeDtypeStruct((bsz, t, D_MODEL), F32),
        compiler_params=pltpu.CompilerParams(
            dimension_semantics=("arbitrary", "arbitrary"), vmem_limit_bytes=VMEM_LIMIT),
        name="out_ffn",
    )(x, att, ssm, mods, mods, mods, mods, g_post_mix, g_pre_ffn, g_post_ffn,
      w_out, w_gate, w_up, w_down)


def kernel(x_prompt, x_sample, c_prompt, c_sample, cache_k, cache_v, cache_logf, state_ssm_re,
           state_ssm_im, w_ada, b_ada, g_pre_mix, g_post_mix, g_pre_ffn, g_post_ffn, w_in, b_forget,
           ssm_a_re, ssm_a_im, ssm_log_dt, ssm_b_re, ssm_b_im, ssm_c_re, ssm_c_im, ssm_d, w_glu,
           w_out, w_gate, w_up, w_down):
    depth = w_in.shape[0]
    bp, tp, _ = x_prompt.shape
    bs, ts, _ = x_sample.shape
    n_att = 3 * D_ATT + N_HEADS

    c_all = jnp.concatenate([c_prompt, c_sample, jnp.zeros((16 - bp - bs, D_MODEL), F32)], axis=0)
    mod = _ada_call(c_all, w_ada, b_ada)
    modr = mod.reshape(depth * 16 * 6, 1, D_MODEL)
    mod_s = mod[:, bp:bp + bs].reshape(depth, bs, 6, D_MODEL).transpose(0, 2, 1, 3)

    w_in_t = jnp.swapaxes(w_in, 1, 2)
    w_in_t = jnp.concatenate(
        [w_in_t[:, :n_att], jnp.zeros((depth, GATE_PAD - N_HEADS, D_MODEL), F32), w_in_t[:, n_att:]],
        axis=1).astype(BF16)
    bf_p = jnp.pad(b_forget, ((0, 0), (0, GATE_PAD - N_HEADS)))
    w_glu_b, w_out_b = w_glu.astype(BF16), w_out.astype(BF16)
    w_gate_b, w_up_b, w_down_b = w_gate.astype(BF16), w_up.astype(BF16), w_down.astype(BF16)
    toep, bpow, cpow, scm = _s5_tables(ssm_a_re, ssm_a_im, ssm_log_dt, jnp.swapaxes(ssm_b_re, 2, 3),
                                       jnp.swapaxes(ssm_b_im, 2, 3), ssm_c_re, ssm_c_im, ssm_d)
    cache_kt, cache_vt = jnp.swapaxes(cache_k, 3, 4), jnp.swapaxes(cache_v, 3, 4)
    h0 = jnp.concatenate([state_ssm_re, state_ssm_im], axis=-1).transpose(0, 2, 1, 3)

    n_chunk_p = tp // S5_CHUNK
    seg_s = ts // S5_CHUNK
    yp, ys = x_prompt, x_sample
    kvp = None
    lfp, ssp, kss, vss, lfs, sss = [], [], [], [], [], []
    for l in range(depth):
        qa, ka, va, kbuf, vbuf, lf, u, stats = _inproj_call(
            yp, modr, l, 0, g_pre_mix[l:l + 1], w_in_t, bf_p[l:l + 1], kvp, depth, 1, 512)
        kvp = (kbuf, vbuf)
        lfp.append(lf)
        att = _attn_call(qa, ka, va, stats, 512, N_HEADS)
        ssm, hfin = _s5_call(u, toep, bpow, cpow, scm, w_glu_b, None, l, n_chunk_p, n_chunk_p,
                             n_chunk_p, n_chunk_p - 32)
        ssp.append(hfin[:, :, 31, :])
        yp = _out_ffn_call(yp, att, ssm, modr, l, 0, g_post_mix[l:l + 1], g_pre_ffn[l:l + 1],
                           g_post_ffn[l:l + 1], w_out_b, w_gate_b, w_up_b, w_down_b, 1, 512)

        qa, ka, va, k_s, v_s, lf, u, _ = _inproj_call(
            ys, mod_s, l, bp, g_pre_mix[l:l + 1], w_in_t, bf_p[l:l + 1], None, None, bs, ts)
        kss.append(k_s)
        vss.append(v_s)
        lfs.append(lf.reshape(N_HEADS, bs, ts).transpose(1, 0, 2))
        att = _attn_sample_call(qa, ka, va, cache_kt, cache_vt, cache_logf, l)
        ssm, hfin = _s5_call(u, toep, bpow, cpow, scm, w_glu_b, h0, l, LANES, bs * seg_s, seg_s, 0)
        sss.append(hfin[0, :, seg_s - 1:bs * seg_s:seg_s, :].transpose(1, 0, 2))
        ys = _out_ffn_call(ys, att, ssm, mod_s, l, bp, g_post_mix[l:l + 1], g_pre_ffn[l:l + 1],
                           g_post_ffn[l:l + 1], w_out_b, w_gate_b, w_up_b, w_down_b, bs, ts)

    ssp, sss = jnp.stack(ssp), jnp.stack(sss)
    return (yp, ys, jnp.swapaxes(kvp[0], 3, 4), jnp.swapaxes(kvp[1], 3, 4), jnp.stack(lfp),
            ssp[..., :SSM_STATE], ssp[..., SSM_STATE:],
            jnp.stack(kss), jnp.stack(vss), jnp.stack(lfs),
            sss[..., :SSM_STATE], sss[..., SSM_STATE:])
```

```python
import functools
import math

import numpy as np
import jax
import jax.numpy as jnp
from jax import lax
from jax.experimental import pallas as pl
from jax.experimental.pallas import tpu as pltpu

F32 = jnp.float32
BF16 = jnp.bfloat16

D_MODEL = 1024
N_HEADS = 8
HEAD_DIM = 64
D_ATT = N_HEADS * HEAD_DIM
D_SSM = D_MODEL - D_ATT
SSM_GROUP = 16
N_GROUPS = D_SSM // SSM_GROUP
SSM_STATE = 64
D_FF = 2816
EPS = 1e-6

LANES = 128
S5_CHUNK = 16
S5_ROWS = S5_CHUNK * SSM_GROUP
GATE_PAD = LANES
W_IN_COLS = 3 * D_ATT + GATE_PAD + D_SSM
U_COL0 = 3 * D_ATT + GATE_PAD
N_U_SLABS = D_SSM // LANES
SCAN_PAD = 128
S5_PAIRS_PER_ITER = 2
VMEM_LIMIT = 56 * 1024 * 1024
MASK_VALUE = -1e30
LOG2E = math.log2(math.e)
DEN_LANE = HEAD_DIM
N_EXTRA = 6
EXTRA_SLOT = 8
SCORE_LIMIT = 80.0
NORM_SLACK = 1.02
EXP2_ZERO_BELOW = -150.0

FIN_ROWS = 32

PROMPT_BLOCK = 512
ATTN_HEADS_PER_STEP = N_HEADS
ADA_COLS = 1536
TABLE_GROUPS = 8

NT_DIMS = (((1,), (1,)), ((), ()))


def _sigmoid(x):
    return 1.0 / (1.0 + jnp.exp(-x))


def _split3(x):
    hi = x.astype(BF16)
    r = x - hi.astype(F32)
    mid = r.astype(BF16)
    lo = (r - mid.astype(F32)).astype(BF16)
    return hi, mid, lo


def _split2(x):
    hi = x.astype(BF16)
    return hi, (x - hi.astype(F32)).astype(BF16)


def _dot(a, b):
    return jnp.dot(a, b, preferred_element_type=F32)


def _dot_nt(a, b):
    return lax.dot_general(a, b, NT_DIMS, preferred_element_type=F32)


def _rms(x, g):
    ms = jnp.mean(x * x, axis=-1, keepdims=True)
    return x * lax.rsqrt(ms + EPS) * g


def _per_row(ref, nb, tb):
    m = ref[...]
    if nb == 1:
        return m.reshape(1, D_MODEL)
    return jnp.broadcast_to(m[:, None, :], (nb, tb, D_MODEL)).reshape(nb * tb, D_MODEL)


def _ada_kernel(c_ref, w_ref, b_ref, o_ref):
    c = c_ref[...]
    a_hi, a_lo = _split2(c * _sigmoid(c))
    w_hi, w_lo = _split2(w_ref[0])
    o_ref[0] = _dot(a_hi, w_hi) + _dot(a_lo, w_hi) + _dot(a_hi, w_lo) + b_ref[0]


def _ada_call(c_all, w_ada, b_ada):
    depth = w_ada.shape[0]
    nb = ADA_COLS
    n_out = w_ada.shape[2]
    return pl.pallas_call(
        _ada_kernel,
        grid=(depth, n_out // nb),
        in_specs=[
            pl.BlockSpec((16, D_MODEL), lambda l, j: (0, 0)),
            pl.BlockSpec((1, D_MODEL, nb), lambda l, j: (l, 0, j)),
            pl.BlockSpec((1, 1, nb), lambda l, j: (l, 0, j)),
        ],
        out_specs=pl.BlockSpec((1, 16, nb), lambda l, j: (l, 0, j)),
        out_shape=jax.ShapeDtypeStruct((depth, 16, n_out), F32),
        compiler_params=pltpu.CompilerParams(
            dimension_semantics=("arbitrary", "arbitrary"), vmem_limit_bytes=VMEM_LIMIT),
        name="ada",
    )(c_all, w_ada, b_ada.reshape(depth, 1, n_out))


def _inproj_kernel(x_ref, sh_ref, sc_ref, g_ref, wt_ref, bf_ref, tri_ref, sel_ref, one_ref, eye_ref,
                   go_ref, qa_ref, ka_ref, va_ref, kc_ref, vc_ref, lf_ref, u_ref, st_ref, carry_ref,
                   *, nb, tb, kv_transposed):
    rows = nb * tb

    @pl.when(pl.program_id(1) == 0)
    def _():
        carry_ref[...] = jnp.zeros_like(carry_ref)

    x = x_ref[...].reshape(rows, D_MODEL)
    hm = _rms(x, g_ref[...]) * (1.0 + _per_row(sc_ref, nb, tb)) + _per_row(sh_ref, nb, tb)
    proj = _dot_nt(hm.astype(BF16), wt_ref[...])
    q = proj[:, 0:D_ATT] * (LOG2E * HEAD_DIM ** -0.5)
    k = proj[:, D_ATT:2 * D_ATT]
    v = proj[:, 2 * D_ATT:3 * D_ATT]
    gate = proj[:, 3 * D_ATT:U_COL0] + bf_ref[...]
    logf = jnp.minimum(gate, 0.0) - jnp.log1p(jnp.exp(-jnp.abs(gate)))
    lane = lax.broadcasted_iota(jnp.int32, (rows, LANES), 1)

    def pack3(val):
        hi, mid, lo = (part.astype(F32) for part in _split3(val))
        packed = jnp.where(lane < 2 * N_HEADS, pltpu.roll(mid, N_HEADS, axis=1),
                           jnp.where(lane < 3 * N_HEADS, pltpu.roll(lo, 2 * N_HEADS, axis=1), 0.0))
        return jnp.where(lane < N_HEADS, hi, packed).astype(BF16)

    lf_pack = pack3(logf)
    cs = _dot(tri_ref[...], lf_pack)
    fcum = (cs + pltpu.roll(cs, LANES - N_HEADS, axis=1) + pltpu.roll(cs, LANES - 2 * N_HEADS, axis=1)
            + carry_ref[...])
    carry_ref[...] = fcum[rows - 1:rows, :]

    extra = _dot(pack3(fcum * LOG2E), sel_ref[...]) + one_ref[...]
    in_extra = (lane >= HEAD_DIM) & (lane < HEAD_DIM + N_EXTRA)
    qn2 = _dot((q * q).astype(BF16), go_ref[...])
    kn2 = _dot((k * k).astype(BF16), go_ref[...])
    st_ref[0, 0] = jnp.concatenate(
        [jnp.max(qn2, axis=0, keepdims=True), jnp.max(kn2, axis=0, keepdims=True),
         fcum[0:1, :], fcum[rows - 1:rows, :], jnp.zeros((4, LANES), F32)], axis=0)
    lo_half = lane < HEAD_DIM
    den_one = jnp.where(lane == DEN_LANE, 1.0, 0.0)
    if kv_transposed:
        kt, vt = k.T, v.T
    for h in range(N_HEADS):
        pair = slice(LANES * (h // 2), LANES * (h // 2) + LANES)
        q2, k2, v2 = q[:, pair], k[:, pair], v[:, pair]
        if h % 2:
            q2 = pltpu.roll(q2, HEAD_DIM, axis=1)
            k2 = pltpu.roll(k2, HEAD_DIM, axis=1)
            v2 = pltpu.roll(v2, HEAD_DIM, axis=1)
        per_stream = lambda val: val.reshape(nb, tb, val.shape[-1])
        to_extra = lambda side: jnp.where(in_extra, pltpu.roll(
            extra[:, LANES * side:LANES * (side + 1)], (HEAD_DIM - EXTRA_SLOT * h) % LANES, axis=1), 0.0)
        qa_ref[:, h] = per_stream(jnp.where(lo_half, q2, to_extra(0)).astype(BF16))
        ka_ref[:, h] = per_stream(jnp.where(lo_half, k2, to_extra(1)).astype(BF16))
        va_ref[:, h] = per_stream(jnp.where(lo_half, v2, den_one).astype(BF16))
        if kv_transposed:
            kc_ref[h] = kt[HEAD_DIM * h:HEAD_DIM * (h + 1), :]
            vc_ref[h] = vt[HEAD_DIM * h:HEAD_DIM * (h + 1), :]
        else:
            kc_ref[:, h] = per_stream(k2[:, 0:HEAD_DIM])
            vc_ref[:, h] = per_stream(v2[:, 0:HEAD_DIM])
    lf_t = _dot_nt(eye_ref[...], lf_pack)
    lf_ref[0] = lf_t[0:N_HEADS] + lf_t[N_HEADS:2 * N_HEADS] + lf_t[2 * N_HEADS:3 * N_HEADS]
    for j in range(N_U_SLABS):
        u_ref[0, j] = proj[:, U_COL0 + LANES * j:U_COL0 + LANES * (j + 1)]


def _inproj_consts(nb, tb):
    tri = np.kron(np.eye(nb, dtype=np.float32), np.tril(np.ones((tb, tb), np.float32)))
    sel = np.zeros((LANES, 2 * LANES), np.float32)
    one = np.zeros((1, 2 * LANES), np.float32)
    group_ones = np.zeros((D_ATT, LANES), np.float32)
    for h in range(N_HEADS):
        group_ones[HEAD_DIM * h:HEAD_DIM * (h + 1), h] = 1.0
        for p in range(3):
            sel[p * N_HEADS + h, EXTRA_SLOT * h + p] = 1.0
            one[0, EXTRA_SLOT * h + 3 + p] = 1.0
            one[0, LANES + EXTRA_SLOT * h + p] = 1.0
            sel[p * N_HEADS + h, LANES + EXTRA_SLOT * h + 3 + p] = -1.0
    eye = np.eye(4 * N_HEADS, LANES, dtype=np.float32)
    eye[3 * N_HEADS:] = 0.0
    return (jnp.asarray(tri, BF16), jnp.asarray(sel, BF16), jnp.asarray(one, F32), jnp.asarray(eye, BF16),
            jnp.asarray(group_ones, BF16))


def _mod_spec(layer, row0, chunk, nb=1):
    if nb == 1:
        return pl.BlockSpec((1, 1, D_MODEL), lambda b, i: ((layer * 16 + row0 + b) * 6 + chunk, 0, 0))
    return pl.BlockSpec((None, None, nb, D_MODEL), lambda b, i: (layer, chunk, 0, 0))


def _inproj_call(x, mods, layer, row0, g_pre, w_in_t, b_forget, kv_bufs, kv_depth, nb, tb):
    bsz, t, _ = x.shape
    nblk = t // tb
    rows = nb * tb
    assert nb == 1 or (nblk == 1 and kv_depth is None)
    tri, sel, one, eye, group_ones = _inproj_consts(nb, tb)
    if nb > 1:
        u_shape = (1, N_U_SLABS, bsz * t, LANES)
        u_spec = pl.BlockSpec((1, N_U_SLABS, rows, LANES), lambda b, i: (0, 0, b, 0))
        lf_shape = (1, N_HEADS, bsz * t)
        lf_spec = pl.BlockSpec((1, N_HEADS, rows), lambda b, i: (0, 0, b))
    else:
        u_shape = (bsz, N_U_SLABS, t, LANES)
        u_spec = pl.BlockSpec((1, N_U_SLABS, tb, LANES), lambda b, i: (b, 0, i, 0))
        lf_shape = (bsz, N_HEADS, t)
        lf_spec = pl.BlockSpec((1, N_HEADS, tb), lambda b, i: (b, 0, i))
    const = lambda b, i: (0, 0)
    in_specs = [
        pl.BlockSpec((nb, tb, D_MODEL), lambda b, i: (b, i, 0)),
        _mod_spec(layer, row0, 0, nb), _mod_spec(layer, row0, 1, nb),
        pl.BlockSpec((1, D_MODEL), const),
        pl.BlockSpec((None, W_IN_COLS, D_MODEL), lambda b, i: (layer, 0, 0)),
        pl.BlockSpec((1, GATE_PAD), const),
        pl.BlockSpec((rows, rows), const),
        pl.BlockSpec((LANES, 2 * LANES), const),
        pl.BlockSpec((1, 2 * LANES), const),
        pl.BlockSpec((4 * N_HEADS, LANES), const),
        pl.BlockSpec((D_ATT, LANES), const),
    ]
    args = [x, mods, mods, g_pre, w_in_t, b_forget, tri, sel, one, eye, group_ones]
    n_main = len(args)
    head_blk = pl.BlockSpec((nb, N_HEADS, tb, LANES), lambda b, i: (b, 0, i, 0))
    aliases = {}
    if kv_depth is None:
        kv_shape = jax.ShapeDtypeStruct((bsz, N_HEADS, t, HEAD_DIM), F32)
        kv_spec = pl.BlockSpec((nb, N_HEADS, tb, HEAD_DIM), lambda b, i: (b, 0, i, 0))
    else:
        kv_shape = jax.ShapeDtypeStruct((kv_depth, bsz, N_HEADS, HEAD_DIM, t), F32)
        kv_spec = pl.BlockSpec((None, None, N_HEADS, HEAD_DIM, tb), lambda b, i: (layer, b, 0, 0, i))
        if kv_bufs is not None:
            in_specs += [pl.BlockSpec(memory_space=pl.ANY), pl.BlockSpec(memory_space=pl.ANY)]
            args += list(kv_bufs)
            aliases = {n_main: 3, n_main + 1: 4}
    out_shape = [
        jax.ShapeDtypeStruct((bsz, N_HEADS, t, LANES), BF16),
        jax.ShapeDtypeStruct((bsz, N_HEADS, t, LANES), BF16),
        jax.ShapeDtypeStruct((bsz, N_HEADS, t, LANES), BF16),
        kv_shape, kv_shape,
        jax.ShapeDtypeStruct(lf_shape, F32),
        jax.ShapeDtypeStruct(u_shape, F32),
        jax.ShapeDtypeStruct((bsz // nb, nblk, 8, LANES), F32),
    ]
    out_specs = [
        head_blk, head_blk, head_blk,
        kv_spec, kv_spec,
        lf_spec,
        u_spec,
        pl.BlockSpec((1, 1, 8, LANES), lambda b, i: (b, i, 0, 0)),
    ]
    n_args = len(args)

    def body(*refs):
        _inproj_kernel(*refs[:n_main], *refs[n_args:], nb=nb, tb=tb, kv_transposed=kv_depth is not None)

    return pl.pallas_call(
        body,
        grid=(bsz // nb, nblk),
        in_specs=in_specs,
        out_specs=out_specs,
        out_shape=out_shape,
        scratch_shapes=[pltpu.VMEM((1, LANES), F32)],
        input_output_aliases=aliases,
        compiler_params=pltpu.CompilerParams(
            dimension_semantics=("arbitrary", "arbitrary"), vmem_limit_bytes=VMEM_LIMIT),
        name="inproj",
    )(*args)


def _finish_heads(accs, lane):
    outs = []
    for acc in accs:
        den = jnp.sum(jnp.where(lane == DEN_LANE, acc, 0.0), axis=1, keepdims=True)
        outs.append(acc * (1.0 / den))
    return jnp.where(lane < HEAD_DIM, outs[0], pltpu.roll(outs[1], HEAD_DIM, axis=1)).astype(BF16)


def _attn_kernel(small_ref, first_ref, qa_ref, ka_ref, va_ref, o_ref, *, tq, hps):
    i = pl.program_id(2)
    row = lax.broadcasted_iota(jnp.int32, (tq, tq), 0)
    col = lax.broadcasted_iota(jnp.int32, (tq, tq), 1)
    causal = col <= row
    lane = lax.broadcasted_iota(jnp.int32, (tq, LANES), 1)
    qs = [qa_ref[0, hh] for hh in range(hps)]

    def score(j, hh, masked):
        start = pl.multiple_of(j * tq, tq)
        s = _dot_nt(qs[hh], ka_ref[0, hh, pl.ds(start, tq), :])
        return jnp.where(causal, s, MASK_VALUE) if masked else s

    def values(j, hh):
        return va_ref[0, hh, pl.ds(pl.multiple_of(j * tq, tq), tq), :]

    def store(accs):
        for pr in range(hps // 2):
            o_ref[0, :, LANES * pr:LANES * (pr + 1)] = _finish_heads(accs[2 * pr:2 * pr + 2], lane)

    def tile_plain(j, accs, masked):
        return tuple(accs[hh] + _dot(jnp.exp2(score(j, hh, masked)).astype(BF16), values(j, hh))
                     for hh in range(hps))

    def diagonal_plain(accs):
        half = tq // 2
        start = pl.multiple_of(i * tq, tq)
        upper = (lax.broadcasted_iota(jnp.int32, (half, half), 1) <=
                 lax.broadcasted_iota(jnp.int32, (half, half), 0))
        lower = (lax.broadcasted_iota(jnp.int32, (half, tq), 1) <=
                 lax.broadcasted_iota(jnp.int32, (half, tq), 0) + half)
        new = []
        for hh in range(hps):
            k_t = ka_ref[0, hh, pl.ds(start, tq), :]
            v_t = va_ref[0, hh, pl.ds(start, tq), :]
            s_top = jnp.where(upper, _dot_nt(qs[hh][0:half], k_t[0:half]), MASK_VALUE)
            s_bot = jnp.where(lower, _dot_nt(qs[hh][half:tq], k_t), MASK_VALUE)
            o_top = _dot(jnp.exp2(s_top).astype(BF16), v_t[0:half])
            o_bot = _dot(jnp.exp2(s_bot).astype(BF16), v_t)
            new.append(accs[hh] + jnp.concatenate([o_top, o_bot], axis=0))
        return new

    def tile_shifted(j, carry, masked):
        new = []
        for hh in range(hps):
            m, acc = carry[hh]
            s = score(j, hh, masked)
            m_new = jnp.maximum(m, jnp.max(s, axis=1, keepdims=True))
            p = jnp.exp2(s - m_new)
            new.append((m_new, jnp.exp2(m - m_new) * acc + _dot(p.astype(BF16), values(j, hh))))
        return tuple(new)

    small = small_ref[pl.program_id(0), pl.program_id(1)] != 0
    zero = jnp.zeros((tq, LANES), F32)

    @pl.when(small)
    def _():
        first = first_ref[pl.program_id(0), pl.program_id(1), i]
        accs = lax.fori_loop(0, i - first, lambda it, accs: tile_plain(first + it, accs, False),
                             (zero,) * hps)
        store(diagonal_plain(accs))

    @pl.when(jnp.logical_not(small))
    def _():
        init = ((jnp.full((tq, 1), MASK_VALUE, F32), zero),) * hps
        carry = lax.fori_loop(0, i, functools.partial(tile_shifted, masked=False), init)
        store([c[1] for c in tile_shifted(i, carry, True)])


def _attn_call(qa, ka, va, stats, tq, hps):
    bsz, _, t, _ = qa.shape
    nt, ng = t // tq, N_HEADS // hps
    assert stats.shape[1] == nt
    bound = jnp.sqrt(jnp.max(stats[:, :, 0, :N_HEADS], axis=1) *
                     jnp.max(stats[:, :, 1, :N_HEADS], axis=1)) * NORM_SLACK
    small = jnp.max(bound.reshape(bsz, ng, hps), axis=-1) < SCORE_LIMIT
    f_first, f_last = stats[:, :, 2, :N_HEADS] * LOG2E, stats[:, :, 3, :N_HEADS] * LOG2E
    top = bound[:, None, None, :] + f_first[:, :, None, :] - f_last[:, None, :, :]
    needed = jnp.max((top >= EXP2_ZERO_BELOW).astype(jnp.int32).reshape(bsz, nt, nt, ng, hps), axis=-1)
    first = jnp.where(small[:, :, None], jnp.sum(1 - needed, axis=2).transpose(0, 2, 1), 0)
    whole = pl.BlockSpec((1, hps, t, LANES), lambda b, p, i, *_: (b, p, 0, 0))
    return pl.pallas_call(
        functools.partial(_attn_kernel, tq=tq, hps=hps),
        grid_spec=pltpu.PrefetchScalarGridSpec(
            num_scalar_prefetch=2,
            grid=(bsz, ng, nt),
            in_specs=[pl.BlockSpec((1, hps, tq, LANES), lambda b, p, i, *_: (b, p, i, 0)), whole, whole],
            out_specs=pl.BlockSpec((1, tq, HEAD_DIM * hps), lambda b, p, i, *_: (b, i, p)),
        ),
        out_shape=jax.ShapeDtypeStruct((bsz, t, D_ATT), BF16),
        compiler_params=pltpu.CompilerParams(
            dimension_semantics=("arbitrary", "arbitrary", "arbitrary"),
            vmem_limit_bytes=VMEM_LIMIT),
        name="fox_attn",
    )(small.astype(jnp.int32), first.astype(jnp.int32), qa, ka, va)


def _attn_sample_kernel(qa_ref, ka_ref, va_ref, ckt_ref, cvt_ref, clf_ref, o_ref, *, t, n_past):
    lane_p = lax.broadcasted_iota(jnp.int32, (N_HEADS, n_past), 1)
    c = clf_ref[...]
    total = jnp.sum(c, axis=1, keepdims=True)
    d = 1
    while d < n_past:
        c = c + jnp.where(lane_p >= d, pltpu.roll(c, d, axis=1), 0.0)
        d *= 2
    f_past = (c - total) * LOG2E
    lane = lax.broadcasted_iota(jnp.int32, (t, LANES), 1)
    causal = (lax.broadcasted_iota(jnp.int32, (t, t), 1) <= lax.broadcasted_iota(jnp.int32, (t, t), 0))
    den_rows = jnp.where(lax.broadcasted_iota(jnp.int32, (LANES - HEAD_DIM, n_past), 0) == 0, 1.0, 0.0)
    accs = []
    for h in range(N_HEADS):
        q = qa_ref[0, h]
        fq = jnp.sum(jnp.where((lane >= HEAD_DIM) & (lane < HEAD_DIM + 3), q.astype(F32), 0.0),
                     axis=1, keepdims=True)
        s_past = _dot(q[:, 0:HEAD_DIM], ckt_ref[h].astype(BF16)) + fq - f_past[h:h + 1, :]
        s_new = jnp.where(causal, _dot_nt(q, ka_ref[0, h]), MASK_VALUE)
        m = jnp.maximum(jnp.max(s_past, axis=1, keepdims=True), jnp.max(s_new, axis=1, keepdims=True))
        p_past = jnp.exp2(s_past - m).astype(BF16)
        p_new = jnp.exp2(s_new - m).astype(BF16)
        cvt_aug = jnp.concatenate([cvt_ref[h], den_rows], axis=0).astype(BF16)
        accs.append(_dot_nt(p_past, cvt_aug) + _dot(p_new, va_ref[0, h]))
    for pr in range(N_HEADS // 2):
        o_ref[0, :, LANES * pr:LANES * (pr + 1)] = _finish_heads(accs[2 * pr:2 * pr + 2], lane)


def _attn_sample_call(qa, ka, va, cache_kt, cache_vt, cache_logf, layer):
    bsz, _, t, _ = qa.shape
    n_past = cache_kt.shape[4]
    cache_spec = pl.BlockSpec((None, None, N_HEADS, HEAD_DIM, n_past), lambda b: (layer, b, 0, 0, 0))
    new_spec = pl.BlockSpec((1, N_HEADS, t, LANES), lambda b: (b, 0, 0, 0))
    return pl.pallas_call(
        functools.partial(_attn_sample_kernel, t=t, n_past=n_past),
        grid=(bsz,),
        in_specs=[
            new_spec, new_spec, new_spec, cache_spec, cache_spec,
            pl.BlockSpec((None, None, N_HEADS, n_past), lambda b: (layer, b, 0, 0)),
        ],
        out_specs=pl.BlockSpec((1, t, D_ATT), lambda b: (b, 0, 0)),
        out_shape=jax.ShapeDtypeStruct((bsz, t, D_ATT), BF16),
        compiler_params=pltpu.CompilerParams(
            dimension_semantics=("arbitrary",), vmem_limit_bytes=VMEM_LIMIT),
        name="fox_attn_sample",
    )(qa, ka, va, cache_kt, cache_vt, cache_logf)


def _s5_table_kernel(ca_ref, cb_ref, ba_ref, bb_ref, prr_ref, pii_ref, dv_ref,
                     toep_ref, bpow_ref, cpow_ref, tt_scr, *, gb):
    row = lax.broadcasted_iota(jnp.int32, (SSM_GROUP, S5_ROWS), 0)
    lane = lax.broadcasted_iota(jnp.int32, (SSM_GROUP, S5_ROWS), 1)
    for gi in range(gb):
        ca, cb, ba, bb = ca_ref[gi], cb_ref[gi], ba_ref[gi], bb_ref[gi]
        power = lambda a, b, n: a * prr_ref[gi, n:n + 1, :] + b * pii_ref[gi, n:n + 1, :]
        xs = [power(ca, cb, n) for n in range(S5_CHUNK + 1)]
        cpow_ref[gi] = jnp.concatenate(xs[1:], axis=0).astype(BF16)
        bpow_ref[gi] = jnp.concatenate(
            [power(ba, bb, S5_CHUNK - 1 - s) for s in range(S5_CHUNK)], axis=0).T.astype(BF16)
        x_hi, x_lo = _split2(jnp.concatenate(xs[:S5_CHUNK], axis=0))
        b_hi, b_lo = _split2(ba)
        r0 = _dot_nt(b_hi, x_hi) + _dot_nt(b_lo, x_hi) + _dot_nt(b_hi, x_lo)
        r0 = r0 + jnp.where(lane == row, dv_ref[gi], 0.0)
        for s in range(S5_CHUNK):
            blk = r0
            if s:
                blk = jnp.where(lane >= SSM_GROUP * s, pltpu.roll(r0, SSM_GROUP * s, axis=1), 0.0)
            tt_scr[SSM_GROUP * s:SSM_GROUP * (s + 1), :] = blk
        toep_ref[gi] = tt_scr[...].T.astype(BF16)


def _s5_tables(a_re, a_im, log_dt, b_re_t, b_im_t, c_re, c_im, d_skip):
    depth = a_re.shape[0]
    dt = jnp.exp(log_dt)[..., None]
    x, y = a_re * dt, a_im * dt
    ex, cy, sy = jnp.exp(x), jnp.cos(y), jnp.sin(y)
    ar, ai = ex * cy, ex * sy
    sh = jnp.sin(0.5 * y)
    nr, ni = jnp.expm1(x) * cy - 2.0 * sh * sh, ai
    den = a_re * a_re + a_im * a_im
    fr = ((nr * a_re + ni * a_im) / den)[:, :, None, :]
    fi = ((ni * a_re - nr * a_im) / den)[:, :, None, :]
    bbr, bbi = fr * b_re_t - fi * b_im_t, fr * b_im_t + fi * b_re_t
    pr, pi = [jnp.ones_like(ar)], [jnp.zeros_like(ar)]
    for _ in range(S5_CHUNK):
        pr, pi = pr + [pr[-1] * ar - pi[-1] * ai], pi + [pr[-1] * ai + pi[-1] * ar]
    mr, mi = pr[S5_CHUNK], pi[S5_CHUNK]
    pr, pi = jnp.stack(pr, axis=2), jnp.stack(pi, axis=2)
    cat = lambda a, b: jnp.concatenate([a, b], axis=-1)
    ins = [cat(c_re, -c_im), cat(-c_im, -c_re), cat(bbr, bbi), cat(-bbi, bbr), cat(pr, pr), cat(pi, pi),
           jnp.pad(d_skip.reshape(depth, N_GROUPS, 1, SSM_GROUP), ((0, 0), (0, 0), (0, 0), (0, S5_ROWS - SSM_GROUP)))]
    gb = TABLE_GROUPS
    spec = lambda rows, cols: pl.BlockSpec((None, gb, rows, cols), lambda l, j: (l, j, 0, 0))
    tab = lambda rows, cols: jax.ShapeDtypeStruct((depth, N_GROUPS, rows, cols), BF16)
    toep, bpow, cpow = pl.pallas_call(
        functools.partial(_s5_table_kernel, gb=gb),
        grid=(depth, N_GROUPS // gb),
        in_specs=[spec(SSM_GROUP, LANES)] * 4 + [spec(S5_CHUNK + 1, LANES)] * 2 + [spec(1, S5_ROWS)],
        out_specs=[spec(S5_ROWS, S5_ROWS), spec(LANES, S5_ROWS), spec(S5_ROWS, LANES)],
        out_shape=[tab(S5_ROWS, S5_ROWS), tab(LANES, S5_ROWS), tab(S5_ROWS, LANES)],
        scratch_shapes=[pltpu.VMEM((S5_ROWS, S5_ROWS), F32)],
        compiler_params=pltpu.CompilerParams(
            dimension_semantics=("arbitrary", "arbitrary"), vmem_limit_bytes=VMEM_LIMIT),
        name="s5_tables",
    )(*ins)
    rows_r, rows_i = [], []
    for _ in range(8):
        rows_r.append(cat(mr, mr))
        rows_i.append(cat(-mi, mi))
        mr, mi = mr * mr - mi * mi, 2.0 * mr * mi
    scm = jnp.stack(rows_r + rows_i, axis=2)
    return toep, bpow, cpow, scm


def _gelu_tanh(y):
    return 0.5 * y * (1.0 + jnp.tanh(math.sqrt(2.0 / math.pi) * (y + 0.044715 * (y * y * y))))


def _s5_kernel(*refs, n, n_real, seglen, fin_row0, use_h0):
    if use_h0:
        u_ref, toep_ref, bpow_ref, cpow_ref, scm_ref, wglu_ref, h0_ref = refs[:7]
        o_ref, hfin_ref, z_scr, zt_scr, h_scr, h0_scr = refs[7:]
    else:
        u_ref, toep_ref, bpow_ref, cpow_ref, scm_ref, wglu_ref = refs[:6]
        o_ref, hfin_ref, z_scr, zt_scr, h_scr = refs[6:]
    gps = LANES // SSM_GROUP
    for pp in range(S5_PAIRS_PER_ITER):
        for part in range(2):
            h_scr[pp, part, 0:SCAN_PAD, :] = jnp.zeros((SCAN_PAD, LANES), F32)
    if use_h0:
        h0_scr[...] = jnp.zeros(h0_scr.shape, F32)

    for k in range(S5_CHUNK):
        for j in range(N_U_SLABS):
            vv = u_ref[0, j, pl.ds(k, n_real, stride=S5_CHUNK), :]
            if n_real < n:
                vv = jnp.concatenate([vv, jnp.zeros((n - n_real, LANES), F32)], axis=0)
            z_scr[gps * j:gps * (j + 1), SSM_GROUP * k:SSM_GROUP * (k + 1), :] = (
                vv.T.reshape(gps, SSM_GROUP, n).astype(BF16))

    chunk_in_seg = lax.broadcasted_iota(jnp.int32, (n, LANES), 0) % seglen
    lo = lax.broadcasted_iota(jnp.int32, (n, LANES), 1) < SSM_STATE
    lo_row = lax.broadcasted_iota(jnp.int32, (1, LANES), 1) < SSM_STATE
    lo_fin = lax.broadcasted_iota(jnp.int32, (FIN_ROWS, LANES), 1) < SSM_STATE
    swap = lambda a: pltpu.roll(a, SSM_STATE, axis=1)

    def pair_step(it, _):
        for pp in range(S5_PAIRS_PER_ITER):
            gs = [2 * (it * S5_PAIRS_PER_ITER + pp), 2 * (it * S5_PAIRS_PER_ITER + pp) + 1]
            ucts = [z_scr[g] for g in gs]
            yts = [_dot(toep_ref[g], u) for g, u in zip(gs, ucts)]
            hs = [_dot(bpow_ref[g], u).T for g, u in zip(gs, ucts)]
            h0s = [0.0, 0.0]
            if use_h0:
                for idx, g in enumerate(gs):
                    h0_scr[idx, pl.ds(0, h0_ref.shape[1], stride=seglen), :] = h0_ref[g]
                    h0s[idx] = h0_scr[idx]
                    hs[idx] = hs[idx] + scm_ref[g, 0:1, :] * h0s[idx] + scm_ref[g, 8:9, :] * swap(h0s[idx])
            hre = jnp.where(lo, hs[0], swap(hs[1]))
            him = jnp.where(lo, swap(hs[0]), hs[1])

            def shifted(by):
                h_scr[pp, 0, SCAN_PAD:SCAN_PAD + n, :] = hre
                h_scr[pp, 1, SCAN_PAD:SCAN_PAD + n, :] = him
                return (h_scr[pp, 0, pl.ds(SCAN_PAD - by, n), :], h_scr[pp, 1, pl.ds(SCAN_PAD - by, n), :])

            d, j = 1, 0
            while d < seglen:
                sre, sim = shifted(d)
                if seglen < n:
                    sre = jnp.where(chunk_in_seg >= d, sre, 0.0)
                    sim = jnp.where(chunk_in_seg >= d, sim, 0.0)
                ar = jnp.where(lo_row, scm_ref[gs[0], j:j + 1, :], scm_ref[gs[1], j:j + 1, :])
                ai = jnp.where(lo_row, -scm_ref[gs[0], 8 + j:9 + j, :], scm_ref[gs[1], 8 + j:9 + j, :])
                hre, him = hre + ar * sre - ai * sim, him + ar * sim + ai * sre
                d, j = 2 * d, j + 1
            pre, pim = shifted(1)
            hps = [jnp.where(lo, pre, swap(pim)), jnp.where(lo, swap(pre), pim)]
            fre, fim = hre[fin_row0:fin_row0 + FIN_ROWS, :], him[fin_row0:fin_row0 + FIN_ROWS, :]
            fins = [jnp.where(lo_fin, fre, swap(fim)), jnp.where(lo_fin, swap(fre), fim)]
            for idx, g in enumerate(gs):
                hp = hps[idx]
                if seglen < n:
                    hp = jnp.where(chunk_in_seg >= 1, hp, h0s[idx])
                zt_scr[g] = _gelu_tanh(yts[idx] + _dot_nt(cpow_ref[g], hp.astype(BF16)))
                hfin_ref[0, g] = fins[idx]
        return 0

    lax.fori_loop(0, N_GROUPS // (2 * S5_PAIRS_PER_ITER), pair_step, 0)

    for k in range(S5_CHUNK):
        slabs = []
        for j in range(N_U_SLABS):
            w = zt_scr[gps * j:gps * (j + 1), SSM_GROUP * k:SSM_GROUP * (k + 1), :]
            slabs.append(w.reshape(LANES, n).T)
        zk = jnp.concatenate(slabs, axis=1)
        out = zk * _sigmoid(_dot(zk.astype(BF16), wglu_ref[...]))
        for j in range(N_U_SLABS):
            o_ref[0, j, pl.ds(k, n_real, stride=S5_CHUNK), :] = out[0:n_real, LANES * j:LANES * (j + 1)]


def _s5_call(u, toep, bpow, cpow, scm, wglu, h0, layer, n, n_real, seglen, fin_row0):
    nb, _, tt, _ = u.shape
    use_h0 = h0 is not None
    one = pl.Buffered(1)
    tab = lambda rows, cols: pl.BlockSpec((None, N_GROUPS, rows, cols), lambda b: (layer, 0, 0, 0),
                                          pipeline_mode=one)
    in_specs = [
        pl.BlockSpec((1, N_U_SLABS, tt, LANES), lambda b: (b, 0, 0, 0)),
        tab(S5_ROWS, S5_ROWS), tab(LANES, S5_ROWS), tab(S5_ROWS, LANES), tab(16, LANES),
        pl.BlockSpec((None, D_SSM, D_SSM), lambda b: (layer, 0, 0), pipeline_mode=one),
    ]
    args = [u, toep, bpow, cpow, scm, wglu]
    scratch = [
        pltpu.VMEM((N_GROUPS, S5_ROWS, n), BF16),
        pltpu.VMEM((N_GROUPS, S5_ROWS, n), F32),
        pltpu.VMEM((S5_PAIRS_PER_ITER, 2, SCAN_PAD + n, LANES), F32),
    ]
    if use_h0:
        in_specs.append(tab(h0.shape[2], LANES))
        args.append(h0)
        scratch.append(pltpu.VMEM((2, n, LANES), F32))
    return pl.pallas_call(
        functools.partial(_s5_kernel, n=n, n_real=n_real, seglen=seglen, fin_row0=fin_row0,
                          use_h0=use_h0),
        grid=(nb,),
        in_specs=in_specs,
        out_specs=[
            pl.BlockSpec((1, N_U_SLABS, tt, LANES), lambda b: (b, 0, 0, 0)),
            pl.BlockSpec((1, N_GROUPS, FIN_ROWS, LANES), lambda b: (b, 0, 0, 0)),
        ],
        out_shape=[
            jax.ShapeDtypeStruct((nb, N_U_SLABS, tt, LANES), F32),
            jax.ShapeDtypeStruct((nb, N_GROUPS, FIN_ROWS, LANES), F32),
        ],
        scratch_shapes=scratch,
        compiler_params=pltpu.CompilerParams(
            dimension_semantics=("arbitrary",), vmem_limit_bytes=VMEM_LIMIT),
        name="s5_mix",
    )(*args)


def _out_ffn_kernel(x_ref, att_ref, ssm_ref, g1_ref, sh2_ref, sc2_ref, g2_ref,
                    gpm_ref, gpf_ref, gpo_ref, wo_ref, wg_ref, wu_ref, wd_ref, o_ref, *, nb, tb):
    rows = nb * tb
    mix = jnp.concatenate(
        [att_ref[...].reshape(rows, D_ATT)] + [ssm_ref[0, j].astype(BF16) for j in range(N_U_SLABS)],
        axis=1)
    o = _dot(mix, wo_ref[...])
    x1 = x_ref[...].reshape(rows, D_MODEL) + _per_row(g1_ref, nb, tb) * _rms(o, gpm_ref[...])
    hf = (_rms(x1, gpf_ref[...]) * (1.0 + _per_row(sc2_ref, nb, tb)) + _per_row(sh2_ref, nb, tb)).astype(BF16)
    gt = _dot(hf, wg_ref[...])
    up = _dot(hf, wu_ref[...])
    hid = (gt * _sigmoid(gt) * up).astype(BF16)
    f = _dot(hid, wd_ref[...])
    o_ref[...] = (x1 + _per_row(g2_ref, nb, tb) * _rms(f, gpo_ref[...])).reshape(nb, tb, D_MODEL)


def _out_ffn_call(x, att, ssm, mods, layer, row0, g_post_mix, g_pre_ffn, g_post_ffn,
                  w_out, w_gate, w_up, w_down, nb, tb):
    bsz, t, _ = x.shape
    nblk = t // tb
    one = pl.Buffered(1)
    const = lambda b, i: (0, 0)
    lay = lambda b, i: (layer, 0, 0)
    if nb > 1:
        ssm_spec = pl.BlockSpec((1, N_U_SLABS, nb * tb, LANES), lambda b, i: (0, 0, b, 0))
    else:
        ssm_spec = pl.BlockSpec((1, N_U_SLABS, tb, LANES), lambda b, i: (b, 0, i, 0))
    row = pl.BlockSpec((1, D_MODEL), const)
    return pl.pallas_call(
        functools.partial(_out_ffn_kernel, nb=nb, tb=tb),
        grid=(bsz // nb, nblk),
        in_specs=[
            pl.BlockSpec((nb, tb, D_MODEL), lambda b, i: (b, i, 0)),
            pl.BlockSpec((nb, tb, D_ATT), lambda b, i: (b, i, 0)),
            ssm_spec,
            _mod_spec(layer, row0, 2, nb), _mod_spec(layer, row0, 3, nb), _mod_spec(layer, row0, 4, nb),
            _mod_spec(layer, row0, 5, nb),
            row, row, row,
            pl.BlockSpec((None, D_MODEL, D_MODEL), lay, pipeline_mode=one),
            pl.BlockSpec((None, D_MODEL, D_FF), lay, pipeline_mode=one),
            pl.BlockSpec((None, D_MODEL, D_FF), lay, pipeline_mode=one),
            pl.BlockSpec((None, D_FF, D_MODEL), lay, pipeline_mode=one),
        ],
        out_specs=pl.BlockSpec((nb, tb, D_MODEL), lambda b, i: (b, i, 0)),
        out_shape=jax.ShapeDtypeStruct((bsz, t, D_MODEL), F32),
        compiler_params=pltpu.CompilerParams(
            dimension_semantics=("arbitrary", "arbitrary"), vmem_limit_bytes=VMEM_LIMIT),
        name="out_ffn",
    )(x, att, ssm, mods, mods, mods, mods, g_post_mix, g_pre_ffn, g_post_ffn,
      w_out, w_gate, w_up, w_down)


def kernel(x_prompt, x_sample, c_prompt, c_sample, cache_k, cache_v, cache_logf, state_ssm_re,
           state_ssm_im, w_ada, b_ada, g_pre_mix, g_post_mix, g_pre_ffn, g_post_ffn, w_in, b_forget,
           ssm_a_re, ssm_a_im, ssm_log_dt, ssm_b_re, ssm_b_im, ssm_c_re, ssm_c_im, ssm_d, w_glu,
           w_out, w_gate, w_up, w_down):
    depth = w_in.shape[0]
    bp, tp, _ = x_prompt.shape
    bs, ts, _ = x_sample.shape
    n_att = 3 * D_ATT + N_HEADS

    c_all = jnp.concatenate([c_prompt, c_sample, jnp.zeros((16 - bp - bs, D_MODEL), F32)], axis=0)
    mod = _ada_call(c_all, w_ada, b_ada)
    modr = mod.reshape(depth * 16 * 6, 1, D_MODEL)
    mod_s = mod[:, bp:bp + bs].reshape(depth, bs, 6, D_MODEL).transpose(0, 2, 1, 3)

    w_in_t = jnp.swapaxes(w_in, 1, 2)
    w_in_t = jnp.concatenate(
        [w_in_t[:, :n_att], jnp.zeros((depth, GATE_PAD - N_HEADS, D_MODEL), F32), w_in_t[:, n_att:]],
        axis=1).astype(BF16)
    bf_p = jnp.pad(b_forget, ((0, 0), (0, GATE_PAD - N_HEADS)))
    w_glu_b, w_out_b = w_glu.astype(BF16), w_out.astype(BF16)
    w_gate_b, w_up_b, w_down_b = w_gate.astype(BF16), w_up.astype(BF16), w_down.astype(BF16)
    toep, bpow, cpow, scm = _s5_tables(ssm_a_re, ssm_a_im, ssm_log_dt, jnp.swapaxes(ssm_b_re, 2, 3),
                                       jnp.swapaxes(ssm_b_im, 2, 3), ssm_c_re, ssm_c_im, ssm_d)
    cache_kt, cache_vt = jnp.swapaxes(cache_k, 3, 4), jnp.swapaxes(cache_v, 3, 4)
    h0 = jnp.concatenate([state_ssm_re, state_ssm_im], axis=-1).transpose(0, 2, 1, 3)

    n_chunk_p = tp // S5_CHUNK
    seg_s = ts // S5_CHUNK
    yp, ys = x_prompt, x_sample
    kvp = None
    lfp, ssp, kss, vss, lfs, sss = [], [], [], [], [], []
    for l in range(depth):
        qa, ka, va, kbuf, vbuf, lf, u, stats = _inproj_call(
            yp, modr, l, 0, g_pre_mix[l:l + 1], w_in_t, bf_p[l:l + 1], kvp, depth, 1, PROMPT_BLOCK)
        kvp = (kbuf, vbuf)
        lfp.append(lf)
        att = _attn_call(qa, ka, va, stats, PROMPT_BLOCK, ATTN_HEADS_PER_STEP)
        ssm, hfin = _s5_call(u, toep, bpow, cpow, scm, w_glu_b, None, l, n_chunk_p, n_chunk_p,
                             n_chunk_p, n_chunk_p - FIN_ROWS)
        ssp.append(hfin[:, :, FIN_ROWS - 1, :])
        yp = _out_ffn_call(yp, att, ssm, modr, l, 0, g_post_mix[l:l + 1], g_pre_ffn[l:l + 1],
                           g_post_ffn[l:l + 1], w_out_b, w_gate_b, w_up_b, w_down_b, 1, PROMPT_BLOCK)

        qa, ka, va, k_s, v_s, lf, u, _ = _inproj_call(
            ys, mod_s, l, bp, g_pre_mix[l:l + 1], w_in_t, bf_p[l:l + 1], None, None, bs, ts)
        kss.append(k_s)
        vss.append(v_s)
        lfs.append(lf.reshape(N_HEADS, bs, ts).transpose(1, 0, 2))
        att = _attn_sample_call(qa, ka, va, cache_kt, cache_vt, cache_logf, l)
        ssm, hfin = _s5_call(u, toep, bpow, cpow, scm, w_glu_b, h0, l, LANES, bs * seg_s, seg_s, 0)
        sss.append(hfin[0, :, seg_s - 1:bs * seg_s:seg_s, :].transpose(1, 0, 2))
        ys = _out_ffn_call(ys, att, ssm, mod_s, l, bp, g_post_mix[l:l + 1], g_pre_ffn[l:l + 1],
                           g_post_ffn[l:l + 1], w_out_b, w_gate_b, w_up_b, w_down_b, bs, ts)

    ssp, sss = jnp.stack(ssp), jnp.stack(sss)
    return (yp, ys, jnp.swapaxes(kvp[0], 3, 4), jnp.swapaxes(kvp[1], 3, 4), jnp.stack(lfp),
            ssp[..., :SSM_STATE], ssp[..., SSM_STATE:],
            jnp.stack(kss), jnp.stack(vss), jnp.stack(lfs),
            sss[..., :SSM_STATE], sss[..., SSM_STATE:])
```

```python
import functools
import math

import numpy as np
import jax
import jax.numpy as jnp
from jax import lax
from jax.experimental import pallas as pl
from jax.experimental.pallas import tpu as pltpu

F32 = jnp.float32
BF16 = jnp.bfloat16

D_MODEL = 1024
N_HEADS = 8
HEAD_DIM = 64
D_ATT = N_HEADS * HEAD_DIM
D_SSM = D_MODEL - D_ATT
SSM_GROUP = 16
N_GROUPS = D_SSM // SSM_GROUP
SSM_STATE = 64
D_FF = 2816
EPS = 1e-6

LANES = 128
S5_CHUNK = 16
S5_ROWS = S5_CHUNK * SSM_GROUP
GATE_PAD = LANES
W_IN_COLS = 3 * D_ATT + GATE_PAD + D_SSM
U_COL0 = 3 * D_ATT + GATE_PAD
N_U_SLABS = D_SSM // LANES
SCAN_PAD = 128
S5_PAIRS_PER_ITER = 2
VMEM_LIMIT = 56 * 1024 * 1024
MASK_VALUE = -1e30
LOG2E = math.log2(math.e)
DEN_LANE = HEAD_DIM
N_EXTRA = 6
EXTRA_SLOT = 8
SCORE_LIMIT = 80.0
NORM_SLACK = 1.02
EXP2_ZERO_BELOW = -150.0

FIN_ROWS = 32

PROMPT_BLOCK = 512
ATTN_HEADS_PER_STEP = N_HEADS
ATTN_HEADS_PER_LOOP = 4
ADA_COLS = 1536
TABLE_GROUPS = 8

NT_DIMS = (((1,), (1,)), ((), ()))


def _sigmoid(x):
    return 1.0 / (1.0 + jnp.exp(-x))


def _split3(x):
    hi = x.astype(BF16)
    r = x - hi.astype(F32)
    mid = r.astype(BF16)
    lo = (r - mid.astype(F32)).astype(BF16)
    return hi, mid, lo


def _split2(x):
    hi = x.astype(BF16)
    return hi, (x - hi.astype(F32)).astype(BF16)


def _dot(a, b):
    return jnp.dot(a, b, preferred_element_type=F32)


def _dot_nt(a, b):
    return lax.dot_general(a, b, NT_DIMS, preferred_element_type=F32)


def _rms(x, g):
    ms = jnp.mean(x * x, axis=-1, keepdims=True)
    return x * lax.rsqrt(ms + EPS) * g


def _per_row(ref, nb, tb):
    m = ref[...]
    if nb == 1:
        return m.reshape(1, D_MODEL)
    return jnp.broadcast_to(m[:, None, :], (nb, tb, D_MODEL)).reshape(nb * tb, D_MODEL)


def _ada_kernel(c_ref, w_ref, b_ref, o_ref):
    c = c_ref[...]
    a_hi, a_lo = _split2(c * _sigmoid(c))
    w_hi, w_lo = _split2(w_ref[0])
    o_ref[0] = _dot(a_hi, w_hi) + _dot(a_lo, w_hi) + _dot(a_hi, w_lo) + b_ref[0]


def _ada_call(c_all, w_ada, b_ada):
    depth = w_ada.shape[0]
    nb = ADA_COLS
    n_out = w_ada.shape[2]
    return pl.pallas_call(
        _ada_kernel,
        grid=(depth, n_out // nb),
        in_specs=[
            pl.BlockSpec((16, D_MODEL), lambda l, j: (0, 0)),
            pl.BlockSpec((1, D_MODEL, nb), lambda l, j: (l, 0, j)),
            pl.BlockSpec((1, 1, nb), lambda l, j: (l, 0, j)),
        ],
        out_specs=pl.BlockSpec((1, 16, nb), lambda l, j: (l, 0, j)),
        out_shape=jax.ShapeDtypeStruct((depth, 16, n_out), F32),
        compiler_params=pltpu.CompilerParams(
            dimension_semantics=("arbitrary", "arbitrary"), vmem_limit_bytes=VMEM_LIMIT),
        name="ada",
    )(c_all, w_ada, b_ada.reshape(depth, 1, n_out))


def _inproj_kernel(x_ref, sh_ref, sc_ref, g_ref, wt_ref, bf_ref, tri_ref, sel_ref, one_ref, eye_ref,
                   go_ref, qa_ref, ka_ref, va_ref, kc_ref, vc_ref, lf_ref, u_ref, st_ref, carry_ref,
                   *, nb, tb, kv_transposed):
    rows = nb * tb

    @pl.when(pl.program_id(1) == 0)
    def _():
        carry_ref[...] = jnp.zeros_like(carry_ref)

    x = x_ref[...].reshape(rows, D_MODEL)
    hm = _rms(x, g_ref[...]) * (1.0 + _per_row(sc_ref, nb, tb)) + _per_row(sh_ref, nb, tb)
    proj = _dot_nt(hm.astype(BF16), wt_ref[...])
    q = proj[:, 0:D_ATT] * (LOG2E * HEAD_DIM ** -0.5)
    k = proj[:, D_ATT:2 * D_ATT]
    v = proj[:, 2 * D_ATT:3 * D_ATT]
    gate = proj[:, 3 * D_ATT:U_COL0] + bf_ref[...]
    logf = jnp.minimum(gate, 0.0) - jnp.log1p(jnp.exp(-jnp.abs(gate)))
    lane = lax.broadcasted_iota(jnp.int32, (rows, LANES), 1)

    def pack3(val):
        hi, mid, lo = (part.astype(F32) for part in _split3(val))
        packed = jnp.where(lane < 2 * N_HEADS, pltpu.roll(mid, N_HEADS, axis=1),
                           jnp.where(lane < 3 * N_HEADS, pltpu.roll(lo, 2 * N_HEADS, axis=1), 0.0))
        return jnp.where(lane < N_HEADS, hi, packed).astype(BF16)

    lf_pack = pack3(logf)
    cs = _dot(tri_ref[...], lf_pack)
    fcum = (cs + pltpu.roll(cs, LANES - N_HEADS, axis=1) + pltpu.roll(cs, LANES - 2 * N_HEADS, axis=1)
            + carry_ref[...])
    carry_ref[...] = fcum[rows - 1:rows, :]

    extra = _dot(pack3(fcum * LOG2E), sel_ref[...]) + one_ref[...]
    in_extra = (lane >= HEAD_DIM) & (lane < HEAD_DIM + N_EXTRA)
    qn2 = _dot((q * q).astype(BF16), go_ref[...])
    kn2 = _dot((k * k).astype(BF16), go_ref[...])
    st_ref[0, 0] = jnp.concatenate(
        [jnp.max(qn2, axis=0, keepdims=True), jnp.max(kn2, axis=0, keepdims=True),
         fcum[0:1, :], fcum[rows - 1:rows, :], jnp.zeros((4, LANES), F32)], axis=0)
    lo_half = lane < HEAD_DIM
    den_one = jnp.where(lane == DEN_LANE, 1.0, 0.0)
    if kv_transposed:
        kt, vt = k.T, v.T
    for h in range(N_HEADS):
        pair = slice(LANES * (h // 2), LANES * (h // 2) + LANES)
        q2, k2, v2 = q[:, pair], k[:, pair], v[:, pair]
        if h % 2:
            q2 = pltpu.roll(q2, HEAD_DIM, axis=1)
            k2 = pltpu.roll(k2, HEAD_DIM, axis=1)
            v2 = pltpu.roll(v2, HEAD_DIM, axis=1)
        per_stream = lambda val: val.reshape(nb, tb, val.shape[-1])
        to_extra = lambda side: jnp.where(in_extra, pltpu.roll(
            extra[:, LANES * side:LANES * (side + 1)], (HEAD_DIM - EXTRA_SLOT * h) % LANES, axis=1), 0.0)
        qa_ref[:, h] = per_stream(jnp.where(lo_half, q2, to_extra(0)).astype(BF16))
        ka_ref[:, h] = per_stream(jnp.where(lo_half, k2, to_extra(1)).astype(BF16))
        va_ref[:, h] = per_stream(jnp.where(lo_half, v2, den_one).astype(BF16))
        if kv_transposed:
            kc_ref[h] = kt[HEAD_DIM * h:HEAD_DIM * (h + 1), :]
            vc_ref[h] = vt[HEAD_DIM * h:HEAD_DIM * (h + 1), :]
        else:
            kc_ref[:, h] = per_stream(k2[:, 0:HEAD_DIM])
            vc_ref[:, h] = per_stream(v2[:, 0:HEAD_DIM])
    lf_t = _dot_nt(eye_ref[...], lf_pack)
    lf_ref[0] = lf_t[0:N_HEADS] + lf_t[N_HEADS:2 * N_HEADS] + lf_t[2 * N_HEADS:3 * N_HEADS]
    for j in range(N_U_SLABS):
        u_ref[0, j] = proj[:, U_COL0 + LANES * j:U_COL0 + LANES * (j + 1)]


def _inproj_consts(nb, tb):
    tri = np.kron(np.eye(nb, dtype=np.float32), np.tril(np.ones((tb, tb), np.float32)))
    sel = np.zeros((LANES, 2 * LANES), np.float32)
    one = np.zeros((1, 2 * LANES), np.float32)
    group_ones = np.zeros((D_ATT, LANES), np.float32)
    for h in range(N_HEADS):
        group_ones[HEAD_DIM * h:HEAD_DIM * (h + 1), h] = 1.0
        for p in range(3):
            sel[p * N_HEADS + h, EXTRA_SLOT * h + p] = 1.0
            one[0, EXTRA_SLOT * h + 3 + p] = 1.0
            one[0, LANES + EXTRA_SLOT * h + p] = 1.0
            sel[p * N_HEADS + h, LANES + EXTRA_SLOT * h + 3 + p] = -1.0
    eye = np.eye(4 * N_HEADS, LANES, dtype=np.float32)
    eye[3 * N_HEADS:] = 0.0
    return (jnp.asarray(tri, BF16), jnp.asarray(sel, BF16), jnp.asarray(one, F32), jnp.asarray(eye, BF16),
            jnp.asarray(group_ones, BF16))


def _mod_spec(layer, row0, chunk, nb=1):
    if nb == 1:
        return pl.BlockSpec((1, 1, D_MODEL), lambda b, i: ((layer * 16 + row0 + b) * 6 + chunk, 0, 0))
    return pl.BlockSpec((None, None, nb, D_MODEL), lambda b, i: (layer, chunk, 0, 0))


def _inproj_call(x, mods, layer, row0, g_pre, w_in_t, b_forget, kv_bufs, kv_depth, nb, tb):
    bsz, t, _ = x.shape
    nblk = t // tb
    rows = nb * tb
    assert nb == 1 or (nblk == 1 and kv_depth is None)
    tri, sel, one, eye, group_ones = _inproj_consts(nb, tb)
    if nb > 1:
        u_shape = (1, N_U_SLABS, bsz * t, LANES)
        u_spec = pl.BlockSpec((1, N_U_SLABS, rows, LANES), lambda b, i: (0, 0, b, 0))
        lf_shape = (1, N_HEADS, bsz * t)
        lf_spec = pl.BlockSpec((1, N_HEADS, rows), lambda b, i: (0, 0, b))
    else:
        u_shape = (bsz, N_U_SLABS, t, LANES)
        u_spec = pl.BlockSpec((1, N_U_SLABS, tb, LANES), lambda b, i: (b, 0, i, 0))
        lf_shape = (bsz, N_HEADS, t)
        lf_spec = pl.BlockSpec((1, N_HEADS, tb), lambda b, i: (b, 0, i))
    const = lambda b, i: (0, 0)
    in_specs = [
        pl.BlockSpec((nb, tb, D_MODEL), lambda b, i: (b, i, 0)),
        _mod_spec(layer, row0, 0, nb), _mod_spec(layer, row0, 1, nb),
        pl.BlockSpec((1, D_MODEL), const),
        pl.BlockSpec((None, W_IN_COLS, D_MODEL), lambda b, i: (layer, 0, 0)),
        pl.BlockSpec((1, GATE_PAD), const),
        pl.BlockSpec((rows, rows), const),
        pl.BlockSpec((LANES, 2 * LANES), const),
        pl.BlockSpec((1, 2 * LANES), const),
        pl.BlockSpec((4 * N_HEADS, LANES), const),
        pl.BlockSpec((D_ATT, LANES), const),
    ]
    args = [x, mods, mods, g_pre, w_in_t, b_forget, tri, sel, one, eye, group_ones]
    n_main = len(args)
    head_blk = pl.BlockSpec((nb, N_HEADS, tb, LANES), lambda b, i: (b, 0, i, 0))
    aliases = {}
    if kv_depth is None:
        kv_shape = jax.ShapeDtypeStruct((bsz, N_HEADS, t, HEAD_DIM), F32)
        kv_spec = pl.BlockSpec((nb, N_HEADS, tb, HEAD_DIM), lambda b, i: (b, 0, i, 0))
    else:
        kv_shape = jax.ShapeDtypeStruct((kv_depth, bsz, N_HEADS, HEAD_DIM, t), F32)
        kv_spec = pl.BlockSpec((None, None, N_HEADS, HEAD_DIM, tb), lambda b, i: (layer, b, 0, 0, i))
        if kv_bufs is not None:
            in_specs += [pl.BlockSpec(memory_space=pl.ANY), pl.BlockSpec(memory_space=pl.ANY)]
            args += list(kv_bufs)
            aliases = {n_main: 3, n_main + 1: 4}
    out_shape = [
        jax.ShapeDtypeStruct((bsz, N_HEADS, t, LANES), BF16),
        jax.ShapeDtypeStruct((bsz, N_HEADS, t, LANES), BF16),
        jax.ShapeDtypeStruct((bsz, N_HEADS, t, LANES), BF16),
        kv_shape, kv_shape,
        jax.ShapeDtypeStruct(lf_shape, F32),
        jax.ShapeDtypeStruct(u_shape, F32),
        jax.ShapeDtypeStruct((bsz // nb, nblk, 8, LANES), F32),
    ]
    out_specs = [
        head_blk, head_blk, head_blk,
        kv_spec, kv_spec,
        lf_spec,
        u_spec,
        pl.BlockSpec((1, 1, 8, LANES), lambda b, i: (b, i, 0, 0)),
    ]
    n_args = len(args)

    def body(*refs):
        _inproj_kernel(*refs[:n_main], *refs[n_args:], nb=nb, tb=tb, kv_transposed=kv_depth is not None)

    return pl.pallas_call(
        body,
        grid=(bsz // nb, nblk),
        in_specs=in_specs,
        out_specs=out_specs,
        out_shape=out_shape,
        scratch_shapes=[pltpu.VMEM((1, LANES), F32)],
        input_output_aliases=aliases,
        compiler_params=pltpu.CompilerParams(
            dimension_semantics=("arbitrary", "arbitrary"), vmem_limit_bytes=VMEM_LIMIT),
        name="inproj",
    )(*args)


def _finish_heads(accs, lane):
    outs = []
    for acc in accs:
        den = jnp.sum(jnp.where(lane == DEN_LANE, acc, 0.0), axis=1, keepdims=True)
        outs.append(acc * (1.0 / den))
    return jnp.where(lane < HEAD_DIM, outs[0], pltpu.roll(outs[1], HEAD_DIM, axis=1)).astype(BF16)


def _attn_kernel(small_ref, first_ref, qa_ref, ka_ref, va_ref, o_ref, *, tq, hps, lps):
    i = pl.program_id(2)
    row = lax.broadcasted_iota(jnp.int32, (tq, tq), 0)
    col = lax.broadcasted_iota(jnp.int32, (tq, tq), 1)
    causal = col <= row
    lane = lax.broadcasted_iota(jnp.int32, (tq, LANES), 1)
    qs = [qa_ref[0, hh] for hh in range(hps)]

    def score(j, hh, masked):
        start = pl.multiple_of(j * tq, tq)
        s = _dot_nt(qs[hh], ka_ref[0, hh, pl.ds(start, tq), :])
        return jnp.where(causal, s, MASK_VALUE) if masked else s

    def values(j, hh):
        return va_ref[0, hh, pl.ds(pl.multiple_of(j * tq, tq), tq), :]

    def store(accs):
        for pr in range(hps // 2):
            o_ref[0, :, LANES * pr:LANES * (pr + 1)] = _finish_heads(accs[2 * pr:2 * pr + 2], lane)

    def tile_plain(j, accs, heads):
        return tuple(acc + _dot(jnp.exp2(score(j, hh, False)).astype(BF16), values(j, hh))
                     for acc, hh in zip(accs, heads))

    def diagonal_plain(accs):
        half = tq // 2
        start = pl.multiple_of(i * tq, tq)
        upper = (lax.broadcasted_iota(jnp.int32, (half, half), 1) <=
                 lax.broadcasted_iota(jnp.int32, (half, half), 0))
        lower = (lax.broadcasted_iota(jnp.int32, (half, tq), 1) <=
                 lax.broadcasted_iota(jnp.int32, (half, tq), 0) + half)
        new = []
        for hh in range(hps):
            k_t = ka_ref[0, hh, pl.ds(start, tq), :]
            v_t = va_ref[0, hh, pl.ds(start, tq), :]
            s_top = jnp.where(upper, _dot_nt(qs[hh][0:half], k_t[0:half]), MASK_VALUE)
            s_bot = jnp.where(lower, _dot_nt(qs[hh][half:tq], k_t), MASK_VALUE)
            o_top = _dot(jnp.exp2(s_top).astype(BF16), v_t[0:half])
            o_bot = _dot(jnp.exp2(s_bot).astype(BF16), v_t)
            new.append(accs[hh] + jnp.concatenate([o_top, o_bot], axis=0))
        return new

    def tile_shifted(j, carry, masked):
        new = []
        for hh in range(hps):
            m, acc = carry[hh]
            s = score(j, hh, masked)
            m_new = jnp.maximum(m, jnp.max(s, axis=1, keepdims=True))
            p = jnp.exp2(s - m_new)
            new.append((m_new, jnp.exp2(m - m_new) * acc + _dot(p.astype(BF16), values(j, hh))))
        return tuple(new)

    small = small_ref[pl.program_id(0), pl.program_id(1)] != 0
    zero = jnp.zeros((tq, LANES), F32)

    @pl.when(small)
    def _():
        accs = []
        for sub in range(hps // lps):
            heads = tuple(range(sub * lps, (sub + 1) * lps))
            first = first_ref[pl.program_id(0), pl.program_id(1) * (hps // lps) + sub, i]
            accs += lax.fori_loop(
                0, i - first,
                lambda it, a, first=first, heads=heads: tile_plain(first + it, a, heads), (zero,) * lps)
        store(diagonal_plain(accs))

    @pl.when(jnp.logical_not(small))
    def _():
        init = ((jnp.full((tq, 1), MASK_VALUE, F32), zero),) * hps
        carry = lax.fori_loop(0, i, functools.partial(tile_shifted, masked=False), init)
        store([c[1] for c in tile_shifted(i, carry, True)])


def _attn_call(qa, ka, va, stats, tq, hps, lps):
    bsz, _, t, _ = qa.shape
    nt, ng, nl = t // tq, N_HEADS // hps, N_HEADS // lps
    assert stats.shape[1] == nt
    bound = jnp.sqrt(jnp.max(stats[:, :, 0, :N_HEADS], axis=1) *
                     jnp.max(stats[:, :, 1, :N_HEADS], axis=1)) * NORM_SLACK
    small = jnp.max(bound.reshape(bsz, ng, hps), axis=-1) < SCORE_LIMIT
    f_first, f_last = stats[:, :, 2, :N_HEADS] * LOG2E, stats[:, :, 3, :N_HEADS] * LOG2E
    top = bound[:, None, None, :] + f_first[:, :, None, :] - f_last[:, None, :, :]
    needed = jnp.max((top >= EXP2_ZERO_BELOW).astype(jnp.int32).reshape(bsz, nt, nt, nl, lps), axis=-1)
    first = jnp.where(jnp.repeat(small, nl // ng, axis=1)[:, :, None],
                      jnp.sum(1 - needed, axis=2).transpose(0, 2, 1), 0)
    whole = pl.BlockSpec((1, hps, t, LANES), lambda b, p, i, *_: (b, p, 0, 0))
    return pl.pallas_call(
        functools.partial(_attn_kernel, tq=tq, hps=hps, lps=lps),
        grid_spec=pltpu.PrefetchScalarGridSpec(
            num_scalar_prefetch=2,
            grid=(bsz, ng, nt),
            in_specs=[pl.BlockSpec((1, hps, tq, LANES), lambda b, p, i, *_: (b, p, i, 0)), whole, whole],
            out_specs=pl.BlockSpec((1, tq, HEAD_DIM * hps), lambda b, p, i, *_: (b, i, p)),
        ),
        out_shape=jax.ShapeDtypeStruct((bsz, t, D_ATT), BF16),
        compiler_params=pltpu.CompilerParams(
            dimension_semantics=("arbitrary", "arbitrary", "arbitrary"),
            vmem_limit_bytes=VMEM_LIMIT),
        name="fox_attn",
    )(small.astype(jnp.int32), first.astype(jnp.int32), qa, ka, va)


def _attn_sample_kernel(qa_ref, ka_ref, va_ref, ckt_ref, cvt_ref, clf_ref, o_ref, *, t, n_past):
    lane_p = lax.broadcasted_iota(jnp.int32, (N_HEADS, n_past), 1)
    c = clf_ref[...]
    total = jnp.sum(c, axis=1, keepdims=True)
    d = 1
    while d < n_past:
        c = c + jnp.where(lane_p >= d, pltpu.roll(c, d, axis=1), 0.0)
        d *= 2
    f_past = (c - total) * LOG2E
    lane = lax.broadcasted_iota(jnp.int32, (t, LANES), 1)
    causal = (lax.broadcasted_iota(jnp.int32, (t, t), 1) <= lax.broadcasted_iota(jnp.int32, (t, t), 0))
    den_rows = jnp.where(lax.broadcasted_iota(jnp.int32, (LANES - HEAD_DIM, n_past), 0) == 0, 1.0, 0.0)
    accs = []
    for h in range(N_HEADS):
        q = qa_ref[0, h]
        fq = jnp.sum(jnp.where((lane >= HEAD_DIM) & (lane < HEAD_DIM + 3), q.astype(F32), 0.0),
                     axis=1, keepdims=True)
        s_past = _dot(q[:, 0:HEAD_DIM], ckt_ref[h].astype(BF16)) + fq - f_past[h:h + 1, :]
        s_new = jnp.where(causal, _dot_nt(q, ka_ref[0, h]), MASK_VALUE)
        m = jnp.maximum(jnp.max(s_past, axis=1, keepdims=True), jnp.max(s_new, axis=1, keepdims=True))
        p_past = jnp.exp2(s_past - m).astype(BF16)
        p_new = jnp.exp2(s_new - m).astype(BF16)
        cvt_aug = jnp.concatenate([cvt_ref[h], den_rows], axis=0).astype(BF16)
        accs.append(_dot_nt(p_past, cvt_aug) + _dot(p_new, va_ref[0, h]))
    for pr in range(N_HEADS // 2):
        o_ref[0, :, LANES * pr:LANES * (pr + 1)] = _finish_heads(accs[2 * pr:2 * pr + 2], lane)


def _attn_sample_call(qa, ka, va, cache_kt, cache_vt, cache_logf, layer):
    bsz, _, t, _ = qa.shape
    n_past = cache_kt.shape[4]
    cache_spec = pl.BlockSpec((None, None, N_HEADS, HEAD_DIM, n_past), lambda b: (layer, b, 0, 0, 0))
    new_spec = pl.BlockSpec((1, N_HEADS, t, LANES), lambda b: (b, 0, 0, 0))
    return pl.pallas_call(
        functools.partial(_attn_sample_kernel, t=t, n_past=n_past),
        grid=(bsz,),
        in_specs=[
            new_spec, new_spec, new_spec, cache_spec, cache_spec,
            pl.BlockSpec((None, None, N_HEADS, n_past), lambda b: (layer, b, 0, 0)),
        ],
        out_specs=pl.BlockSpec((1, t, D_ATT), lambda b: (b, 0, 0)),
        out_shape=jax.ShapeDtypeStruct((bsz, t, D_ATT), BF16),
        compiler_params=pltpu.CompilerParams(
            dimension_semantics=("arbitrary",), vmem_limit_bytes=VMEM_LIMIT),
        name="fox_attn_sample",
    )(qa, ka, va, cache_kt, cache_vt, cache_logf)


def _s5_table_kernel(ca_ref, cb_ref, ba_ref, bb_ref, prr_ref, pii_ref, dv_ref,
                     toep_ref, bpow_ref, cpow_ref, tt_scr, *, gb):
    row = lax.broadcasted_iota(jnp.int32, (SSM_GROUP, S5_ROWS), 0)
    lane = lax.broadcasted_iota(jnp.int32, (SSM_GROUP, S5_ROWS), 1)
    for gi in range(gb):
        ca, cb, ba, bb = ca_ref[gi], cb_ref[gi], ba_ref[gi], bb_ref[gi]
        power = lambda a, b, n: a * prr_ref[gi, n:n + 1, :] + b * pii_ref[gi, n:n + 1, :]
        xs = [power(ca, cb, n) for n in range(S5_CHUNK + 1)]
        cpow_ref[gi] = jnp.concatenate(xs[1:], axis=0).astype(BF16)
        bpow_ref[gi] = jnp.concatenate(
            [power(ba, bb, S5_CHUNK - 1 - s) for s in range(S5_CHUNK)], axis=0).T.astype(BF16)
        x_hi, x_lo = _split2(jnp.concatenate(xs[:S5_CHUNK], axis=0))
        b_hi, b_lo = _split2(ba)
        r0 = _dot_nt(b_hi, x_hi) + _dot_nt(b_lo, x_hi) + _dot_nt(b_hi, x_lo)
        r0 = r0 + jnp.where(lane == row, dv_ref[gi], 0.0)
        for s in range(S5_CHUNK):
            blk = r0
            if s:
                blk = jnp.where(lane >= SSM_GROUP * s, pltpu.roll(r0, SSM_GROUP * s, axis=1), 0.0)
            tt_scr[SSM_GROUP * s:SSM_GROUP * (s + 1), :] = blk
        toep_ref[gi] = tt_scr[...].T.astype(BF16)


def _s5_tables(a_re, a_im, log_dt, b_re_t, b_im_t, c_re, c_im, d_skip):
    depth = a_re.shape[0]
    dt = jnp.exp(log_dt)[..., None]
    x, y = a_re * dt, a_im * dt
    ex, cy, sy = jnp.exp(x), jnp.cos(y), jnp.sin(y)
    ar, ai = ex * cy, ex * sy
    sh = jnp.sin(0.5 * y)
    nr, ni = jnp.expm1(x) * cy - 2.0 * sh * sh, ai
    den = a_re * a_re + a_im * a_im
    fr = ((nr * a_re + ni * a_im) / den)[:, :, None, :]
    fi = ((ni * a_re - nr * a_im) / den)[:, :, None, :]
    bbr, bbi = fr * b_re_t - fi * b_im_t, fr * b_im_t + fi * b_re_t
    pr, pi = [jnp.ones_like(ar)], [jnp.zeros_like(ar)]
    for _ in range(S5_CHUNK):
        pr, pi = pr + [pr[-1] * ar - pi[-1] * ai], pi + [pr[-1] * ai + pi[-1] * ar]
    mr, mi = pr[S5_CHUNK], pi[S5_CHUNK]
    pr, pi = jnp.stack(pr, axis=2), jnp.stack(pi, axis=2)
    cat = lambda a, b: jnp.concatenate([a, b], axis=-1)
    ins = [cat(c_re, -c_im), cat(-c_im, -c_re), cat(bbr, bbi), cat(-bbi, bbr), cat(pr, pr), cat(pi, pi),
           jnp.pad(d_skip.reshape(depth, N_GROUPS, 1, SSM_GROUP), ((0, 0), (0, 0), (0, 0), (0, S5_ROWS - SSM_GROUP)))]
    gb = TABLE_GROUPS
    spec = lambda rows, cols: pl.BlockSpec((None, gb, rows, cols), lambda l, j: (l, j, 0, 0))
    tab = lambda rows, cols: jax.ShapeDtypeStruct((depth, N_GROUPS, rows, cols), BF16)
    toep, bpow, cpow = pl.pallas_call(
        functools.partial(_s5_table_kernel, gb=gb),
        grid=(depth, N_GROUPS // gb),
        in_specs=[spec(SSM_GROUP, LANES)] * 4 + [spec(S5_CHUNK + 1, LANES)] * 2 + [spec(1, S5_ROWS)],
        out_specs=[spec(S5_ROWS, S5_ROWS), spec(LANES, S5_ROWS), spec(S5_ROWS, LANES)],
        out_shape=[tab(S5_ROWS, S5_ROWS), tab(LANES, S5_ROWS), tab(S5_ROWS, LANES)],
        scratch_shapes=[pltpu.VMEM((S5_ROWS, S5_ROWS), F32)],
        compiler_params=pltpu.CompilerParams(
            dimension_semantics=("arbitrary", "arbitrary"), vmem_limit_bytes=VMEM_LIMIT),
        name="s5_tables",
    )(*ins)
    rows_r, rows_i = [], []
    for _ in range(8):
        rows_r.append(cat(mr, mr))
        rows_i.append(cat(-mi, mi))
        mr, mi = mr * mr - mi * mi, 2.0 * mr * mi
    scm = jnp.stack(rows_r + rows_i, axis=2)
    return toep, bpow, cpow, scm


def _gelu_tanh(y):
    return 0.5 * y * (1.0 + jnp.tanh(math.sqrt(2.0 / math.pi) * (y + 0.044715 * (y * y * y))))


def _s5_kernel(*refs, n, n_real, seglen, fin_row0, use_h0):
    if use_h0:
        u_ref, toep_ref, bpow_ref, cpow_ref, scm_ref, wglu_ref, h0_ref = refs[:7]
        o_ref, hfin_ref, z_scr, zt_scr, h_scr, h0_scr = refs[7:]
    else:
        u_ref, toep_ref, bpow_ref, cpow_ref, scm_ref, wglu_ref = refs[:6]
        o_ref, hfin_ref, z_scr, zt_scr, h_scr = refs[6:]
    gps = LANES // SSM_GROUP
    for pp in range(S5_PAIRS_PER_ITER):
        for part in range(2):
            h_scr[pp, part, 0:SCAN_PAD, :] = jnp.zeros((SCAN_PAD, LANES), F32)
    if use_h0:
        h0_scr[...] = jnp.zeros(h0_scr.shape, F32)

    for k in range(S5_CHUNK):
        for j in range(N_U_SLABS):
            vv = u_ref[0, j, pl.ds(k, n_real, stride=S5_CHUNK), :]
            if n_real < n:
                vv = jnp.concatenate([vv, jnp.zeros((n - n_real, LANES), F32)], axis=0)
            z_scr[gps * j:gps * (j + 1), SSM_GROUP * k:SSM_GROUP * (k + 1), :] = (
                vv.T.reshape(gps, SSM_GROUP, n).astype(BF16))

    chunk_in_seg = lax.broadcasted_iota(jnp.int32, (n, LANES), 0) % seglen
    lo = lax.broadcasted_iota(jnp.int32, (n, LANES), 1) < SSM_STATE
    lo_row = lax.broadcasted_iota(jnp.int32, (1, LANES), 1) < SSM_STATE
    lo_fin = lax.broadcasted_iota(jnp.int32, (FIN_ROWS, LANES), 1) < SSM_STATE
    swap = lambda a: pltpu.roll(a, SSM_STATE, axis=1)

    def pair_step(it, _):
        for pp in range(S5_PAIRS_PER_ITER):
            gs = [2 * (it * S5_PAIRS_PER_ITER + pp), 2 * (it * S5_PAIRS_PER_ITER + pp) + 1]
            ucts = [z_scr[g] for g in gs]
            yts = [_dot(toep_ref[g], u) for g, u in zip(gs, ucts)]
            hs = [_dot(bpow_ref[g], u).T for g, u in zip(gs, ucts)]
            h0s = [0.0, 0.0]
            if use_h0:
                for idx, g in enumerate(gs):
                    h0_scr[idx, pl.ds(0, h0_ref.shape[1], stride=seglen), :] = h0_ref[g]
                    h0s[idx] = h0_scr[idx]
                    hs[idx] = hs[idx] + scm_ref[g, 0:1, :] * h0s[idx] + scm_ref[g, 8:9, :] * swap(h0s[idx])
            hre = jnp.where(lo, hs[0], swap(hs[1]))
            him = jnp.where(lo, swap(hs[0]), hs[1])

            def shifted(by):
                h_scr[pp, 0, SCAN_PAD:SCAN_PAD + n, :] = hre
                h_scr[pp, 1, SCAN_PAD:SCAN_PAD + n, :] = him
                return (h_scr[pp, 0, pl.ds(SCAN_PAD - by, n), :], h_scr[pp, 1, pl.ds(SCAN_PAD - by, n), :])

            d, j = 1, 0
            while d < seglen:
                sre, sim = shifted(d)
                if seglen < n:
                    sre = jnp.where(chunk_in_seg >= d, sre, 0.0)
                    sim = jnp.where(chunk_in_seg >= d, sim, 0.0)
                ar = jnp.where(lo_row, scm_ref[gs[0], j:j + 1, :], scm_ref[gs[1], j:j + 1, :])
                ai = jnp.where(lo_row, -scm_ref[gs[0], 8 + j:9 + j, :], scm_ref[gs[1], 8 + j:9 + j, :])
                hre, him = hre + ar * sre - ai * sim, him + ar * sim + ai * sre
                d, j = 2 * d, j + 1
            pre, pim = shifted(1)
            hps = [jnp.where(lo, pre, swap(pim)), jnp.where(lo, swap(pre), pim)]
            fre, fim = hre[fin_row0:fin_row0 + FIN_ROWS, :], him[fin_row0:fin_row0 + FIN_ROWS, :]
            fins = [jnp.where(lo_fin, fre, swap(fim)), jnp.where(lo_fin, swap(fre), fim)]
            for idx, g in enumerate(gs):
                hp = hps[idx]
                if seglen < n:
                    hp = jnp.where(chunk_in_seg >= 1, hp, h0s[idx])
                zt_scr[g] = _gelu_tanh(yts[idx] + _dot_nt(cpow_ref[g], hp.astype(BF16)))
                hfin_ref[0, g] = fins[idx]
        return 0

    lax.fori_loop(0, N_GROUPS // (2 * S5_PAIRS_PER_ITER), pair_step, 0)

    for k in range(S5_CHUNK):
        slabs = []
        for j in range(N_U_SLABS):
            w = zt_scr[gps * j:gps * (j + 1), SSM_GROUP * k:SSM_GROUP * (k + 1), :]
            slabs.append(w.reshape(LANES, n).T)
        zk = jnp.concatenate(slabs, axis=1)
        out = zk * _sigmoid(_dot(zk.astype(BF16), wglu_ref[...]))
        for j in range(N_U_SLABS):
            o_ref[0, j, pl.ds(k, n_real, stride=S5_CHUNK), :] = out[0:n_real, LANES * j:LANES * (j + 1)]


def _s5_call(u, toep, bpow, cpow, scm, wglu, h0, layer, n, n_real, seglen, fin_row0):
    nb, _, tt, _ = u.shape
    use_h0 = h0 is not None
    one = pl.Buffered(1)
    tab = lambda rows, cols: pl.BlockSpec((None, N_GROUPS, rows, cols), lambda b: (layer, 0, 0, 0),
                                          pipeline_mode=one)
    in_specs = [
        pl.BlockSpec((1, N_U_SLABS, tt, LANES), lambda b: (b, 0, 0, 0)),
        tab(S5_ROWS, S5_ROWS), tab(LANES, S5_ROWS), tab(S5_ROWS, LANES), tab(16, LANES),
        pl.BlockSpec((None, D_SSM, D_SSM), lambda b: (layer, 0, 0), pipeline_mode=one),
    ]
    args = [u, toep, bpow, cpow, scm, wglu]
    scratch = [
        pltpu.VMEM((N_GROUPS, S5_ROWS, n), BF16),
        pltpu.VMEM((N_GROUPS, S5_ROWS, n), F32),
        pltpu.VMEM((S5_PAIRS_PER_ITER, 2, SCAN_PAD + n, LANES), F32),
    ]
    if use_h0:
        in_specs.append(tab(h0.shape[2], LANES))
        args.append(h0)
        scratch.append(pltpu.VMEM((2, n, LANES), F32))
    return pl.pallas_call(
        functools.partial(_s5_kernel, n=n, n_real=n_real, seglen=seglen, fin_row0=fin_row0,
                          use_h0=use_h0),
        grid=(nb,),
        in_specs=in_specs,
        out_specs=[
            pl.BlockSpec((1, N_U_SLABS, tt, LANES), lambda b: (b, 0, 0, 0)),
            pl.BlockSpec((1, N_GROUPS, FIN_ROWS, LANES), lambda b: (b, 0, 0, 0)),
        ],
        out_shape=[
            jax.ShapeDtypeStruct((nb, N_U_SLABS, tt, LANES), F32),
            jax.ShapeDtypeStruct((nb, N_GROUPS, FIN_ROWS, LANES), F32),
        ],
        scratch_shapes=scratch,
        compiler_params=pltpu.CompilerParams(
            dimension_semantics=("arbitrary",), vmem_limit_bytes=VMEM_LIMIT),
        name="s5_mix",
    )(*args)


def _out_ffn_kernel(x_ref, att_ref, ssm_ref, g1_ref, sh2_ref, sc2_ref, g2_ref,
                    gpm_ref, gpf_ref, gpo_ref, wo_ref, wg_ref, wu_ref, wd_ref, o_ref, *, nb, tb):
    rows = nb * tb
    mix = jnp.concatenate(
        [att_ref[...].reshape(rows, D_ATT)] + [ssm_ref[0, j].astype(BF16) for j in range(N_U_SLABS)],
        axis=1)
    o = _dot(mix, wo_ref[...])
    x1 = x_ref[...].reshape(rows, D_MODEL) + _per_row(g1_ref, nb, tb) * _rms(o, gpm_ref[...])
    hf = (_rms(x1, gpf_ref[...]) * (1.0 + _per_row(sc2_ref, nb, tb)) + _per_row(sh2_ref, nb, tb)).astype(BF16)
    gt = _dot(hf, wg_ref[...])
    up = _dot(hf, wu_ref[...])
    hid = (gt * _sigmoid(gt) * up).astype(BF16)
    f = _dot(hid, wd_ref[...])
    o_ref[...] = (x1 + _per_row(g2_ref, nb, tb) * _rms(f, gpo_ref[...])).reshape(nb, tb, D_MODEL)


def _out_ffn_call(x, att, ssm, mods, layer, row0, g_post_mix, g_pre_ffn, g_post_ffn,
                  w_out, w_gate, w_up, w_down, nb, tb):
    bsz, t, _ = x.shape
    nblk = t // tb
    one = pl.Buffered(1)
    const = lambda b, i: (0, 0)
    lay = lambda b, i: (layer, 0, 0)
    if nb > 1:
        ssm_spec = pl.BlockSpec((1, N_U_SLABS, nb * tb, LANES), lambda b, i: (0, 0, b, 0))
    else:
        ssm_spec = pl.BlockSpec((1, N_U_SLABS, tb, LANES), lambda b, i: (b, 0, i, 0))
    row = pl.BlockSpec((1, D_MODEL), const)
    return pl.pallas_call(
        functools.partial(_out_ffn_kernel, nb=nb, tb=tb),
        grid=(bsz // nb, nblk),
        in_specs=[
            pl.BlockSpec((nb, tb, D_MODEL), lambda b, i: (b, i, 0)),
            pl.BlockSpec((nb, tb, D_ATT), lambda b, i: (b, i, 0)),
            ssm_spec,
            _mod_spec(layer, row0, 2, nb), _mod_spec(layer, row0, 3, nb), _mod_spec(layer, row0, 4, nb),
            _mod_spec(layer, row0, 5, nb),
            row, row, row,
            pl.BlockSpec((None, D_MODEL, D_MODEL), lay, pipeline_mode=one),
            pl.BlockSpec((None, D_MODEL, D_FF), lay, pipeline_mode=one),
            pl.BlockSpec((None, D_MODEL, D_FF), lay, pipeline_mode=one),
            pl.BlockSpec((None, D_FF, D_MODEL), lay, pipeline_mode=one),
        ],
        out_specs=pl.BlockSpec((nb, tb, D_MODEL), lambda b, i: (b, i, 0)),
        out_shape=jax.ShapeDtypeStruct((bsz, t, D_MODEL), F32),
        compiler_params=pltpu.CompilerParams(
            dimension_semantics=("arbitrary", "arbitrary"), vmem_limit_bytes=VMEM_LIMIT),
        name="out_ffn",
    )(x, att, ssm, mods, mods, mods, mods, g_post_mix, g_pre_ffn, g_post_ffn,
      w_out, w_gate, w_up, w_down)


def kernel(x_prompt, x_sample, c_prompt, c_sample, cache_k, cache_v, cache_logf, state_ssm_re,
           state_ssm_im, w_ada, b_ada, g_pre_mix, g_post_mix, g_pre_ffn, g_post_ffn, w_in, b_forget,
           ssm_a_re, ssm_a_im, ssm_log_dt, ssm_b_re, ssm_b_im, ssm_c_re, ssm_c_im, ssm_d, w_glu,
           w_out, w_gate, w_up, w_down):
    depth = w_in.shape[0]
    bp, tp, _ = x_prompt.shape
    bs, ts, _ = x_sample.shape
    n_att = 3 * D_ATT + N_HEADS

    c_all = jnp.concatenate([c_prompt, c_sample, jnp.zeros((16 - bp - bs, D_MODEL), F32)], axis=0)
    mod = _ada_call(c_all, w_ada, b_ada)
    modr = mod.reshape(depth * 16 * 6, 1, D_MODEL)
    mod_s = mod[:, bp:bp + bs].reshape(depth, bs, 6, D_MODEL).transpose(0, 2, 1, 3)

    w_in_t = jnp.swapaxes(w_in, 1, 2)
    w_in_t = jnp.concatenate(
        [w_in_t[:, :n_att], jnp.zeros((depth, GATE_PAD - N_HEADS, D_MODEL), F32), w_in_t[:, n_att:]],
        axis=1).astype(BF16)
    bf_p = jnp.pad(b_forget, ((0, 0), (0, GATE_PAD - N_HEADS)))
    w_glu_b, w_out_b = w_glu.astype(BF16), w_out.astype(BF16)
    w_gate_b, w_up_b, w_down_b = w_gate.astype(BF16), w_up.astype(BF16), w_down.astype(BF16)
    toep, bpow, cpow, scm = _s5_tables(ssm_a_re, ssm_a_im, ssm_log_dt, jnp.swapaxes(ssm_b_re, 2, 3),
                                       jnp.swapaxes(ssm_b_im, 2, 3), ssm_c_re, ssm_c_im, ssm_d)
    cache_kt, cache_vt = jnp.swapaxes(cache_k, 3, 4), jnp.swapaxes(cache_v, 3, 4)
    h0 = jnp.concatenate([state_ssm_re, state_ssm_im], axis=-1).transpose(0, 2, 1, 3)

    n_chunk_p = tp // S5_CHUNK
    seg_s = ts // S5_CHUNK
    yp, ys = x_prompt, x_sample
    kvp = None
    lfp, ssp, kss, vss, lfs, sss = [], [], [], [], [], []
    for l in range(depth):
        qa, ka, va, kbuf, vbuf, lf, u, stats = _inproj_call(
            yp, modr, l, 0, g_pre_mix[l:l + 1], w_in_t, bf_p[l:l + 1], kvp, depth, 1, PROMPT_BLOCK)
        kvp = (kbuf, vbuf)
        lfp.append(lf)
        att = _attn_call(qa, ka, va, stats, PROMPT_BLOCK, ATTN_HEADS_PER_STEP, ATTN_HEADS_PER_LOOP)
        ssm, hfin = _s5_call(u, toep, bpow, cpow, scm, w_glu_b, None, l, n_chunk_p, n_chunk_p,
                             n_chunk_p, n_chunk_p - FIN_ROWS)
        ssp.append(hfin[:, :, FIN_ROWS - 1, :])
        yp = _out_ffn_call(yp, att, ssm, modr, l, 0, g_post_mix[l:l + 1], g_pre_ffn[l:l + 1],
                           g_post_ffn[l:l + 1], w_out_b, w_gate_b, w_up_b, w_down_b, 1, PROMPT_BLOCK)

        qa, ka, va, k_s, v_s, lf, u, _ = _inproj_call(
            ys, mod_s, l, bp, g_pre_mix[l:l + 1], w_in_t, bf_p[l:l + 1], None, None, bs, ts)
        kss.append(k_s)
        vss.append(v_s)
        lfs.append(lf.reshape(N_HEADS, bs, ts).transpose(1, 0, 2))
        att = _attn_sample_call(qa, ka, va, cache_kt, cache_vt, cache_logf, l)
        ssm, hfin = _s5_call(u, toep, bpow, cpow, scm, w_glu_b, h0, l, LANES, bs * seg_s, seg_s, 0)
        sss.append(hfin[0, :, seg_s - 1:bs * seg_s:seg_s, :].transpose(1, 0, 2))
        ys = _out_ffn_call(ys, att, ssm, mod_s, l, bp, g_post_mix[l:l + 1], g_pre_ffn[l:l + 1],
                           g_post_ffn[l:l + 1], w_out_b, w_gate_b, w_up_b, w_down_b, bs, ts)

    ssp, sss = jnp.stack(ssp), jnp.stack(sss)
    return (yp, ys, jnp.swapaxes(kvp[0], 3, 4), jnp.swapaxes(kvp[1], 3, 4), jnp.stack(lfp),
            ssp[..., :SSM_STATE], ssp[..., SSM_STATE:],
            jnp.stack(kss), jnp.stack(vss), jnp.stack(lfs),
            sss[..., :SSM_STATE], sss[..., SSM_STATE:])
```

```python
import functools
import math

import numpy as np
import jax
import jax.numpy as jnp
from jax import lax
from jax.experimental import pallas as pl
from jax.experimental.pallas import tpu as pltpu

F32 = jnp.float32
BF16 = jnp.bfloat16

D_MODEL = 1024
N_HEADS = 8
HEAD_DIM = 64
D_ATT = N_HEADS * HEAD_DIM
D_SSM = D_MODEL - D_ATT
SSM_GROUP = 16
N_GROUPS = D_SSM // SSM_GROUP
SSM_STATE = 64
D_FF = 2816
EPS = 1e-6

LANES = 128
S5_CHUNK = 16
S5_ROWS = S5_CHUNK * SSM_GROUP
GATE_PAD = LANES
W_IN_COLS = 3 * D_ATT + GATE_PAD + D_SSM
U_COL0 = 3 * D_ATT + GATE_PAD
N_U_SLABS = D_SSM // LANES
SCAN_PAD = 128
S5_PAIRS_PER_ITER = 2
VMEM_LIMIT = 56 * 1024 * 1024
MASK_VALUE = -1e30
LOG2E = math.log2(math.e)
DEN_LANE = HEAD_DIM
N_EXTRA = 6
EXTRA_SLOT = 8
SCORE_LIMIT = 80.0
NORM_SLACK = 1.02
EXP2_ZERO_BELOW = -math.inf

FIN_ROWS = 32

PROMPT_BLOCK = 512
ATTN_HEADS_PER_STEP = N_HEADS
ATTN_HEADS_PER_LOOP = 4
ADA_COLS = 1536
TABLE_GROUPS = 8

NT_DIMS = (((1,), (1,)), ((), ()))


def _sigmoid(x):
    return 1.0 / (1.0 + jnp.exp(-x))


def _split3(x):
    hi = x.astype(BF16)
    r = x - hi.astype(F32)
    mid = r.astype(BF16)
    lo = (r - mid.astype(F32)).astype(BF16)
    return hi, mid, lo


def _split2(x):
    hi = x.astype(BF16)
    return hi, (x - hi.astype(F32)).astype(BF16)


def _dot(a, b):
    return jnp.dot(a, b, preferred_element_type=F32)


def _dot_nt(a, b):
    return lax.dot_general(a, b, NT_DIMS, preferred_element_type=F32)


def _rms(x, g):
    ms = jnp.mean(x * x, axis=-1, keepdims=True)
    return x * lax.rsqrt(ms + EPS) * g


def _per_row(ref, nb, tb):
    m = ref[...]
    if nb == 1:
        return m.reshape(1, D_MODEL)
    return jnp.broadcast_to(m[:, None, :], (nb, tb, D_MODEL)).reshape(nb * tb, D_MODEL)


def _ada_kernel(c_ref, w_ref, b_ref, o_ref):
    c = c_ref[...]
    a_hi, a_lo = _split2(c * _sigmoid(c))
    w_hi, w_lo = _split2(w_ref[0])
    o_ref[0] = _dot(a_hi, w_hi) + _dot(a_lo, w_hi) + _dot(a_hi, w_lo) + b_ref[0]


def _ada_call(c_all, w_ada, b_ada):
    depth = w_ada.shape[0]
    nb = ADA_COLS
    n_out = w_ada.shape[2]
    return pl.pallas_call(
        _ada_kernel,
        grid=(depth, n_out // nb),
        in_specs=[
            pl.BlockSpec((16, D_MODEL), lambda l, j: (0, 0)),
            pl.BlockSpec((1, D_MODEL, nb), lambda l, j: (l, 0, j)),
            pl.BlockSpec((1, 1, nb), lambda l, j: (l, 0, j)),
        ],
        out_specs=pl.BlockSpec((1, 16, nb), lambda l, j: (l, 0, j)),
        out_shape=jax.ShapeDtypeStruct((depth, 16, n_out), F32),
        compiler_params=pltpu.CompilerParams(
            dimension_semantics=("arbitrary", "arbitrary"), vmem_limit_bytes=VMEM_LIMIT),
        name="ada",
    )(c_all, w_ada, b_ada.reshape(depth, 1, n_out))


def _inproj_kernel(x_ref, sh_ref, sc_ref, g_ref, wt_ref, bf_ref, tri_ref, sel_ref, one_ref, eye_ref,
                   go_ref, qa_ref, ka_ref, va_ref, kc_ref, vc_ref, lf_ref, u_ref, st_ref, carry_ref,
                   *, nb, tb, kv_transposed):
    rows = nb * tb

    @pl.when(pl.program_id(1) == 0)
    def _():
        carry_ref[...] = jnp.zeros_like(carry_ref)

    x = x_ref[...].reshape(rows, D_MODEL)
    hm = _rms(x, g_ref[...]) * (1.0 + _per_row(sc_ref, nb, tb)) + _per_row(sh_ref, nb, tb)
    proj = _dot_nt(hm.astype(BF16), wt_ref[...])
    q = proj[:, 0:D_ATT] * (LOG2E * HEAD_DIM ** -0.5)
    k = proj[:, D_ATT:2 * D_ATT]
    v = proj[:, 2 * D_ATT:3 * D_ATT]
    gate = proj[:, 3 * D_ATT:U_COL0] + bf_ref[...]
    logf = jnp.minimum(gate, 0.0) - jnp.log1p(jnp.exp(-jnp.abs(gate)))
    lane = lax.broadcasted_iota(jnp.int32, (rows, LANES), 1)

    def pack3(val):
        hi, mid, lo = (part.astype(F32) for part in _split3(val))
        packed = jnp.where(lane < 2 * N_HEADS, pltpu.roll(mid, N_HEADS, axis=1),
                           jnp.where(lane < 3 * N_HEADS, pltpu.roll(lo, 2 * N_HEADS, axis=1), 0.0))
        return jnp.where(lane < N_HEADS, hi, packed).astype(BF16)

    lf_pack = pack3(logf)
    cs = _dot(tri_ref[...], lf_pack)
    fcum = (cs + pltpu.roll(cs, LANES - N_HEADS, axis=1) + pltpu.roll(cs, LANES - 2 * N_HEADS, axis=1)
            + carry_ref[...])
    carry_ref[...] = fcum[rows - 1:rows, :]

    extra = _dot(pack3(fcum * LOG2E), sel_ref[...]) + one_ref[...]
    in_extra = (lane >= HEAD_DIM) & (lane < HEAD_DIM + N_EXTRA)
    qn2 = _dot((q * q).astype(BF16), go_ref[...])
    kn2 = _dot((k * k).astype(BF16), go_ref[...])
    st_ref[0, 0] = jnp.concatenate(
        [jnp.max(qn2, axis=0, keepdims=True), jnp.max(kn2, axis=0, keepdims=True),
         fcum[0:1, :], fcum[rows - 1:rows, :], jnp.zeros((4, LANES), F32)], axis=0)
    lo_half = lane < HEAD_DIM
    den_one = jnp.where(lane == DEN_LANE, 1.0, 0.0)
    if kv_transposed:
        kt, vt = k.T, v.T
    for h in range(N_HEADS):
        pair = slice(LANES * (h // 2), LANES * (h // 2) + LANES)
        q2, k2, v2 = q[:, pair], k[:, pair], v[:, pair]
        if h % 2:
            q2 = pltpu.roll(q2, HEAD_DIM, axis=1)
            k2 = pltpu.roll(k2, HEAD_DIM, axis=1)
            v2 = pltpu.roll(v2, HEAD_DIM, axis=1)
        per_stream = lambda val: val.reshape(nb, tb, val.shape[-1])
        to_extra = lambda side: jnp.where(in_extra, pltpu.roll(
            extra[:, LANES * side:LANES * (side + 1)], (HEAD_DIM - EXTRA_SLOT * h) % LANES, axis=1), 0.0)
        qa_ref[:, h] = per_stream(jnp.where(lo_half, q2, to_extra(0)).astype(BF16))
        ka_ref[:, h] = per_stream(jnp.where(lo_half, k2, to_extra(1)).astype(BF16))
        va_ref[:, h] = per_stream(jnp.where(lo_half, v2, den_one).astype(BF16))
        if kv_transposed:
            kc_ref[h] = kt[HEAD_DIM * h:HEAD_DIM * (h + 1), :]
            vc_ref[h] = vt[HEAD_DIM * h:HEAD_DIM * (h + 1), :]
        else:
            kc_ref[:, h] = per_stream(k2[:, 0:HEAD_DIM])
            vc_ref[:, h] = per_stream(v2[:, 0:HEAD_DIM])
    lf_t = _dot_nt(eye_ref[...], lf_pack)
    lf_ref[0] = lf_t[0:N_HEADS] + lf_t[N_HEADS:2 * N_HEADS] + lf_t[2 * N_HEADS:3 * N_HEADS]
    for j in range(N_U_SLABS):
        u_ref[0, j] = proj[:, U_COL0 + LANES * j:U_COL0 + LANES * (j + 1)]


def _inproj_consts(nb, tb):
    tri = np.kron(np.eye(nb, dtype=np.float32), np.tril(np.ones((tb, tb), np.float32)))
    sel = np.zeros((LANES, 2 * LANES), np.float32)
    one = np.zeros((1, 2 * LANES), np.float32)
    group_ones = np.zeros((D_ATT, LANES), np.float32)
    for h in range(N_HEADS):
        group_ones[HEAD_DIM * h:HEAD_DIM * (h + 1), h] = 1.0
        for p in range(3):
            sel[p * N_HEADS + h, EXTRA_SLOT * h + p] = 1.0
            one[0, EXTRA_SLOT * h + 3 + p] = 1.0
            one[0, LANES + EXTRA_SLOT * h + p] = 1.0
            sel[p * N_HEADS + h, LANES + EXTRA_SLOT * h + 3 + p] = -1.0
    eye = np.eye(4 * N_HEADS, LANES, dtype=np.float32)
    eye[3 * N_HEADS:] = 0.0
    return (jnp.asarray(tri, BF16), jnp.asarray(sel, BF16), jnp.asarray(one, F32), jnp.asarray(eye, BF16),
            jnp.asarray(group_ones, BF16))


def _mod_spec(layer, row0, chunk, nb=1):
    if nb == 1:
        return pl.BlockSpec((1, 1, D_MODEL), lambda b, i: ((layer * 16 + row0 + b) * 6 + chunk, 0, 0))
    return pl.BlockSpec((None, None, nb, D_MODEL), lambda b, i: (layer, chunk, 0, 0))


def _inproj_call(x, mods, layer, row0, g_pre, w_in_t, b_forget, kv_bufs, kv_depth, nb, tb):
    bsz, t, _ = x.shape
    nblk = t // tb
    rows = nb * tb
    assert nb == 1 or (nblk == 1 and kv_depth is None)
    tri, sel, one, eye, group_ones = _inproj_consts(nb, tb)
    if nb > 1:
        u_shape = (1, N_U_SLABS, bsz * t, LANES)
        u_spec = pl.BlockSpec((1, N_U_SLABS, rows, LANES), lambda b, i: (0, 0, b, 0))
        lf_shape = (1, N_HEADS, bsz * t)
        lf_spec = pl.BlockSpec((1, N_HEADS, rows), lambda b, i: (0, 0, b))
    else:
        u_shape = (bsz, N_U_SLABS, t, LANES)
        u_spec = pl.BlockSpec((1, N_U_SLABS, tb, LANES), lambda b, i: (b, 0, i, 0))
        lf_shape = (bsz, N_HEADS, t)
        lf_spec = pl.BlockSpec((1, N_HEADS, tb), lambda b, i: (b, 0, i))
    const = lambda b, i: (0, 0)
    in_specs = [
        pl.BlockSpec((nb, tb, D_MODEL), lambda b, i: (b, i, 0)),
        _mod_spec(layer, row0, 0, nb), _mod_spec(layer, row0, 1, nb),
        pl.BlockSpec((1, D_MODEL), const),
        pl.BlockSpec((None, W_IN_COLS, D_MODEL), lambda b, i: (layer, 0, 0)),
        pl.BlockSpec((1, GATE_PAD), const),
        pl.BlockSpec((rows, rows), const),
        pl.BlockSpec((LANES, 2 * LANES), const),
        pl.BlockSpec((1, 2 * LANES), const),
        pl.BlockSpec((4 * N_HEADS, LANES), const),
        pl.BlockSpec((D_ATT, LANES), const),
    ]
    args = [x, mods, mods, g_pre, w_in_t, b_forget, tri, sel, one, eye, group_ones]
    n_main = len(args)
    head_blk = pl.BlockSpec((nb, N_HEADS, tb, LANES), lambda b, i: (b, 0, i, 0))
    aliases = {}
    if kv_depth is None:
        kv_shape = jax.ShapeDtypeStruct((bsz, N_HEADS, t, HEAD_DIM), F32)
        kv_spec = pl.BlockSpec((nb, N_HEADS, tb, HEAD_DIM), lambda b, i: (b, 0, i, 0))
    else:
        kv_shape = jax.ShapeDtypeStruct((kv_depth, bsz, N_HEADS, HEAD_DIM, t), F32)
        kv_spec = pl.BlockSpec((None, None, N_HEADS, HEAD_DIM, tb), lambda b, i: (layer, b, 0, 0, i))
        if kv_bufs is not None:
            in_specs += [pl.BlockSpec(memory_space=pl.ANY), pl.BlockSpec(memory_space=pl.ANY)]
            args += list(kv_bufs)
            aliases = {n_main: 3, n_main + 1: 4}
    out_shape = [
        jax.ShapeDtypeStruct((bsz, N_HEADS, t, LANES), BF16),
        jax.ShapeDtypeStruct((bsz, N_HEADS, t, LANES), BF16),
        jax.ShapeDtypeStruct((bsz, N_HEADS, t, LANES), BF16),
        kv_shape, kv_shape,
        jax.ShapeDtypeStruct(lf_shape, F32),
        jax.ShapeDtypeStruct(u_shape, F32),
        jax.ShapeDtypeStruct((bsz // nb, nblk, 8, LANES), F32),
    ]
    out_specs = [
        head_blk, head_blk, head_blk,
        kv_spec, kv_spec,
        lf_spec,
        u_spec,
        pl.BlockSpec((1, 1, 8, LANES), lambda b, i: (b, i, 0, 0)),
    ]
    n_args = len(args)

    def body(*refs):
        _inproj_kernel(*refs[:n_main], *refs[n_args:], nb=nb, tb=tb, kv_transposed=kv_depth is not None)

    return pl.pallas_call(
        body,
        grid=(bsz // nb, nblk),
        in_specs=in_specs,
        out_specs=out_specs,
        out_shape=out_shape,
        scratch_shapes=[pltpu.VMEM((1, LANES), F32)],
        input_output_aliases=aliases,
        compiler_params=pltpu.CompilerParams(
            dimension_semantics=("arbitrary", "arbitrary"), vmem_limit_bytes=VMEM_LIMIT),
        name="inproj",
    )(*args)


def _finish_heads(accs, lane):
    outs = []
    for acc in accs:
        den = jnp.sum(jnp.where(lane == DEN_LANE, acc, 0.0), axis=1, keepdims=True)
        outs.append(acc * (1.0 / den))
    return jnp.where(lane < HEAD_DIM, outs[0], pltpu.roll(outs[1], HEAD_DIM, axis=1)).astype(BF16)


def _attn_kernel(small_ref, first_ref, qa_ref, ka_ref, va_ref, o_ref, *, tq, hps, lps):
    i = pl.program_id(2)
    row = lax.broadcasted_iota(jnp.int32, (tq, tq), 0)
    col = lax.broadcasted_iota(jnp.int32, (tq, tq), 1)
    causal = col <= row
    lane = lax.broadcasted_iota(jnp.int32, (tq, LANES), 1)
    qs = [qa_ref[0, hh] for hh in range(hps)]

    def score(j, hh, masked):
        start = pl.multiple_of(j * tq, tq)
        s = _dot_nt(qs[hh], ka_ref[0, hh, pl.ds(start, tq), :])
        return jnp.where(causal, s, MASK_VALUE) if masked else s

    def values(j, hh):
        return va_ref[0, hh, pl.ds(pl.multiple_of(j * tq, tq), tq), :]

    def store(accs):
        for pr in range(hps // 2):
            o_ref[0, :, LANES * pr:LANES * (pr + 1)] = _finish_heads(accs[2 * pr:2 * pr + 2], lane)

    def tile_plain(j, accs, heads):
        return tuple(acc + _dot(jnp.exp2(score(j, hh, False)).astype(BF16), values(j, hh))
                     for acc, hh in zip(accs, heads))

    def diagonal_plain(accs):
        half = tq // 2
        start = pl.multiple_of(i * tq, tq)
        upper = (lax.broadcasted_iota(jnp.int32, (half, half), 1) <=
                 lax.broadcasted_iota(jnp.int32, (half, half), 0))
        lower = (lax.broadcasted_iota(jnp.int32, (half, tq), 1) <=
                 lax.broadcasted_iota(jnp.int32, (half, tq), 0) + half)
        new = []
        for hh in range(hps):
            k_t = ka_ref[0, hh, pl.ds(start, tq), :]
            v_t = va_ref[0, hh, pl.ds(start, tq), :]
            s_top = jnp.where(upper, _dot_nt(qs[hh][0:half], k_t[0:half]), MASK_VALUE)
            s_bot = jnp.where(lower, _dot_nt(qs[hh][half:tq], k_t), MASK_VALUE)
            o_top = _dot(jnp.exp2(s_top).astype(BF16), v_t[0:half])
            o_bot = _dot(jnp.exp2(s_bot).astype(BF16), v_t)
            new.append(accs[hh] + jnp.concatenate([o_top, o_bot], axis=0))
        return new

    def tile_shifted(j, carry, masked):
        new = []
        for hh in range(hps):
            m, acc = carry[hh]
            s = score(j, hh, masked)
            m_new = jnp.maximum(m, jnp.max(s, axis=1, keepdims=True))
            p = jnp.exp2(s - m_new)
            new.append((m_new, jnp.exp2(m - m_new) * acc + _dot(p.astype(BF16), values(j, hh))))
        return tuple(new)

    small = small_ref[pl.program_id(0), pl.program_id(1)] != 0
    zero = jnp.zeros((tq, LANES), F32)

    @pl.when(small)
    def _():
        accs = []
        for sub in range(hps // lps):
            heads = tuple(range(sub * lps, (sub + 1) * lps))
            first = first_ref[pl.program_id(0), pl.program_id(1) * (hps // lps) + sub, i]
            accs += lax.fori_loop(
                0, i - first,
                lambda it, a, first=first, heads=heads: tile_plain(first + it, a, heads), (zero,) * lps)
        store(diagonal_plain(accs))

    @pl.when(jnp.logical_not(small))
    def _():
        init = ((jnp.full((tq, 1), MASK_VALUE, F32), zero),) * hps
        carry = lax.fori_loop(0, i, functools.partial(tile_shifted, masked=False), init)
        store([c[1] for c in tile_shifted(i, carry, True)])


def _attn_call(qa, ka, va, stats, tq, hps, lps):
    bsz, _, t, _ = qa.shape
    nt, ng, nl = t // tq, N_HEADS // hps, N_HEADS // lps
    assert stats.shape[1] == nt
    bound = jnp.sqrt(jnp.max(stats[:, :, 0, :N_HEADS], axis=1) *
                     jnp.max(stats[:, :, 1, :N_HEADS], axis=1)) * NORM_SLACK
    small = jnp.max(bound.reshape(bsz, ng, hps), axis=-1) < SCORE_LIMIT
    f_first, f_last = stats[:, :, 2, :N_HEADS] * LOG2E, stats[:, :, 3, :N_HEADS] * LOG2E
    top = bound[:, None, None, :] + f_first[:, :, None, :] - f_last[:, None, :, :]
    needed = jnp.max((top >= EXP2_ZERO_BELOW).astype(jnp.int32).reshape(bsz, nt, nt, nl, lps), axis=-1)
    first = jnp.where(jnp.repeat(small, nl // ng, axis=1)[:, :, None],
                      jnp.sum(1 - needed, axis=2).transpose(0, 2, 1), 0)
    whole = pl.BlockSpec((1, hps, t, LANES), lambda b, p, i, *_: (b, p, 0, 0))
    return pl.pallas_call(
        functools.partial(_attn_kernel, tq=tq, hps=hps, lps=lps),
        grid_spec=pltpu.PrefetchScalarGridSpec(
            num_scalar_prefetch=2,
            grid=(bsz, ng, nt),
            in_specs=[pl.BlockSpec((1, hps, tq, LANES), lambda b, p, i, *_: (b, p, i, 0)), whole, whole],
            out_specs=pl.BlockSpec((1, tq, HEAD_DIM * hps), lambda b, p, i, *_: (b, i, p)),
        ),
        out_shape=jax.ShapeDtypeStruct((bsz, t, D_ATT), BF16),
        compiler_params=pltpu.CompilerParams(
            dimension_semantics=("arbitrary", "arbitrary", "arbitrary"),
            vmem_limit_bytes=VMEM_LIMIT),
        name="fox_attn",
    )(small.astype(jnp.int32), first.astype(jnp.int32), qa, ka, va)


def _attn_sample_kernel(qa_ref, ka_ref, va_ref, ckt_ref, cvt_ref, clf_ref, o_ref, *, t, n_past):
    lane_p = lax.broadcasted_iota(jnp.int32, (N_HEADS, n_past), 1)
    c = clf_ref[...]
    total = jnp.sum(c, axis=1, keepdims=True)
    d = 1
    while d < n_past:
        c = c + jnp.where(lane_p >= d, pltpu.roll(c, d, axis=1), 0.0)
        d *= 2
    f_past = (c - total) * LOG2E
    lane = lax.broadcasted_iota(jnp.int32, (t, LANES), 1)
    causal = (lax.broadcasted_iota(jnp.int32, (t, t), 1) <= lax.broadcasted_iota(jnp.int32, (t, t), 0))
    den_rows = jnp.where(lax.broadcasted_iota(jnp.int32, (LANES - HEAD_DIM, n_past), 0) == 0, 1.0, 0.0)
    accs = []
    for h in range(N_HEADS):
        q = qa_ref[0, h]
        fq = jnp.sum(jnp.where((lane >= HEAD_DIM) & (lane < HEAD_DIM + 3), q.astype(F32), 0.0),
                     axis=1, keepdims=True)
        s_past = _dot(q[:, 0:HEAD_DIM], ckt_ref[h].astype(BF16)) + fq - f_past[h:h + 1, :]
        s_new = jnp.where(causal, _dot_nt(q, ka_ref[0, h]), MASK_VALUE)
        m = jnp.maximum(jnp.max(s_past, axis=1, keepdims=True), jnp.max(s_new, axis=1, keepdims=True))
        p_past = jnp.exp2(s_past - m).astype(BF16)
        p_new = jnp.exp2(s_new - m).astype(BF16)
        cvt_aug = jnp.concatenate([cvt_ref[h], den_rows], axis=0).astype(BF16)
        accs.append(_dot_nt(p_past, cvt_aug) + _dot(p_new, va_ref[0, h]))
    for pr in range(N_HEADS // 2):
        o_ref[0, :, LANES * pr:LANES * (pr + 1)] = _finish_heads(accs[2 * pr:2 * pr + 2], lane)


def _attn_sample_call(qa, ka, va, cache_kt, cache_vt, cache_logf, layer):
    bsz, _, t, _ = qa.shape
    n_past = cache_kt.shape[4]
    cache_spec = pl.BlockSpec((None, None, N_HEADS, HEAD_DIM, n_past), lambda b: (layer, b, 0, 0, 0))
    new_spec = pl.BlockSpec((1, N_HEADS, t, LANES), lambda b: (b, 0, 0, 0))
    return pl.pallas_call(
        functools.partial(_attn_sample_kernel, t=t, n_past=n_past),
        grid=(bsz,),
        in_specs=[
            new_spec, new_spec, new_spec, cache_spec, cache_spec,
            pl.BlockSpec((None, None, N_HEADS, n_past), lambda b: (layer, b, 0, 0)),
        ],
        out_specs=pl.BlockSpec((1, t, D_ATT), lambda b: (b, 0, 0)),
        out_shape=jax.ShapeDtypeStruct((bsz, t, D_ATT), BF16),
        compiler_params=pltpu.CompilerParams(
            dimension_semantics=("arbitrary",), vmem_limit_bytes=VMEM_LIMIT),
        name="fox_attn_sample",
    )(qa, ka, va, cache_kt, cache_vt, cache_logf)


def _s5_table_kernel(ca_ref, cb_ref, ba_ref, bb_ref, prr_ref, pii_ref, dv_ref,
                     toep_ref, bpow_ref, cpow_ref, tt_scr, *, gb):
    row = lax.broadcasted_iota(jnp.int32, (SSM_GROUP, S5_ROWS), 0)
    lane = lax.broadcasted_iota(jnp.int32, (SSM_GROUP, S5_ROWS), 1)
    for gi in range(gb):
        ca, cb, ba, bb = ca_ref[gi], cb_ref[gi], ba_ref[gi], bb_ref[gi]
        power = lambda a, b, n: a * prr_ref[gi, n:n + 1, :] + b * pii_ref[gi, n:n + 1, :]
        xs = [power(ca, cb, n) for n in range(S5_CHUNK + 1)]
        cpow_ref[gi] = jnp.concatenate(xs[1:], axis=0).astype(BF16)
        bpow_ref[gi] = jnp.concatenate(
            [power(ba, bb, S5_CHUNK - 1 - s) for s in range(S5_CHUNK)], axis=0).T.astype(BF16)
        x_hi, x_lo = _split2(jnp.concatenate(xs[:S5_CHUNK], axis=0))
        b_hi, b_lo = _split2(ba)
        r0 = _dot_nt(b_hi, x_hi) + _dot_nt(b_lo, x_hi) + _dot_nt(b_hi, x_lo)
        r0 = r0 + jnp.where(lane == row, dv_ref[gi], 0.0)
        for s in range(S5_CHUNK):
            blk = r0
            if s:
                blk = jnp.where(lane >= SSM_GROUP * s, pltpu.roll(r0, SSM_GROUP * s, axis=1), 0.0)
            tt_scr[SSM_GROUP * s:SSM_GROUP * (s + 1), :] = blk
        toep_ref[gi] = tt_scr[...].T.astype(BF16)


def _s5_tables(a_re, a_im, log_dt, b_re_t, b_im_t, c_re, c_im, d_skip):
    depth = a_re.shape[0]
    dt = jnp.exp(log_dt)[..., None]
    x, y = a_re * dt, a_im * dt
    ex, cy, sy = jnp.exp(x), jnp.cos(y), jnp.sin(y)
    ar, ai = ex * cy, ex * sy
    sh = jnp.sin(0.5 * y)
    nr, ni = jnp.expm1(x) * cy - 2.0 * sh * sh, ai
    den = a_re * a_re + a_im * a_im
    fr = ((nr * a_re + ni * a_im) / den)[:, :, None, :]
    fi = ((ni * a_re - nr * a_im) / den)[:, :, None, :]
    bbr, bbi = fr * b_re_t - fi * b_im_t, fr * b_im_t + fi * b_re_t
    pr, pi = [jnp.ones_like(ar)], [jnp.zeros_like(ar)]
    for _ in range(S5_CHUNK):
        pr, pi = pr + [pr[-1] * ar - pi[-1] * ai], pi + [pr[-1] * ai + pi[-1] * ar]
    mr, mi = pr[S5_CHUNK], pi[S5_CHUNK]
    pr, pi = jnp.stack(pr, axis=2), jnp.stack(pi, axis=2)
    cat = lambda a, b: jnp.concatenate([a, b], axis=-1)
    ins = [cat(c_re, -c_im), cat(-c_im, -c_re), cat(bbr, bbi), cat(-bbi, bbr), cat(pr, pr), cat(pi, pi),
           jnp.pad(d_skip.reshape(depth, N_GROUPS, 1, SSM_GROUP), ((0, 0), (0, 0), (0, 0), (0, S5_ROWS - SSM_GROUP)))]
    gb = TABLE_GROUPS
    spec = lambda rows, cols: pl.BlockSpec((None, gb, rows, cols), lambda l, j: (l, j, 0, 0))
    tab = lambda rows, cols: jax.ShapeDtypeStruct((depth, N_GROUPS, rows, cols), BF16)
    toep, bpow, cpow = pl.pallas_call(
        functools.partial(_s5_table_kernel, gb=gb),
        grid=(depth, N_GROUPS // gb),
        in_specs=[spec(SSM_GROUP, LANES)] * 4 + [spec(S5_CHUNK + 1, LANES)] * 2 + [spec(1, S5_ROWS)],
        out_specs=[spec(S5_ROWS, S5_ROWS), spec(LANES, S5_ROWS), spec(S5_ROWS, LANES)],
        out_shape=[tab(S5_ROWS, S5_ROWS), tab(LANES, S5_ROWS), tab(S5_ROWS, LANES)],
        scratch_shapes=[pltpu.VMEM((S5_ROWS, S5_ROWS), F32)],
        compiler_params=pltpu.CompilerParams(
            dimension_semantics=("arbitrary", "arbitrary"), vmem_limit_bytes=VMEM_LIMIT),
        name="s5_tables",
    )(*ins)
    rows_r, rows_i = [], []
    for _ in range(8):
        rows_r.append(cat(mr, mr))
        rows_i.append(cat(-mi, mi))
        mr, mi = mr * mr - mi * mi, 2.0 * mr * mi
    scm = jnp.stack(rows_r + rows_i, axis=2)
    return toep, bpow, cpow, scm


def _gelu_tanh(y):
    return 0.5 * y * (1.0 + jnp.tanh(math.sqrt(2.0 / math.pi) * (y + 0.044715 * (y * y * y))))


def _s5_kernel(*refs, n, n_real, seglen, fin_row0, use_h0):
    if use_h0:
        u_ref, toep_ref, bpow_ref, cpow_ref, scm_ref, wglu_ref, h0_ref = refs[:7]
        o_ref, hfin_ref, z_scr, zt_scr, h_scr, h0_scr = refs[7:]
    else:
        u_ref, toep_ref, bpow_ref, cpow_ref, scm_ref, wglu_ref = refs[:6]
        o_ref, hfin_ref, z_scr, zt_scr, h_scr = refs[6:]
    gps = LANES // SSM_GROUP
    for pp in range(S5_PAIRS_PER_ITER):
        for part in range(2):
            h_scr[pp, part, 0:SCAN_PAD, :] = jnp.zeros((SCAN_PAD, LANES), F32)
    if use_h0:
        h0_scr[...] = jnp.zeros(h0_scr.shape, F32)

    for k in range(S5_CHUNK):
        for j in range(N_U_SLABS):
            vv = u_ref[0, j, pl.ds(k, n_real, stride=S5_CHUNK), :]
            if n_real < n:
                vv = jnp.concatenate([vv, jnp.zeros((n - n_real, LANES), F32)], axis=0)
            z_scr[gps * j:gps * (j + 1), SSM_GROUP * k:SSM_GROUP * (k + 1), :] = (
                vv.T.reshape(gps, SSM_GROUP, n).astype(BF16))

    chunk_in_seg = lax.broadcasted_iota(jnp.int32, (n, LANES), 0) % seglen
    lo = lax.broadcasted_iota(jnp.int32, (n, LANES), 1) < SSM_STATE
    lo_row = lax.broadcasted_iota(jnp.int32, (1, LANES), 1) < SSM_STATE
    lo_fin = lax.broadcasted_iota(jnp.int32, (FIN_ROWS, LANES), 1) < SSM_STATE
    swap = lambda a: pltpu.roll(a, SSM_STATE, axis=1)

    def pair_step(it, _):
        for pp in range(S5_PAIRS_PER_ITER):
            gs = [2 * (it * S5_PAIRS_PER_ITER + pp), 2 * (it * S5_PAIRS_PER_ITER + pp) + 1]
            ucts = [z_scr[g] for g in gs]
            yts = [_dot(toep_ref[g], u) for g, u in zip(gs, ucts)]
            hs = [_dot(bpow_ref[g], u).T for g, u in zip(gs, ucts)]
            h0s = [0.0, 0.0]
            if use_h0:
                for idx, g in enumerate(gs):
                    h0_scr[idx, pl.ds(0, h0_ref.shape[1], stride=seglen), :] = h0_ref[g]
                    h0s[idx] = h0_scr[idx]
                    hs[idx] = hs[idx] + scm_ref[g, 0:1, :] * h0s[idx] + scm_ref[g, 8:9, :] * swap(h0s[idx])
            hre = jnp.where(lo, hs[0], swap(hs[1]))
            him = jnp.where(lo, swap(hs[0]), hs[1])

            def shifted(by):
                h_scr[pp, 0, SCAN_PAD:SCAN_PAD + n, :] = hre
                h_scr[pp, 1, SCAN_PAD:SCAN_PAD + n, :] = him
                return (h_scr[pp, 0, pl.ds(SCAN_PAD - by, n), :], h_scr[pp, 1, pl.ds(SCAN_PAD - by, n), :])

            d, j = 1, 0
            while d < seglen:
                sre, sim = shifted(d)
                if seglen < n:
                    sre = jnp.where(chunk_in_seg >= d, sre, 0.0)
                    sim = jnp.where(chunk_in_seg >= d, sim, 0.0)
                ar = jnp.where(lo_row, scm_ref[gs[0], j:j + 1, :], scm_ref[gs[1], j:j + 1, :])
                ai = jnp.where(lo_row, -scm_ref[gs[0], 8 + j:9 + j, :], scm_ref[gs[1], 8 + j:9 + j, :])
                hre, him = hre + ar * sre - ai * sim, him + ar * sim + ai * sre
                d, j = 2 * d, j + 1
            pre, pim = shifted(1)
            hps = [jnp.where(lo, pre, swap(pim)), jnp.where(lo, swap(pre), pim)]
            fre, fim = hre[fin_row0:fin_row0 + FIN_ROWS, :], him[fin_row0:fin_row0 + FIN_ROWS, :]
            fins = [jnp.where(lo_fin, fre, swap(fim)), jnp.where(lo_fin, swap(fre), fim)]
            for idx, g in enumerate(gs):
                hp = hps[idx]
                if seglen < n:
                    hp = jnp.where(chunk_in_seg >= 1, hp, h0s[idx])
                zt_scr[g] = _gelu_tanh(yts[idx] + _dot_nt(cpow_ref[g], hp.astype(BF16)))
                hfin_ref[0, g] = fins[idx]
        return 0

    lax.fori_loop(0, N_GROUPS // (2 * S5_PAIRS_PER_ITER), pair_step, 0)

    for k in range(S5_CHUNK):
        slabs = []
        for j in range(N_U_SLABS):
            w = zt_scr[gps * j:gps * (j + 1), SSM_GROUP * k:SSM_GROUP * (k + 1), :]
            slabs.append(w.reshape(LANES, n).T)
        zk = jnp.concatenate(slabs, axis=1)
        out = zk * _sigmoid(_dot(zk.astype(BF16), wglu_ref[...]))
        for j in range(N_U_SLABS):
            o_ref[0, j, pl.ds(k, n_real, stride=S5_CHUNK), :] = out[0:n_real, LANES * j:LANES * (j + 1)]


def _s5_call(u, toep, bpow, cpow, scm, wglu, h0, layer, n, n_real, seglen, fin_row0):
    nb, _, tt, _ = u.shape
    use_h0 = h0 is not None
    one = pl.Buffered(1)
    tab = lambda rows, cols: pl.BlockSpec((None, N_GROUPS, rows, cols), lambda b: (layer, 0, 0, 0),
                                          pipeline_mode=one)
    in_specs = [
        pl.BlockSpec((1, N_U_SLABS, tt, LANES), lambda b: (b, 0, 0, 0)),
        tab(S5_ROWS, S5_ROWS), tab(LANES, S5_ROWS), tab(S5_ROWS, LANES), tab(16, LANES),
        pl.BlockSpec((None, D_SSM, D_SSM), lambda b: (layer, 0, 0), pipeline_mode=one),
    ]
    args = [u, toep, bpow, cpow, scm, wglu]
    scratch = [
        pltpu.VMEM((N_GROUPS, S5_ROWS, n), BF16),
        pltpu.VMEM((N_GROUPS, S5_ROWS, n), F32),
        pltpu.VMEM((S5_PAIRS_PER_ITER, 2, SCAN_PAD + n, LANES), F32),
    ]
    if use_h0:
        in_specs.append(tab(h0.shape[2], LANES))
        args.append(h0)
        scratch.append(pltpu.VMEM((2, n, LANES), F32))
    return pl.pallas_call(
        functools.partial(_s5_kernel, n=n, n_real=n_real, seglen=seglen, fin_row0=fin_row0,
                          use_h0=use_h0),
        grid=(nb,),
        in_specs=in_specs,
        out_specs=[
            pl.BlockSpec((1, N_U_SLABS, tt, LANES), lambda b: (b, 0, 0, 0)),
            pl.BlockSpec((1, N_GROUPS, FIN_ROWS, LANES), lambda b: (b, 0, 0, 0)),
        ],
        out_shape=[
            jax.ShapeDtypeStruct((nb, N_U_SLABS, tt, LANES), F32),
            jax.ShapeDtypeStruct((nb, N_GROUPS, FIN_ROWS, LANES), F32),
        ],
        scratch_shapes=scratch,
        compiler_params=pltpu.CompilerParams(
            dimension_semantics=("arbitrary",), vmem_limit_bytes=VMEM_LIMIT),
        name="s5_mix",
    )(*args)


def _out_ffn_kernel(x_ref, att_ref, ssm_ref, g1_ref, sh2_ref, sc2_ref, g2_ref,
                    gpm_ref, gpf_ref, gpo_ref, wo_ref, wg_ref, wu_ref, wd_ref, o_ref, *, nb, tb):
    rows = nb * tb
    mix = jnp.concatenate(
        [att_ref[...].reshape(rows, D_ATT)] + [ssm_ref[0, j].astype(BF16) for j in range(N_U_SLABS)],
        axis=1)
    o = _dot(mix, wo_ref[...])
    x1 = x_ref[...].reshape(rows, D_MODEL) + _per_row(g1_ref, nb, tb) * _rms(o, gpm_ref[...])
    hf = (_rms(x1, gpf_ref[...]) * (1.0 + _per_row(sc2_ref, nb, tb)) + _per_row(sh2_ref, nb, tb)).astype(BF16)
    gt = _dot(hf, wg_ref[...])
    up = _dot(hf, wu_ref[...])
    hid = (gt * _sigmoid(gt) * up).astype(BF16)
    f = _dot(hid, wd_ref[...])
    o_ref[...] = (x1 + _per_row(g2_ref, nb, tb) * _rms(f, gpo_ref[...])).reshape(nb, tb, D_MODEL)


def _out_ffn_call(x, att, ssm, mods, layer, row0, g_post_mix, g_pre_ffn, g_post_ffn,
                  w_out, w_gate, w_up, w_down, nb, tb):
    bsz, t, _ = x.shape
    nblk = t // tb
    one = pl.Buffered(1)
    const = lambda b, i: (0, 0)
    lay = lambda b, i: (layer, 0, 0)
    if nb > 1:
        ssm_spec = pl.BlockSpec((1, N_U_SLABS, nb * tb, LANES), lambda b, i: (0, 0, b, 0))
    else:
        ssm_spec = pl.BlockSpec((1, N_U_SLABS, tb, LANES), lambda b, i: (b, 0, i, 0))
    row = pl.BlockSpec((1, D_MODEL), const)
    return pl.pallas_call(
        functools.partial(_out_ffn_kernel, nb=nb, tb=tb),
        grid=(bsz // nb, nblk),
        in_specs=[
            pl.BlockSpec((nb, tb, D_MODEL), lambda b, i: (b, i, 0)),
            pl.BlockSpec((nb, tb, D_ATT), lambda b, i: (b, i, 0)),
            ssm_spec,
            _mod_spec(layer, row0, 2, nb), _mod_spec(layer, row0, 3, nb), _mod_spec(layer, row0, 4, nb),
            _mod_spec(layer, row0, 5, nb),
            row, row, row,
            pl.BlockSpec((None, D_MODEL, D_MODEL), lay, pipeline_mode=one),
            pl.BlockSpec((None, D_MODEL, D_FF), lay, pipeline_mode=one),
            pl.BlockSpec((None, D_MODEL, D_FF), lay, pipeline_mode=one),
            pl.BlockSpec((None, D_FF, D_MODEL), lay, pipeline_mode=one),
        ],
        out_specs=pl.BlockSpec((nb, tb, D_MODEL), lambda b, i: (b, i, 0)),
        out_shape=jax.ShapeDtypeStruct((bsz, t, D_MODEL), F32),
        compiler_params=pltpu.CompilerParams(
            dimension_semantics=("arbitrary", "arbitrary"), vmem_limit_bytes=VMEM_LIMIT),
        name="out_ffn",
    )(x, att, ssm, mods, mods, mods, mods, g_post_mix, g_pre_ffn, g_post_ffn,
      w_out, w_gate, w_up, w_down)


def kernel(x_prompt, x_sample, c_prompt, c_sample, cache_k, cache_v, cache_logf, state_ssm_re,
           state_ssm_im, w_ada, b_ada, g_pre_mix, g_post_mix, g_pre_ffn, g_post_ffn, w_in, b_forget,
           ssm_a_re, ssm_a_im, ssm_log_dt, ssm_b_re, ssm_b_im, ssm_c_re, ssm_c_im, ssm_d, w_glu,
           w_out, w_gate, w_up, w_down):
    depth = w_in.shape[0]
    bp, tp, _ = x_prompt.shape
    bs, ts, _ = x_sample.shape
    n_att = 3 * D_ATT + N_HEADS

    c_all = jnp.concatenate([c_prompt, c_sample, jnp.zeros((16 - bp - bs, D_MODEL), F32)], axis=0)
    mod = _ada_call(c_all, w_ada, b_ada)
    modr = mod.reshape(depth * 16 * 6, 1, D_MODEL)
    mod_s = mod[:, bp:bp + bs].reshape(depth, bs, 6, D_MODEL).transpose(0, 2, 1, 3)

    w_in_t = jnp.swapaxes(w_in, 1, 2)
    w_in_t = jnp.concatenate(
        [w_in_t[:, :n_att], jnp.zeros((depth, GATE_PAD - N_HEADS, D_MODEL), F32), w_in_t[:, n_att:]],
        axis=1).astype(BF16)
    bf_p = jnp.pad(b_forget, ((0, 0), (0, GATE_PAD - N_HEADS)))
    w_glu_b, w_out_b = w_glu.astype(BF16), w_out.astype(BF16)
    w_gate_b, w_up_b, w_down_b = w_gate.astype(BF16), w_up.astype(BF16), w_down.astype(BF16)
    toep, bpow, cpow, scm = _s5_tables(ssm_a_re, ssm_a_im, ssm_log_dt, jnp.swapaxes(ssm_b_re, 2, 3),
                                       jnp.swapaxes(ssm_b_im, 2, 3), ssm_c_re, ssm_c_im, ssm_d)
    cache_kt, cache_vt = jnp.swapaxes(cache_k, 3, 4), jnp.swapaxes(cache_v, 3, 4)
    h0 = jnp.concatenate([state_ssm_re, state_ssm_im], axis=-1).transpose(0, 2, 1, 3)

    n_chunk_p = tp // S5_CHUNK
    seg_s = ts // S5_CHUNK
    yp, ys = x_prompt, x_sample
    kvp = None
    lfp, ssp, kss, vss, lfs, sss = [], [], [], [], [], []
    for l in range(depth):
        qa, ka, va, kbuf, vbuf, lf, u, stats = _inproj_call(
            yp, modr, l, 0, g_pre_mix[l:l + 1], w_in_t, bf_p[l:l + 1], kvp, depth, 1, PROMPT_BLOCK)
        kvp = (kbuf, vbuf)
        lfp.append(lf)
        att = _attn_call(qa, ka, va, stats, PROMPT_BLOCK, ATTN_HEADS_PER_STEP, ATTN_HEADS_PER_LOOP)
        ssm, hfin = _s5_call(u, toep, bpow, cpow, scm, w_glu_b, None, l, n_chunk_p, n_chunk_p,
                             n_chunk_p, n_chunk_p - FIN_ROWS)
        ssp.append(hfin[:, :, FIN_ROWS - 1, :])
        yp = _out_ffn_call(yp, att, ssm, modr, l, 0, g_post_mix[l:l + 1], g_pre_ffn[l:l + 1],
                           g_post_ffn[l:l + 1], w_out_b, w_gate_b, w_up_b, w_down_b, 1, PROMPT_BLOCK)

        qa, ka, va, k_s, v_s, lf, u, _ = _inproj_call(
            ys, mod_s, l, bp, g_pre_mix[l:l + 1], w_in_t, bf_p[l:l + 1], None, None, bs, ts)
        kss.append(k_s)
        vss.append(v_s)
        lfs.append(lf.reshape(N_HEADS, bs, ts).transpose(1, 0, 2))
        att = _attn_sample_call(qa, ka, va, cache_kt, cache_vt, cache_logf, l)
        ssm, hfin = _s5_call(u, toep, bpow, cpow, scm, w_glu_b, h0, l, LANES, bs * seg_s, seg_s, 0)
        sss.append(hfin[0, :, seg_s - 1:bs * seg_s:seg_s, :].transpose(1, 0, 2))
        ys = _out_ffn_call(ys, att, ssm, mod_s, l, bp, g_post_mix[l:l + 1], g_pre_ffn[l:l + 1],
                           g_post_ffn[l:l + 1], w_out_b, w_gate_b, w_up_b, w_down_b, bs, ts)

    ssp, sss = jnp.stack(ssp), jnp.stack(sss)
    return (yp, ys, jnp.swapaxes(kvp[0], 3, 4), jnp.swapaxes(kvp[1], 3, 4), jnp.stack(lfp),
            ssp[..., :SSM_STATE], ssp[..., SSM_STATE:],
            jnp.stack(kss), jnp.stack(vss), jnp.stack(lfs),
            sss[..., :SSM_STATE], sss[..., SSM_STATE:])
```

```python
import functools
import math

import numpy as np
import jax
import jax.numpy as jnp
from jax import lax
from jax.experimental import pallas as pl
from jax.experimental.pallas import tpu as pltpu

F32 = jnp.float32
BF16 = jnp.bfloat16

D_MODEL = 1024
N_HEADS = 8
HEAD_DIM = 64
D_ATT = N_HEADS * HEAD_DIM
D_SSM = D_MODEL - D_ATT
SSM_GROUP = 16
N_GROUPS = D_SSM // SSM_GROUP
SSM_STATE = 64
D_FF = 2816
EPS = 1e-6

LANES = 128
S5_CHUNK = 16
S5_ROWS = S5_CHUNK * SSM_GROUP
GATE_PAD = LANES
W_IN_COLS = 3 * D_ATT + GATE_PAD + D_SSM
U_COL0 = 3 * D_ATT + GATE_PAD
N_U_SLABS = D_SSM // LANES
SCAN_PAD = 128
S5_PAIRS_PER_ITER = 2
VMEM_LIMIT = 56 * 1024 * 1024
MASK_VALUE = -1e30
LOG2E = math.log2(math.e)
DEN_LANE = HEAD_DIM
N_EXTRA = 6
EXTRA_SLOT = 8
SCORE_LIMIT = 80.0
NORM_SLACK = 1.02
EXP2_ZERO_BELOW = -150.0

FIN_ROWS = 32

PROMPT_BLOCK = 512
ATTN_HEADS_PER_STEP = N_HEADS
ATTN_HEADS_PER_LOOP = 4
ADA_COLS = 1536
TABLE_GROUPS = 8

NT_DIMS = (((1,), (1,)), ((), ()))


def _sigmoid(x):
    return 1.0 / (1.0 + jnp.exp(-x))


def _split3(x):
    hi = x.astype(BF16)
    r = x - hi.astype(F32)
    mid = r.astype(BF16)
    lo = (r - mid.astype(F32)).astype(BF16)
    return hi, mid, lo


def _split2(x):
    hi = x.astype(BF16)
    return hi, (x - hi.astype(F32)).astype(BF16)


def _dot(a, b):
    return jnp.dot(a, b, preferred_element_type=F32)


def _dot_nt(a, b):
    return lax.dot_general(a, b, NT_DIMS, preferred_element_type=F32)


def _rms(x, g):
    ms = jnp.mean(x * x, axis=-1, keepdims=True)
    return x * lax.rsqrt(ms + EPS) * g


def _per_row(ref, nb, tb):
    m = ref[...]
    if nb == 1:
        return m.reshape(1, D_MODEL)
    return jnp.broadcast_to(m[:, None, :], (nb, tb, D_MODEL)).reshape(nb * tb, D_MODEL)


def _ada_kernel(c_ref, w_ref, b_ref, o_ref):
    c = c_ref[...]
    a_hi, a_lo = _split2(c * _sigmoid(c))
    w_hi, w_lo = _split2(w_ref[0])
    o_ref[0] = _dot(a_hi, w_hi) + _dot(a_lo, w_hi) + _dot(a_hi, w_lo) + b_ref[0]


def _ada_call(c_all, w_ada, b_ada):
    depth = w_ada.shape[0]
    nb = ADA_COLS
    n_out = w_ada.shape[2]
    return pl.pallas_call(
        _ada_kernel,
        grid=(depth, n_out // nb),
        in_specs=[
            pl.BlockSpec((16, D_MODEL), lambda l, j: (0, 0)),
            pl.BlockSpec((1, D_MODEL, nb), lambda l, j: (l, 0, j)),
            pl.BlockSpec((1, 1, nb), lambda l, j: (l, 0, j)),
        ],
        out_specs=pl.BlockSpec((1, 16, nb), lambda l, j: (l, 0, j)),
        out_shape=jax.ShapeDtypeStruct((depth, 16, n_out), F32),
        compiler_params=pltpu.CompilerParams(
            dimension_semantics=("arbitrary", "arbitrary"), vmem_limit_bytes=VMEM_LIMIT),
        name="ada",
    )(c_all, w_ada, b_ada.reshape(depth, 1, n_out))


def _inproj_kernel(x_ref, sh_ref, sc_ref, g_ref, wt_ref, bf_ref, tri_ref, sel_ref, one_ref, eye_ref,
                   go_ref, qa_ref, ka_ref, va_ref, kc_ref, vc_ref, lf_ref, u_ref, st_ref, carry_ref,
                   *, nb, tb, kv_transposed):
    rows = nb * tb

    @pl.when(pl.program_id(1) == 0)
    def _():
        carry_ref[...] = jnp.zeros_like(carry_ref)

    x = x_ref[...].reshape(rows, D_MODEL)
    hm = _rms(x, g_ref[...]) * (1.0 + _per_row(sc_ref, nb, tb)) + _per_row(sh_ref, nb, tb)
    proj = _dot_nt(hm.astype(BF16), wt_ref[...])
    q = proj[:, 0:D_ATT] * (LOG2E * HEAD_DIM ** -0.5)
    k = proj[:, D_ATT:2 * D_ATT]
    v = proj[:, 2 * D_ATT:3 * D_ATT]
    gate = proj[:, 3 * D_ATT:U_COL0] + bf_ref[...]
    logf = jnp.minimum(gate, 0.0) - jnp.log1p(jnp.exp(-jnp.abs(gate)))
    lane = lax.broadcasted_iota(jnp.int32, (rows, LANES), 1)

    def pack3(val):
        hi, mid, lo = (part.astype(F32) for part in _split3(val))
        packed = jnp.where(lane < 2 * N_HEADS, pltpu.roll(mid, N_HEADS, axis=1),
                           jnp.where(lane < 3 * N_HEADS, pltpu.roll(lo, 2 * N_HEADS, axis=1), 0.0))
        return jnp.where(lane < N_HEADS, hi, packed).astype(BF16)

    lf_pack = pack3(logf)
    cs = _dot(tri_ref[...], lf_pack)
    fcum = (cs + pltpu.roll(cs, LANES - N_HEADS, axis=1) + pltpu.roll(cs, LANES - 2 * N_HEADS, axis=1)
            + carry_ref[...])
    carry_ref[...] = fcum[rows - 1:rows, :]

    extra = _dot(pack3(fcum * LOG2E), sel_ref[...]) + one_ref[...]
    in_extra = (lane >= HEAD_DIM) & (lane < HEAD_DIM + N_EXTRA)
    qn2 = _dot((q * q).astype(BF16), go_ref[...])
    kn2 = _dot((k * k).astype(BF16), go_ref[...])
    st_ref[0, 0] = jnp.concatenate(
        [jnp.max(qn2, axis=0, keepdims=True), jnp.max(kn2, axis=0, keepdims=True),
         fcum[0:1, :], fcum[rows - 1:rows, :], jnp.zeros((4, LANES), F32)], axis=0)
    lo_half = lane < HEAD_DIM
    den_one = jnp.where(lane == DEN_LANE, 1.0, 0.0)
    if kv_transposed:
        kt, vt = k.T, v.T
    for h in range(N_HEADS):
        pair = slice(LANES * (h // 2), LANES * (h // 2) + LANES)
        q2, k2, v2 = q[:, pair], k[:, pair], v[:, pair]
        if h % 2:
            q2 = pltpu.roll(q2, HEAD_DIM, axis=1)
            k2 = pltpu.roll(k2, HEAD_DIM, axis=1)
            v2 = pltpu.roll(v2, HEAD_DIM, axis=1)
        per_stream = lambda val: val.reshape(nb, tb, val.shape[-1])
        to_extra = lambda side: jnp.where(in_extra, pltpu.roll(
            extra[:, LANES * side:LANES * (side + 1)], (HEAD_DIM - EXTRA_SLOT * h) % LANES, axis=1), 0.0)
        qa_ref[:, h] = per_stream(jnp.where(lo_half, q2, to_extra(0)).astype(BF16))
        ka_ref[:, h] = per_stream(jnp.where(lo_half, k2, to_extra(1)).astype(BF16))
        va_ref[:, h] = per_stream(jnp.where(lo_half, v2, den_one).astype(BF16))
        if kv_transposed:
            kc_ref[h] = kt[HEAD_DIM * h:HEAD_DIM * (h + 1), :]
            vc_ref[h] = vt[HEAD_DIM * h:HEAD_DIM * (h + 1), :]
        else:
            kc_ref[:, h] = per_stream(k2[:, 0:HEAD_DIM])
            vc_ref[:, h] = per_stream(v2[:, 0:HEAD_DIM])
    lf_t = _dot_nt(eye_ref[...], lf_pack)
    lf_ref[0] = lf_t[0:N_HEADS] + lf_t[N_HEADS:2 * N_HEADS] + lf_t[2 * N_HEADS:3 * N_HEADS]
    for j in range(N_U_SLABS):
        u_ref[0, j] = proj[:, U_COL0 + LANES * j:U_COL0 + LANES * (j + 1)]


def _inproj_consts(nb, tb):
    tri = np.kron(np.eye(nb, dtype=np.float32), np.tril(np.ones((tb, tb), np.float32)))
    sel = np.zeros((LANES, 2 * LANES), np.float32)
    one = np.zeros((1, 2 * LANES), np.float32)
    group_ones = np.zeros((D_ATT, LANES), np.float32)
    for h in range(N_HEADS):
        group_ones[HEAD_DIM * h:HEAD_DIM * (h + 1), h] = 1.0
        for p in range(3):
            sel[p * N_HEADS + h, EXTRA_SLOT * h + p] = 1.0
            one[0, EXTRA_SLOT * h + 3 + p] = 1.0
            one[0, LANES + EXTRA_SLOT * h + p] = 1.0
            sel[p * N_HEADS + h, LANES + EXTRA_SLOT * h + 3 + p] = -1.0
    eye = np.eye(4 * N_HEADS, LANES, dtype=np.float32)
    eye[3 * N_HEADS:] = 0.0
    return (jnp.asarray(tri, BF16), jnp.asarray(sel, BF16), jnp.asarray(one, F32), jnp.asarray(eye, BF16),
            jnp.asarray(group_ones, BF16))


def _mod_spec(layer, row0, chunk, nb=1):
    if nb == 1:
        return pl.BlockSpec((1, 1, D_MODEL), lambda b, i: ((layer * 16 + row0 + b) * 6 + chunk, 0, 0))
    return pl.BlockSpec((None, None, nb, D_MODEL), lambda b, i: (layer, chunk, 0, 0))


def _inproj_call(x, mods, layer, row0, g_pre, w_in_t, b_forget, kv_bufs, kv_depth, nb, tb):
    bsz, t, _ = x.shape
    nblk = t // tb
    rows = nb * tb
    assert nb == 1 or (nblk == 1 and kv_depth is None)
    tri, sel, one, eye, group_ones = _inproj_consts(nb, tb)
    if nb > 1:
        u_shape = (1, N_U_SLABS, bsz * t, LANES)
        u_spec = pl.BlockSpec((1, N_U_SLABS, rows, LANES), lambda b, i: (0, 0, b, 0))
        lf_shape = (1, N_HEADS, bsz * t)
        lf_spec = pl.BlockSpec((1, N_HEADS, rows), lambda b, i: (0, 0, b))
    else:
        u_shape = (bsz, N_U_SLABS, t, LANES)
        u_spec = pl.BlockSpec((1, N_U_SLABS, tb, LANES), lambda b, i: (b, 0, i, 0))
        lf_shape = (bsz, N_HEADS, t)
        lf_spec = pl.BlockSpec((1, N_HEADS, tb), lambda b, i: (b, 0, i))
    const = lambda b, i: (0, 0)
    in_specs = [
        pl.BlockSpec((nb, tb, D_MODEL), lambda b, i: (b, i, 0)),
        _mod_spec(layer, row0, 0, nb), _mod_spec(layer, row0, 1, nb),
        pl.BlockSpec((1, D_MODEL), const),
        pl.BlockSpec((None, W_IN_COLS, D_MODEL), lambda b, i: (layer, 0, 0)),
        pl.BlockSpec((1, GATE_PAD), const),
        pl.BlockSpec((rows, rows), const),
        pl.BlockSpec((LANES, 2 * LANES), const),
        pl.BlockSpec((1, 2 * LANES), const),
        pl.BlockSpec((4 * N_HEADS, LANES), const),
        pl.BlockSpec((D_ATT, LANES), const),
    ]
    args = [x, mods, mods, g_pre, w_in_t, b_forget, tri, sel, one, eye, group_ones]
    n_main = len(args)
    head_blk = pl.BlockSpec((nb, N_HEADS, tb, LANES), lambda b, i: (b, 0, i, 0))
    aliases = {}
    if kv_depth is None:
        kv_shape = jax.ShapeDtypeStruct((bsz, N_HEADS, t, HEAD_DIM), F32)
        kv_spec = pl.BlockSpec((nb, N_HEADS, tb, HEAD_DIM), lambda b, i: (b, 0, i, 0))
    else:
        kv_shape = jax.ShapeDtypeStruct((kv_depth, bsz, N_HEADS, HEAD_DIM, t), F32)
        kv_spec = pl.BlockSpec((None, None, N_HEADS, HEAD_DIM, tb), lambda b, i: (layer, b, 0, 0, i))
        if kv_bufs is not None:
            in_specs += [pl.BlockSpec(memory_space=pl.ANY), pl.BlockSpec(memory_space=pl.ANY)]
            args += list(kv_bufs)
            aliases = {n_main: 3, n_main + 1: 4}
    out_shape = [
        jax.ShapeDtypeStruct((bsz, N_HEADS, t, LANES), BF16),
        jax.ShapeDtypeStruct((bsz, N_HEADS, t, LANES), BF16),
        jax.ShapeDtypeStruct((bsz, N_HEADS, t, LANES), BF16),
        kv_shape, kv_shape,
        jax.ShapeDtypeStruct(lf_shape, F32),
        jax.ShapeDtypeStruct(u_shape, F32),
        jax.ShapeDtypeStruct((bsz // nb, nblk, 8, LANES), F32),
    ]
    out_specs = [
        head_blk, head_blk, head_blk,
        kv_spec, kv_spec,
        lf_spec,
        u_spec,
        pl.BlockSpec((1, 1, 8, LANES), lambda b, i: (b, i, 0, 0)),
    ]
    n_args = len(args)

    def body(*refs):
        _inproj_kernel(*refs[:n_main], *refs[n_args:], nb=nb, tb=tb, kv_transposed=kv_depth is not None)

    return pl.pallas_call(
        body,
        grid=(bsz // nb, nblk),
        in_specs=in_specs,
        out_specs=out_specs,
        out_shape=out_shape,
        scratch_shapes=[pltpu.VMEM((1, LANES), F32)],
        input_output_aliases=aliases,
        compiler_params=pltpu.CompilerParams(
            dimension_semantics=("arbitrary", "arbitrary"), vmem_limit_bytes=VMEM_LIMIT),
        name="inproj",
    )(*args)


def _finish_heads(accs, lane):
    outs = []
    for acc in accs:
        den = jnp.sum(jnp.where(lane == DEN_LANE, acc, 0.0), axis=1, keepdims=True)
        outs.append(acc * (1.0 / den))
    return jnp.where(lane < HEAD_DIM, outs[0], pltpu.roll(outs[1], HEAD_DIM, axis=1)).astype(BF16)


def _attn_kernel(small_ref, first_ref, qa_ref, ka_ref, va_ref, o_ref, *, tq, hps, lps):
    i = pl.program_id(2)
    row = lax.broadcasted_iota(jnp.int32, (tq, tq), 0)
    col = lax.broadcasted_iota(jnp.int32, (tq, tq), 1)
    causal = col <= row
    lane = lax.broadcasted_iota(jnp.int32, (tq, LANES), 1)
    qs = [qa_ref[0, hh] for hh in range(hps)]

    def score(j, hh, masked):
        start = pl.multiple_of(j * tq, tq)
        s = _dot_nt(qs[hh], ka_ref[0, hh, pl.ds(start, tq), :])
        return jnp.where(causal, s, MASK_VALUE) if masked else s

    def values(j, hh):
        return va_ref[0, hh, pl.ds(pl.multiple_of(j * tq, tq), tq), :]

    def store(accs):
        for pr in range(hps // 2):
            o_ref[0, :, LANES * pr:LANES * (pr + 1)] = _finish_heads(accs[2 * pr:2 * pr + 2], lane)

    def tile_plain(j, accs, heads):
        return tuple(acc + _dot(jnp.exp2(score(j, hh, False)).astype(BF16), values(j, hh))
                     for acc, hh in zip(accs, heads))

    def diagonal_plain(accs):
        half = tq // 2
        start = pl.multiple_of(i * tq, tq)
        upper = (lax.broadcasted_iota(jnp.int32, (half, half), 1) <=
                 lax.broadcasted_iota(jnp.int32, (half, half), 0))
        lower = (lax.broadcasted_iota(jnp.int32, (half, tq), 1) <=
                 lax.broadcasted_iota(jnp.int32, (half, tq), 0) + half)
        new = []
        for hh in range(hps):
            k_t = ka_ref[0, hh, pl.ds(start, tq), :]
            v_t = va_ref[0, hh, pl.ds(start, tq), :]
            s_top = jnp.where(upper, _dot_nt(qs[hh][0:half], k_t[0:half]), MASK_VALUE)
            s_bot = jnp.where(lower, _dot_nt(qs[hh][half:tq], k_t), MASK_VALUE)
            o_top = _dot(jnp.exp2(s_top).astype(BF16), v_t[0:half])
            o_bot = _dot(jnp.exp2(s_bot).astype(BF16), v_t)
            new.append(accs[hh] + jnp.concatenate([o_top, o_bot], axis=0))
        return new

    def tile_shifted(j, carry, masked):
        new = []
        for hh in range(hps):
            m, acc = carry[hh]
            s = score(j, hh, masked)
            m_new = jnp.maximum(m, jnp.max(s, axis=1, keepdims=True))
            p = jnp.exp2(s - m_new)
            new.append((m_new, jnp.exp2(m - m_new) * acc + _dot(p.astype(BF16), values(j, hh))))
        return tuple(new)

    small = small_ref[pl.program_id(0), pl.program_id(1)] != 0
    zero = jnp.zeros((tq, LANES), F32)

    @pl.when(small)
    def _():
        accs = []
        for sub in range(hps // lps):
            heads = tuple(range(sub * lps, (sub + 1) * lps))
            first = first_ref[pl.program_id(0), pl.program_id(1) * (hps // lps) + sub, i]
            accs += lax.fori_loop(
                0, i - first,
                lambda it, a, first=first, heads=heads: tile_plain(first + it, a, heads), (zero,) * lps)
        store(diagonal_plain(accs))

    @pl.when(jnp.logical_not(small))
    def _():
        init = ((jnp.full((tq, 1), MASK_VALUE, F32), zero),) * hps
        carry = lax.fori_loop(0, i, functools.partial(tile_shifted, masked=False), init)
        store([c[1] for c in tile_shifted(i, carry, True)])


def _attn_call(qa, ka, va, stats, tq, hps, lps):
    bsz, _, t, _ = qa.shape
    nt, ng, nl = t // tq, N_HEADS // hps, N_HEADS // lps
    assert stats.shape[1] == nt
    bound = jnp.sqrt(jnp.max(stats[:, :, 0, :N_HEADS], axis=1) *
                     jnp.max(stats[:, :, 1, :N_HEADS], axis=1)) * NORM_SLACK
    small = jnp.max(bound.reshape(bsz, ng, hps), axis=-1) < SCORE_LIMIT
    f_first, f_last = stats[:, :, 2, :N_HEADS] * LOG2E, stats[:, :, 3, :N_HEADS] * LOG2E
    top = bound[:, None, None, :] + f_first[:, :, None, :] - f_last[:, None, :, :]
    needed = jnp.max((top >= EXP2_ZERO_BELOW).astype(jnp.int32).reshape(bsz, nt, nt, nl, lps), axis=-1)
    first = jnp.where(jnp.repeat(small, nl // ng, axis=1)[:, :, None],
                      jnp.sum(1 - needed, axis=2).transpose(0, 2, 1), 0)
    whole = pl.BlockSpec((1, hps, t, LANES), lambda b, p, i, *_: (b, p, 0, 0))
    return pl.pallas_call(
        functools.partial(_attn_kernel, tq=tq, hps=hps, lps=lps),
        grid_spec=pltpu.PrefetchScalarGridSpec(
            num_scalar_prefetch=2,
            grid=(bsz, ng, nt),
            in_specs=[pl.BlockSpec((1, hps, tq, LANES), lambda b, p, i, *_: (b, p, i, 0)), whole, whole],
            out_specs=pl.BlockSpec((1, tq, HEAD_DIM * hps), lambda b, p, i, *_: (b, i, p)),
        ),
        out_shape=jax.ShapeDtypeStruct((bsz, t, D_ATT), BF16),
        compiler_params=pltpu.CompilerParams(
            dimension_semantics=("arbitrary", "arbitrary", "arbitrary"),
            vmem_limit_bytes=VMEM_LIMIT),
        name="fox_attn",
    )(small.astype(jnp.int32), first.astype(jnp.int32), qa, ka, va)


def _attn_sample_kernel(qa_ref, ka_ref, va_ref, ckt_ref, cvt_ref, clf_ref, o_ref, *, t, n_past):
    lane_p = lax.broadcasted_iota(jnp.int32, (N_HEADS, n_past), 1)
    c = clf_ref[...]
    total = jnp.sum(c, axis=1, keepdims=True)
    d = 1
    while d < n_past:
        c = c + jnp.where(lane_p >= d, pltpu.roll(c, d, axis=1), 0.0)
        d *= 2
    f_past = (c - total) * LOG2E
    lane = lax.broadcasted_iota(jnp.int32, (t, LANES), 1)
    causal = (lax.broadcasted_iota(jnp.int32, (t, t), 1) <= lax.broadcasted_iota(jnp.int32, (t, t), 0))
    den_rows = jnp.where(lax.broadcasted_iota(jnp.int32, (LANES - HEAD_DIM, n_past), 0) == 0, 1.0, 0.0)
    heads = range(N_HEADS)
    qs = [qa_ref[0, h] for h in heads]
    fqs = [jnp.sum(jnp.where((lane >= HEAD_DIM) & (lane < HEAD_DIM + 3), q.astype(F32), 0.0),
                   axis=1, keepdims=True) for q in qs]
    s_past = [_dot(qs[h][:, 0:HEAD_DIM], ckt_ref[h].astype(BF16)) + fqs[h] - f_past[h:h + 1, :]
              for h in heads]
    s_new = [jnp.where(causal, _dot_nt(qs[h], ka_ref[0, h]), MASK_VALUE) for h in heads]
    ms = [jnp.maximum(jnp.max(s_past[h], axis=1, keepdims=True), jnp.max(s_new[h], axis=1, keepdims=True))
          for h in heads]
    p_past = [jnp.exp2(s_past[h] - ms[h]).astype(BF16) for h in heads]
    p_new = [jnp.exp2(s_new[h] - ms[h]).astype(BF16) for h in heads]
    cvt_aug = [jnp.concatenate([cvt_ref[h], den_rows], axis=0).astype(BF16) for h in heads]
    accs = [_dot_nt(p_past[h], cvt_aug[h]) + _dot(p_new[h], va_ref[0, h]) for h in heads]
    for pr in range(N_HEADS // 2):
        o_ref[0, :, LANES * pr:LANES * (pr + 1)] = _finish_heads(accs[2 * pr:2 * pr + 2], lane)


def _attn_sample_call(qa, ka, va, cache_kt, cache_vt, cache_logf, layer):
    bsz, _, t, _ = qa.shape
    n_past = cache_kt.shape[4]
    cache_spec = pl.BlockSpec((None, None, N_HEADS, HEAD_DIM, n_past), lambda b: (layer, b, 0, 0, 0))
    new_spec = pl.BlockSpec((1, N_HEADS, t, LANES), lambda b: (b, 0, 0, 0))
    return pl.pallas_call(
        functools.partial(_attn_sample_kernel, t=t, n_past=n_past),
        grid=(bsz,),
        in_specs=[
            new_spec, new_spec, new_spec, cache_spec, cache_spec,
            pl.BlockSpec((None, None, N_HEADS, n_past), lambda b: (layer, b, 0, 0)),
        ],
        out_specs=pl.BlockSpec((1, t, D_ATT), lambda b: (b, 0, 0)),
        out_shape=jax.ShapeDtypeStruct((bsz, t, D_ATT), BF16),
        compiler_params=pltpu.CompilerParams(
            dimension_semantics=("arbitrary",), vmem_limit_bytes=VMEM_LIMIT),
        name="fox_attn_sample",
    )(qa, ka, va, cache_kt, cache_vt, cache_logf)


def _s5_table_kernel(ca_ref, cb_ref, ba_ref, bb_ref, prr_ref, pii_ref, dv_ref,
                     toep_ref, bpow_ref, cpow_ref, tt_scr, *, gb):
    row = lax.broadcasted_iota(jnp.int32, (SSM_GROUP, S5_ROWS), 0)
    lane = lax.broadcasted_iota(jnp.int32, (SSM_GROUP, S5_ROWS), 1)
    for gi in range(gb):
        ca, cb, ba, bb = ca_ref[gi], cb_ref[gi], ba_ref[gi], bb_ref[gi]
        power = lambda a, b, n: a * prr_ref[gi, n:n + 1, :] + b * pii_ref[gi, n:n + 1, :]
        xs = [power(ca, cb, n) for n in range(S5_CHUNK + 1)]
        cpow_ref[gi] = jnp.concatenate(xs[1:], axis=0).astype(BF16)
        bpow_ref[gi] = jnp.concatenate(
            [power(ba, bb, S5_CHUNK - 1 - s) for s in range(S5_CHUNK)], axis=0).T.astype(BF16)
        x_hi, x_lo = _split2(jnp.concatenate(xs[:S5_CHUNK], axis=0))
        b_hi, b_lo = _split2(ba)
        r0 = _dot_nt(b_hi, x_hi) + _dot_nt(b_lo, x_hi) + _dot_nt(b_hi, x_lo)
        r0 = r0 + jnp.where(lane == row, dv_ref[gi], 0.0)
        for s in range(S5_CHUNK):
            blk = r0
            if s:
                blk = jnp.where(lane >= SSM_GROUP * s, pltpu.roll(r0, SSM_GROUP * s, axis=1), 0.0)
            tt_scr[SSM_GROUP * s:SSM_GROUP * (s + 1), :] = blk
        toep_ref[gi] = tt_scr[...].T.astype(BF16)


def _s5_tables(a_re, a_im, log_dt, b_re_t, b_im_t, c_re, c_im, d_skip):
    depth = a_re.shape[0]
    dt = jnp.exp(log_dt)[..., None]
    x, y = a_re * dt, a_im * dt
    ex, cy, sy = jnp.exp(x), jnp.cos(y), jnp.sin(y)
    ar, ai = ex * cy, ex * sy
    sh = jnp.sin(0.5 * y)
    nr, ni = jnp.expm1(x) * cy - 2.0 * sh * sh, ai
    den = a_re * a_re + a_im * a_im
    fr = ((nr * a_re + ni * a_im) / den)[:, :, None, :]
    fi = ((ni * a_re - nr * a_im) / den)[:, :, None, :]
    bbr, bbi = fr * b_re_t - fi * b_im_t, fr * b_im_t + fi * b_re_t
    pr, pi = [jnp.ones_like(ar)], [jnp.zeros_like(ar)]
    for _ in range(S5_CHUNK):
        pr, pi = pr + [pr[-1] * ar - pi[-1] * ai], pi + [pr[-1] * ai + pi[-1] * ar]
    mr, mi = pr[S5_CHUNK], pi[S5_CHUNK]
    pr, pi = jnp.stack(pr, axis=2), jnp.stack(pi, axis=2)
    cat = lambda a, b: jnp.concatenate([a, b], axis=-1)
    ins = [cat(c_re, -c_im), cat(-c_im, -c_re), cat(bbr, bbi), cat(-bbi, bbr), cat(pr, pr), cat(pi, pi),
           jnp.pad(d_skip.reshape(depth, N_GROUPS, 1, SSM_GROUP), ((0, 0), (0, 0), (0, 0), (0, S5_ROWS - SSM_GROUP)))]
    gb = TABLE_GROUPS
    spec = lambda rows, cols: pl.BlockSpec((None, gb, rows, cols), lambda l, j: (l, j, 0, 0))
    tab = lambda rows, cols: jax.ShapeDtypeStruct((depth, N_GROUPS, rows, cols), BF16)
    toep, bpow, cpow = pl.pallas_call(
        functools.partial(_s5_table_kernel, gb=gb),
        grid=(depth, N_GROUPS // gb),
        in_specs=[spec(SSM_GROUP, LANES)] * 4 + [spec(S5_CHUNK + 1, LANES)] * 2 + [spec(1, S5_ROWS)],
        out_specs=[spec(S5_ROWS, S5_ROWS), spec(LANES, S5_ROWS), spec(S5_ROWS, LANES)],
        out_shape=[tab(S5_ROWS, S5_ROWS), tab(LANES, S5_ROWS), tab(S5_ROWS, LANES)],
        scratch_shapes=[pltpu.VMEM((S5_ROWS, S5_ROWS), F32)],
        compiler_params=pltpu.CompilerParams(
            dimension_semantics=("arbitrary", "arbitrary"), vmem_limit_bytes=VMEM_LIMIT),
        name="s5_tables",
    )(*ins)
    rows_r, rows_i = [], []
    for _ in range(8):
        rows_r.append(cat(mr, mr))
        rows_i.append(cat(-mi, mi))
        mr, mi = mr * mr - mi * mi, 2.0 * mr * mi
    scm = jnp.stack(rows_r + rows_i, axis=2)
    return toep, bpow, cpow, scm


def _gelu_tanh(y):
    return 0.5 * y * (1.0 + jnp.tanh(math.sqrt(2.0 / math.pi) * (y + 0.044715 * (y * y * y))))


def _s5_kernel(*refs, n, n_real, seglen, fin_row0, use_h0):
    if use_h0:
        u_ref, toep_ref, bpow_ref, cpow_ref, scm_ref, wglu_ref, h0_ref = refs[:7]
        o_ref, hfin_ref, z_scr, zt_scr, h_scr, h0_scr = refs[7:]
    else:
        u_ref, toep_ref, bpow_ref, cpow_ref, scm_ref, wglu_ref = refs[:6]
        o_ref, hfin_ref, z_scr, zt_scr, h_scr = refs[6:]
    gps = LANES // SSM_GROUP
    for pp in range(S5_PAIRS_PER_ITER):
        for part in range(2):
            h_scr[pp, part, 0:SCAN_PAD, :] = jnp.zeros((SCAN_PAD, LANES), F32)
    if use_h0:
        h0_scr[...] = jnp.zeros(h0_scr.shape, F32)

    for k in range(S5_CHUNK):
        for j in range(N_U_SLABS):
            vv = u_ref[0, j, pl.ds(k, n_real, stride=S5_CHUNK), :]
            if n_real < n:
                vv = jnp.concatenate([vv, jnp.zeros((n - n_real, LANES), F32)], axis=0)
            z_scr[gps * j:gps * (j + 1), SSM_GROUP * k:SSM_GROUP * (k + 1), :] = (
                vv.T.reshape(gps, SSM_GROUP, n).astype(BF16))

    chunk_in_seg = lax.broadcasted_iota(jnp.int32, (n, LANES), 0) % seglen
    lo = lax.broadcasted_iota(jnp.int32, (n, LANES), 1) < SSM_STATE
    lo_row = lax.broadcasted_iota(jnp.int32, (1, LANES), 1) < SSM_STATE
    lo_fin = lax.broadcasted_iota(jnp.int32, (FIN_ROWS, LANES), 1) < SSM_STATE
    swap = lambda a: pltpu.roll(a, SSM_STATE, axis=1)

    def pair_step(it, _):
        pairs = range(S5_PAIRS_PER_ITER)
        gss = [[2 * (it * S5_PAIRS_PER_ITER + pp), 2 * (it * S5_PAIRS_PER_ITER + pp) + 1] for pp in pairs]
        flat = [g for gs in gss for g in gs]
        ucts = [z_scr[g] for g in flat]
        yts = [_dot(toep_ref[g], u) for g, u in zip(flat, ucts)]
        hs = [_dot(bpow_ref[g], u).T for g, u in zip(flat, ucts)]
        h0s = [0.0] * len(flat)
        if use_h0:
            for idx, g in enumerate(flat):
                h0_scr[idx, pl.ds(0, h0_ref.shape[1], stride=seglen), :] = h0_ref[g]
                h0s[idx] = h0_scr[idx]
                hs[idx] = hs[idx] + scm_ref[g, 0:1, :] * h0s[idx] + scm_ref[g, 8:9, :] * swap(h0s[idx])
        hre = [jnp.where(lo, hs[2 * pp], swap(hs[2 * pp + 1])) for pp in pairs]
        him = [jnp.where(lo, swap(hs[2 * pp]), hs[2 * pp + 1]) for pp in pairs]

        def shifted(by):
            for pp in pairs:
                h_scr[pp, 0, SCAN_PAD:SCAN_PAD + n, :] = hre[pp]
                h_scr[pp, 1, SCAN_PAD:SCAN_PAD + n, :] = him[pp]
            return ([h_scr[pp, 0, pl.ds(SCAN_PAD - by, n), :] for pp in pairs],
                    [h_scr[pp, 1, pl.ds(SCAN_PAD - by, n), :] for pp in pairs])

        d, j = 1, 0
        while d < seglen:
            sre, sim = shifted(d)
            for pp in pairs:
                gs = gss[pp]
                a, b = sre[pp], sim[pp]
                if seglen < n:
                    a = jnp.where(chunk_in_seg >= d, a, 0.0)
                    b = jnp.where(chunk_in_seg >= d, b, 0.0)
                ar = jnp.where(lo_row, scm_ref[gs[0], j:j + 1, :], scm_ref[gs[1], j:j + 1, :])
                ai = jnp.where(lo_row, -scm_ref[gs[0], 8 + j:9 + j, :], scm_ref[gs[1], 8 + j:9 + j, :])
                hre[pp], him[pp] = hre[pp] + ar * a - ai * b, him[pp] + ar * b + ai * a
            d, j = 2 * d, j + 1
        pre, pim = shifted(1)
        for pp in pairs:
            hps = [jnp.where(lo, pre[pp], swap(pim[pp])), jnp.where(lo, swap(pre[pp]), pim[pp])]
            fre = hre[pp][fin_row0:fin_row0 + FIN_ROWS, :]
            fim = him[pp][fin_row0:fin_row0 + FIN_ROWS, :]
            fins = [jnp.where(lo_fin, fre, swap(fim)), jnp.where(lo_fin, swap(fre), fim)]
            for idx, g in enumerate(gss[pp]):
                hp = hps[idx]
                if seglen < n:
                    hp = jnp.where(chunk_in_seg >= 1, hp, h0s[2 * pp + idx])
                zt_scr[g] = _gelu_tanh(yts[2 * pp + idx] + _dot_nt(cpow_ref[g], hp.astype(BF16)))
                hfin_ref[0, g] = fins[idx]
        return 0

    lax.fori_loop(0, N_GROUPS // (2 * S5_PAIRS_PER_ITER), pair_step, 0)

    for k in range(S5_CHUNK):
        slabs = []
        for j in range(N_U_SLABS):
            w = zt_scr[gps * j:gps * (j + 1), SSM_GROUP * k:SSM_GROUP * (k + 1), :]
            slabs.append(w.reshape(LANES, n).T)
        zk = jnp.concatenate(slabs, axis=1)
        out = zk * _sigmoid(_dot(zk.astype(BF16), wglu_ref[...]))
        for j in range(N_U_SLABS):
            o_ref[0, j, pl.ds(k, n_real, stride=S5_CHUNK), :] = out[0:n_real, LANES * j:LANES * (j + 1)]


def _s5_call(u, toep, bpow, cpow, scm, wglu, h0, layer, n, n_real, seglen, fin_row0):
    nb, _, tt, _ = u.shape
    use_h0 = h0 is not None
    one = pl.Buffered(1)
    tab = lambda rows, cols: pl.BlockSpec((None, N_GROUPS, rows, cols), lambda b: (layer, 0, 0, 0),
                                          pipeline_mode=one)
    in_specs = [
        pl.BlockSpec((1, N_U_SLABS, tt, LANES), lambda b: (b, 0, 0, 0)),
        tab(S5_ROWS, S5_ROWS), tab(LANES, S5_ROWS), tab(S5_ROWS, LANES), tab(16, LANES),
        pl.BlockSpec((None, D_SSM, D_SSM), lambda b: (layer, 0, 0), pipeline_mode=one),
    ]
    args = [u, toep, bpow, cpow, scm, wglu]
    scratch = [
        pltpu.VMEM((N_GROUPS, S5_ROWS, n), BF16),
        pltpu.VMEM((N_GROUPS, S5_ROWS, n), F32),
        pltpu.VMEM((S5_PAIRS_PER_ITER, 2, SCAN_PAD + n, LANES), F32),
    ]
    if use_h0:
        in_specs.append(tab(h0.shape[2], LANES))
        args.append(h0)
        scratch.append(pltpu.VMEM((2 * S5_PAIRS_PER_ITER, n, LANES), F32))
    return pl.pallas_call(
        functools.partial(_s5_kernel, n=n, n_real=n_real, seglen=seglen, fin_row0=fin_row0,
                          use_h0=use_h0),
        grid=(nb,),
        in_specs=in_specs,
        out_specs=[
            pl.BlockSpec((1, N_U_SLABS, tt, LANES), lambda b: (b, 0, 0, 0)),
            pl.BlockSpec((1, N_GROUPS, FIN_ROWS, LANES), lambda b: (b, 0, 0, 0)),
        ],
        out_shape=[
            jax.ShapeDtypeStruct((nb, N_U_SLABS, tt, LANES), F32),
            jax.ShapeDtypeStruct((nb, N_GROUPS, FIN_ROWS, LANES), F32),
        ],
        scratch_shapes=scratch,
        compiler_params=pltpu.CompilerParams(
            dimension_semantics=("arbitrary",), vmem_limit_bytes=VMEM_LIMIT),
        name="s5_mix",
    )(*args)


def _out_ffn_kernel(x_ref, att_ref, ssm_ref, g1_ref, sh2_ref, sc2_ref, g2_ref,
                    gpm_ref, gpf_ref, gpo_ref, wo_ref, wg_ref, wu_ref, wd_ref, o_ref, *, nb, tb):
    rows = nb * tb
    mix = jnp.concatenate(
        [att_ref[...].reshape(rows, D_ATT)] + [ssm_ref[0, j].astype(BF16) for j in range(N_U_SLABS)],
        axis=1)
    o = _dot(mix, wo_ref[...])
    x1 = x_ref[...].reshape(rows, D_MODEL) + _per_row(g1_ref, nb, tb) * _rms(o, gpm_ref[...])
    hf = (_rms(x1, gpf_ref[...]) * (1.0 + _per_row(sc2_ref, nb, tb)) + _per_row(sh2_ref, nb, tb)).astype(BF16)
    gt = _dot(hf, wg_ref[...])
    up = _dot(hf, wu_ref[...])
    hid = (gt * _sigmoid(gt) * up).astype(BF16)
    f = _dot(hid, wd_ref[...])
    o_ref[...] = (x1 + _per_row(g2_ref, nb, tb) * _rms(f, gpo_ref[...])).reshape(nb, tb, D_MODEL)


def _out_ffn_call(x, att, ssm, mods, layer, row0, g_post_mix, g_pre_ffn, g_post_ffn,
                  w_out, w_gate, w_up, w_down, nb, tb):
    bsz, t, _ = x.shape
    nblk = t // tb
    one = pl.Buffered(1)
    const = lambda b, i: (0, 0)
    lay = lambda b, i: (layer, 0, 0)
    if nb > 1:
        ssm_spec = pl.BlockSpec((1, N_U_SLABS, nb * tb, LANES), lambda b, i: (0, 0, b, 0))
    else:
        ssm_spec = pl.BlockSpec((1, N_U_SLABS, tb, LANES), lambda b, i: (b, 0, i, 0))
    row = pl.BlockSpec((1, D_MODEL), const)
    return pl.pallas_call(
        functools.partial(_out_ffn_kernel, nb=nb, tb=tb),
        grid=(bsz // nb, nblk),
        in_specs=[
            pl.BlockSpec((nb, tb, D_MODEL), lambda b, i: (b, i, 0)),
            pl.BlockSpec((nb, tb, D_ATT), lambda b, i: (b, i, 0)),
            ssm_spec,
            _mod_spec(layer, row0, 2, nb), _mod_spec(layer, row0, 3, nb), _mod_spec(layer, row0, 4, nb),
            _mod_spec(layer, row0, 5, nb),
            row, row, row,
            pl.BlockSpec((None, D_MODEL, D_MODEL), lay, pipeline_mode=one),
            pl.BlockSpec((None, D_MODEL, D_FF), lay, pipeline_mode=one),
            pl.BlockSpec((None, D_MODEL, D_FF), lay, pipeline_mode=one),
            pl.BlockSpec((None, D_FF, D_MODEL), lay, pipeline_mode=one),
        ],
        out_specs=pl.BlockSpec((nb, tb, D_MODEL), lambda b, i: (b, i, 0)),
        out_shape=jax.ShapeDtypeStruct((bsz, t, D_MODEL), F32),
        compiler_params=pltpu.CompilerParams(
            dimension_semantics=("arbitrary", "arbitrary"), vmem_limit_bytes=VMEM_LIMIT),
        name="out_ffn",
    )(x, att, ssm, mods, mods, mods, mods, g_post_mix, g_pre_ffn, g_post_ffn,
      w_out, w_gate, w_up, w_down)


def kernel(x_prompt, x_sample, c_prompt, c_sample, cache_k, cache_v, cache_logf, state_ssm_re,
           state_ssm_im, w_ada, b_ada, g_pre_mix, g_post_mix, g_pre_ffn, g_post_ffn, w_in, b_forget,
           ssm_a_re, ssm_a_im, ssm_log_dt, ssm_b_re, ssm_b_im, ssm_c_re, ssm_c_im, ssm_d, w_glu,
           w_out, w_gate, w_up, w_down):
    depth = w_in.shape[0]
    bp, tp, _ = x_prompt.shape
    bs, ts, _ = x_sample.shape
    n_att = 3 * D_ATT + N_HEADS

    c_all = jnp.concatenate([c_prompt, c_sample, jnp.zeros((16 - bp - bs, D_MODEL), F32)], axis=0)
    mod = _ada_call(c_all, w_ada, b_ada)
    modr = mod.reshape(depth * 16 * 6, 1, D_MODEL)
    mod_s = mod[:, bp:bp + bs].reshape(depth, bs, 6, D_MODEL).transpose(0, 2, 1, 3)

    w_in_t = jnp.swapaxes(w_in, 1, 2)
    w_in_t = jnp.concatenate(
        [w_in_t[:, :n_att], jnp.zeros((depth, GATE_PAD - N_HEADS, D_MODEL), F32), w_in_t[:, n_att:]],
        axis=1).astype(BF16)
    bf_p = jnp.pad(b_forget, ((0, 0), (0, GATE_PAD - N_HEADS)))
    w_glu_b, w_out_b = w_glu.astype(BF16), w_out.astype(BF16)
    w_gate_b, w_up_b, w_down_b = w_gate.astype(BF16), w_up.astype(BF16), w_down.astype(BF16)
    toep, bpow, cpow, scm = _s5_tables(ssm_a_re, ssm_a_im, ssm_log_dt, jnp.swapaxes(ssm_b_re, 2, 3),
                                       jnp.swapaxes(ssm_b_im, 2, 3), ssm_c_re, ssm_c_im, ssm_d)
    cache_kt, cache_vt = jnp.swapaxes(cache_k, 3, 4), jnp.swapaxes(cache_v, 3, 4)
    h0 = jnp.concatenate([state_ssm_re, state_ssm_im], axis=-1).transpose(0, 2, 1, 3)

    n_chunk_p = tp // S5_CHUNK
    seg_s = ts // S5_CHUNK
    yp, ys = x_prompt, x_sample
    kvp = None
    lfp, ssp, kss, vss, lfs, sss = [], [], [], [], [], []
    for l in range(depth):
        qa, ka, va, kbuf, vbuf, lf, u, stats = _inproj_call(
            yp, modr, l, 0, g_pre_mix[l:l + 1], w_in_t, bf_p[l:l + 1], kvp, depth, 1, PROMPT_BLOCK)
        kvp = (kbuf, vbuf)
        lfp.append(lf)
        att = _attn_call(qa, ka, va, stats, PROMPT_BLOCK, ATTN_HEADS_PER_STEP, ATTN_HEADS_PER_LOOP)
        ssm, hfin = _s5_call(u, toep, bpow, cpow, scm, w_glu_b, None, l, n_chunk_p, n_chunk_p,
                             n_chunk_p, n_chunk_p - FIN_ROWS)
        ssp.append(hfin[:, :, FIN_ROWS - 1, :])
        yp = _out_ffn_call(yp, att, ssm, modr, l, 0, g_post_mix[l:l + 1], g_pre_ffn[l:l + 1],
                           g_post_ffn[l:l + 1], w_out_b, w_gate_b, w_up_b, w_down_b, 1, PROMPT_BLOCK)

        qa, ka, va, k_s, v_s, lf, u, _ = _inproj_call(
            ys, mod_s, l, bp, g_pre_mix[l:l + 1], w_in_t, bf_p[l:l + 1], None, None, bs, ts)
        kss.append(k_s)
        vss.append(v_s)
        lfs.append(lf.reshape(N_HEADS, bs, ts).transpose(1, 0, 2))
        att = _attn_sample_call(qa, ka, va, cache_kt, cache_vt, cache_logf, l)
        ssm, hfin = _s5_call(u, toep, bpow, cpow, scm, w_glu_b, h0, l, LANES, bs * seg_s, seg_s, 0)
        sss.append(hfin[0, :, seg_s - 1:bs * seg_s:seg_s, :].transpose(1, 0, 2))
        ys = _out_ffn_call(ys, att, ssm, mod_s, l, bp, g_post_mix[l:l + 1], g_pre_ffn[l:l + 1],
                           g_post_ffn[l:l + 1], w_out_b, w_gate_b, w_up_b, w_down_b, bs, ts)

    ssp, sss = jnp.stack(ssp), jnp.stack(sss)
    return (yp, ys, jnp.swapaxes(kvp[0], 3, 4), jnp.swapaxes(kvp[1], 3, 4), jnp.stack(lfp),
            ssp[..., :SSM_STATE], ssp[..., SSM_STATE:],
            jnp.stack(kss), jnp.stack(vss), jnp.stack(lfs),
            sss[..., :SSM_STATE], sss[..., SSM_STATE:])
```

```python
import functools
import math

import numpy as np
import jax
import jax.numpy as jnp
from jax import lax
from jax.experimental import pallas as pl
from jax.experimental.pallas import tpu as pltpu

F32 = jnp.float32
BF16 = jnp.bfloat16

D_MODEL = 1024
N_HEADS = 8
HEAD_DIM = 64
D_ATT = N_HEADS * HEAD_DIM
D_SSM = D_MODEL - D_ATT
SSM_GROUP = 16
N_GROUPS = D_SSM // SSM_GROUP
SSM_STATE = 64
D_FF = 2816
EPS = 1e-6

LANES = 128
S5_CHUNK = 16
S5_ROWS = S5_CHUNK * SSM_GROUP
GATE_PAD = LANES
W_IN_COLS = 3 * D_ATT + GATE_PAD + D_SSM
U_COL0 = 3 * D_ATT + GATE_PAD
N_U_SLABS = D_SSM // LANES
SCAN_PAD = 128
S5_PAIRS_PER_ITER = 2
VMEM_LIMIT = 56 * 1024 * 1024
MASK_VALUE = -1e30
LOG2E = math.log2(math.e)
DEN_LANE = HEAD_DIM
N_EXTRA = 6
EXTRA_SLOT = 8
SCORE_LIMIT = 80.0
NORM_SLACK = 1.02
EXP2_ZERO_BELOW = -150.0

FIN_ROWS = 32

PROMPT_BLOCK = 512
ATTN_HEADS_PER_STEP = N_HEADS
ATTN_HEADS_PER_LOOP = 4
ADA_COLS = 1536
TABLE_GROUPS = 8

NT_DIMS = (((1,), (1,)), ((), ()))


def _sigmoid(x):
    return 1.0 / (1.0 + jnp.exp(-x))


def _split3(x):
    hi = x.astype(BF16)
    r = x - hi.astype(F32)
    mid = r.astype(BF16)
    lo = (r - mid.astype(F32)).astype(BF16)
    return hi, mid, lo


def _split2(x):
    hi = x.astype(BF16)
    return hi, (x - hi.astype(F32)).astype(BF16)


def _dot(a, b):
    return jnp.dot(a, b, preferred_element_type=F32)


def _dot_nt(a, b):
    return lax.dot_general(a, b, NT_DIMS, preferred_element_type=F32)


def _rms(x, g):
    ms = jnp.mean(x * x, axis=-1, keepdims=True)
    return x * lax.rsqrt(ms + EPS) * g


def _per_row(ref, nb, tb):
    m = ref[...]
    if nb == 1:
        return m.reshape(1, D_MODEL)
    return jnp.broadcast_to(m[:, None, :], (nb, tb, D_MODEL)).reshape(nb * tb, D_MODEL)


def _ada_kernel(c_ref, w_ref, b_ref, o_ref):
    c = c_ref[...]
    a_hi, a_lo = _split2(c * _sigmoid(c))
    w_hi, w_lo = _split2(w_ref[0])
    o_ref[0] = _dot(a_hi, w_hi) + _dot(a_lo, w_hi) + _dot(a_hi, w_lo) + b_ref[0]


def _ada_call(c_all, w_ada, b_ada):
    depth = w_ada.shape[0]
    nb = ADA_COLS
    n_out = w_ada.shape[2]
    return pl.pallas_call(
        _ada_kernel,
        grid=(depth, n_out // nb),
        in_specs=[
            pl.BlockSpec((16, D_MODEL), lambda l, j: (0, 0)),
            pl.BlockSpec((1, D_MODEL, nb), lambda l, j: (l, 0, j)),
            pl.BlockSpec((1, 1, nb), lambda l, j: (l, 0, j)),
        ],
        out_specs=pl.BlockSpec((1, 16, nb), lambda l, j: (l, 0, j)),
        out_shape=jax.ShapeDtypeStruct((depth, 16, n_out), F32),
        compiler_params=pltpu.CompilerParams(
            dimension_semantics=("arbitrary", "arbitrary"), vmem_limit_bytes=VMEM_LIMIT),
        name="ada",
    )(c_all, w_ada, b_ada.reshape(depth, 1, n_out))


def _inproj_kernel(x_ref, sh_ref, sc_ref, g_ref, wt_ref, bf_ref, tri_ref, sel_ref, one_ref, eye_ref,
                   go_ref, qa_ref, ka_ref, va_ref, kc_ref, vc_ref, lf_ref, u_ref, st_ref, carry_ref,
                   *, nb, tb, kv_transposed):
    rows = nb * tb

    @pl.when(pl.program_id(1) == 0)
    def _():
        carry_ref[...] = jnp.zeros_like(carry_ref)

    x = x_ref[...].reshape(rows, D_MODEL)
    hm = _rms(x, g_ref[...]) * (1.0 + _per_row(sc_ref, nb, tb)) + _per_row(sh_ref, nb, tb)
    proj = _dot_nt(hm.astype(BF16), wt_ref[...])
    q = proj[:, 0:D_ATT] * (LOG2E * HEAD_DIM ** -0.5)
    k = proj[:, D_ATT:2 * D_ATT]
    v = proj[:, 2 * D_ATT:3 * D_ATT]
    gate = proj[:, 3 * D_ATT:U_COL0] + bf_ref[...]
    logf = jnp.minimum(gate, 0.0) - jnp.log1p(jnp.exp(-jnp.abs(gate)))
    lane = lax.broadcasted_iota(jnp.int32, (rows, LANES), 1)

    def pack3(val):
        hi, mid, lo = (part.astype(F32) for part in _split3(val))
        packed = jnp.where(lane < 2 * N_HEADS, pltpu.roll(mid, N_HEADS, axis=1),
                           jnp.where(lane < 3 * N_HEADS, pltpu.roll(lo, 2 * N_HEADS, axis=1), 0.0))
        return jnp.where(lane < N_HEADS, hi, packed).astype(BF16)

    lf_pack = pack3(logf)
    cs = _dot(tri_ref[...], lf_pack)
    fcum = (cs + pltpu.roll(cs, LANES - N_HEADS, axis=1) + pltpu.roll(cs, LANES - 2 * N_HEADS, axis=1)
            + carry_ref[...])
    carry_ref[...] = fcum[rows - 1:rows, :]

    extra = _dot(pack3(fcum * LOG2E), sel_ref[...]) + one_ref[...]
    in_extra = (lane >= HEAD_DIM) & (lane < HEAD_DIM + N_EXTRA)
    qn2 = _dot((q * q).astype(BF16), go_ref[...])
    kn2 = _dot((k * k).astype(BF16), go_ref[...])
    st_ref[0, 0] = jnp.concatenate(
        [jnp.max(qn2, axis=0, keepdims=True), jnp.max(kn2, axis=0, keepdims=True),
         fcum[0:1, :], fcum[rows - 1:rows, :], fcum[rows // 2:rows // 2 + 1, :],
         jnp.zeros((3, LANES), F32)], axis=0)
    lo_half = lane < HEAD_DIM
    den_one = jnp.where(lane == DEN_LANE, 1.0, 0.0)
    if kv_transposed:
        kt, vt = k.T, v.T
    for h in range(N_HEADS):
        pair = slice(LANES * (h // 2), LANES * (h // 2) + LANES)
        q2, k2, v2 = q[:, pair], k[:, pair], v[:, pair]
        if h % 2:
            q2 = pltpu.roll(q2, HEAD_DIM, axis=1)
            k2 = pltpu.roll(k2, HEAD_DIM, axis=1)
            v2 = pltpu.roll(v2, HEAD_DIM, axis=1)
        per_stream = lambda val: val.reshape(nb, tb, val.shape[-1])
        to_extra = lambda side: jnp.where(in_extra, pltpu.roll(
            extra[:, LANES * side:LANES * (side + 1)], (HEAD_DIM - EXTRA_SLOT * h) % LANES, axis=1), 0.0)
        qa_ref[:, h] = per_stream(jnp.where(lo_half, q2, to_extra(0)).astype(BF16))
        ka_ref[:, h] = per_stream(jnp.where(lo_half, k2, to_extra(1)).astype(BF16))
        va_ref[:, h] = per_stream(jnp.where(lo_half, v2, den_one).astype(BF16))
        if kv_transposed:
            kc_ref[h] = kt[HEAD_DIM * h:HEAD_DIM * (h + 1), :]
            vc_ref[h] = vt[HEAD_DIM * h:HEAD_DIM * (h + 1), :]
        else:
            kc_ref[:, h] = per_stream(k2[:, 0:HEAD_DIM])
            vc_ref[:, h] = per_stream(v2[:, 0:HEAD_DIM])
    lf_t = _dot_nt(eye_ref[...], lf_pack)
    lf_ref[0] = lf_t[0:N_HEADS] + lf_t[N_HEADS:2 * N_HEADS] + lf_t[2 * N_HEADS:3 * N_HEADS]
    for j in range(N_U_SLABS):
        u_ref[0, j] = proj[:, U_COL0 + LANES * j:U_COL0 + LANES * (j + 1)]


def _inproj_consts(nb, tb):
    tri = np.kron(np.eye(nb, dtype=np.float32), np.tril(np.ones((tb, tb), np.float32)))
    sel = np.zeros((LANES, 2 * LANES), np.float32)
    one = np.zeros((1, 2 * LANES), np.float32)
    group_ones = np.zeros((D_ATT, LANES), np.float32)
    for h in range(N_HEADS):
        group_ones[HEAD_DIM * h:HEAD_DIM * (h + 1), h] = 1.0
        for p in range(3):
            sel[p * N_HEADS + h, EXTRA_SLOT * h + p] = 1.0
            one[0, EXTRA_SLOT * h + 3 + p] = 1.0
            one[0, LANES + EXTRA_SLOT * h + p] = 1.0
            sel[p * N_HEADS + h, LANES + EXTRA_SLOT * h + 3 + p] = -1.0
    eye = np.eye(4 * N_HEADS, LANES, dtype=np.float32)
    eye[3 * N_HEADS:] = 0.0
    return (jnp.asarray(tri, BF16), jnp.asarray(sel, BF16), jnp.asarray(one, F32), jnp.asarray(eye, BF16),
            jnp.asarray(group_ones, BF16))


def _mod_spec(layer, row0, chunk, nb=1):
    if nb == 1:
        return pl.BlockSpec((1, 1, D_MODEL), lambda b, i: ((layer * 16 + row0 + b) * 6 + chunk, 0, 0))
    return pl.BlockSpec((None, None, nb, D_MODEL), lambda b, i: (layer, chunk, 0, 0))


def _inproj_call(x, mods, layer, row0, g_pre, w_in_t, b_forget, kv_bufs, kv_depth, nb, tb):
    bsz, t, _ = x.shape
    nblk = t // tb
    rows = nb * tb
    assert nb == 1 or (nblk == 1 and kv_depth is None)
    tri, sel, one, eye, group_ones = _inproj_consts(nb, tb)
    if nb > 1:
        u_shape = (1, N_U_SLABS, bsz * t, LANES)
        u_spec = pl.BlockSpec((1, N_U_SLABS, rows, LANES), lambda b, i: (0, 0, b, 0))
        lf_shape = (1, N_HEADS, bsz * t)
        lf_spec = pl.BlockSpec((1, N_HEADS, rows), lambda b, i: (0, 0, b))
    else:
        u_shape = (bsz, N_U_SLABS, t, LANES)
        u_spec = pl.BlockSpec((1, N_U_SLABS, tb, LANES), lambda b, i: (b, 0, i, 0))
        lf_shape = (bsz, N_HEADS, t)
        lf_spec = pl.BlockSpec((1, N_HEADS, tb), lambda b, i: (b, 0, i))
    const = lambda b, i: (0, 0)
    in_specs = [
        pl.BlockSpec((nb, tb, D_MODEL), lambda b, i: (b, i, 0)),
        _mod_spec(layer, row0, 0, nb), _mod_spec(layer, row0, 1, nb),
        pl.BlockSpec((1, D_MODEL), const),
        pl.BlockSpec((None, W_IN_COLS, D_MODEL), lambda b, i: (layer, 0, 0)),
        pl.BlockSpec((1, GATE_PAD), const),
        pl.BlockSpec((rows, rows), const),
        pl.BlockSpec((LANES, 2 * LANES), const),
        pl.BlockSpec((1, 2 * LANES), const),
        pl.BlockSpec((4 * N_HEADS, LANES), const),
        pl.BlockSpec((D_ATT, LANES), const),
    ]
    args = [x, mods, mods, g_pre, w_in_t, b_forget, tri, sel, one, eye, group_ones]
    n_main = len(args)
    head_blk = pl.BlockSpec((nb, N_HEADS, tb, LANES), lambda b, i: (b, 0, i, 0))
    aliases = {}
    if kv_depth is None:
        kv_shape = jax.ShapeDtypeStruct((bsz, N_HEADS, t, HEAD_DIM), F32)
        kv_spec = pl.BlockSpec((nb, N_HEADS, tb, HEAD_DIM), lambda b, i: (b, 0, i, 0))
    else:
        kv_shape = jax.ShapeDtypeStruct((kv_depth, bsz, N_HEADS, HEAD_DIM, t), F32)
        kv_spec = pl.BlockSpec((None, None, N_HEADS, HEAD_DIM, tb), lambda b, i: (layer, b, 0, 0, i))
        if kv_bufs is not None:
            in_specs += [pl.BlockSpec(memory_space=pl.ANY), pl.BlockSpec(memory_space=pl.ANY)]
            args += list(kv_bufs)
            aliases = {n_main: 3, n_main + 1: 4}
    out_shape = [
        jax.ShapeDtypeStruct((bsz, N_HEADS, t, LANES), BF16),
        jax.ShapeDtypeStruct((bsz, N_HEADS, t, LANES), BF16),
        jax.ShapeDtypeStruct((bsz, N_HEADS, t, LANES), BF16),
        kv_shape, kv_shape,
        jax.ShapeDtypeStruct(lf_shape, F32),
        jax.ShapeDtypeStruct(u_shape, F32),
        jax.ShapeDtypeStruct((bsz // nb, nblk, 8, LANES), F32),
    ]
    out_specs = [
        head_blk, head_blk, head_blk,
        kv_spec, kv_spec,
        lf_spec,
        u_spec,
        pl.BlockSpec((1, 1, 8, LANES), lambda b, i: (b, i, 0, 0)),
    ]
    n_args = len(args)

    def body(*refs):
        _inproj_kernel(*refs[:n_main], *refs[n_args:], nb=nb, tb=tb, kv_transposed=kv_depth is not None)

    return pl.pallas_call(
        body,
        grid=(bsz // nb, nblk),
        in_specs=in_specs,
        out_specs=out_specs,
        out_shape=out_shape,
        scratch_shapes=[pltpu.VMEM((1, LANES), F32)],
        input_output_aliases=aliases,
        compiler_params=pltpu.CompilerParams(
            dimension_semantics=("arbitrary", "arbitrary"), vmem_limit_bytes=VMEM_LIMIT),
        name="inproj",
    )(*args)


def _finish_heads(accs, lane):
    outs = []
    for acc in accs:
        den = jnp.sum(jnp.where(lane == DEN_LANE, acc, 0.0), axis=1, keepdims=True)
        outs.append(acc * (1.0 / den))
    return jnp.where(lane < HEAD_DIM, outs[0], pltpu.roll(outs[1], HEAD_DIM, axis=1)).astype(BF16)


def _attn_kernel(small_ref, first_ref, first_all_ref, qa_ref, ka_ref, va_ref, o_ref, *, tq, hps, lps):
    i = pl.program_id(2)
    row = lax.broadcasted_iota(jnp.int32, (tq, tq), 0)
    col = lax.broadcasted_iota(jnp.int32, (tq, tq), 1)
    causal = col <= row
    lane = lax.broadcasted_iota(jnp.int32, (tq, LANES), 1)
    qs = [qa_ref[0, hh] for hh in range(hps)]

    def score(j, hh, masked):
        start = pl.multiple_of(j * tq, tq)
        s = _dot_nt(qs[hh], ka_ref[0, hh, pl.ds(start, tq), :])
        return jnp.where(causal, s, MASK_VALUE) if masked else s

    def values(j, hh):
        return va_ref[0, hh, pl.ds(pl.multiple_of(j * tq, tq), tq), :]

    def store(accs):
        for pr in range(hps // 2):
            o_ref[0, :, LANES * pr:LANES * (pr + 1)] = _finish_heads(accs[2 * pr:2 * pr + 2], lane)

    def tile_plain(j, accs, heads):
        return tuple(acc + _dot(jnp.exp2(score(j, hh, False)).astype(BF16), values(j, hh))
                     for acc, hh in zip(accs, heads))

    def tile_upper(j, accs, heads):
        half = tq // 2
        start = pl.multiple_of(j * tq, tq)
        new = []
        for acc, hh in zip(accs, heads):
            s = _dot_nt(qs[hh][0:half], ka_ref[0, hh, pl.ds(start, tq), :])
            o = _dot(jnp.exp2(s).astype(BF16), values(j, hh))
            new.append(acc + jnp.concatenate([o, jnp.zeros((tq - half, LANES), F32)], axis=0))
        return tuple(new)

    def diagonal_plain(accs):
        half = tq // 2
        start = pl.multiple_of(i * tq, tq)
        upper = (lax.broadcasted_iota(jnp.int32, (half, half), 1) <=
                 lax.broadcasted_iota(jnp.int32, (half, half), 0))
        lower = (lax.broadcasted_iota(jnp.int32, (half, tq), 1) <=
                 lax.broadcasted_iota(jnp.int32, (half, tq), 0) + half)
        new = []
        for hh in range(hps):
            k_t = ka_ref[0, hh, pl.ds(start, tq), :]
            v_t = va_ref[0, hh, pl.ds(start, tq), :]
            s_top = jnp.where(upper, _dot_nt(qs[hh][0:half], k_t[0:half]), MASK_VALUE)
            s_bot = jnp.where(lower, _dot_nt(qs[hh][half:tq], k_t), MASK_VALUE)
            o_top = _dot(jnp.exp2(s_top).astype(BF16), v_t[0:half])
            o_bot = _dot(jnp.exp2(s_bot).astype(BF16), v_t)
            new.append(accs[hh] + jnp.concatenate([o_top, o_bot], axis=0))
        return new

    def tile_shifted(j, carry, masked):
        new = []
        for hh in range(hps):
            m, acc = carry[hh]
            s = score(j, hh, masked)
            m_new = jnp.maximum(m, jnp.max(s, axis=1, keepdims=True))
            p = jnp.exp2(s - m_new)
            new.append((m_new, jnp.exp2(m - m_new) * acc + _dot(p.astype(BF16), values(j, hh))))
        return tuple(new)

    small = small_ref[pl.program_id(0), pl.program_id(1)] != 0
    zero = jnp.zeros((tq, LANES), F32)

    @pl.when(small)
    def _():
        accs = []
        for sub in range(hps // lps):
            heads = tuple(range(sub * lps, (sub + 1) * lps))
            where = (pl.program_id(0), pl.program_id(1) * (hps // lps) + sub, i)
            first, first_all = first_ref[where], first_all_ref[where]
            part = lax.fori_loop(
                0, first_all - first,
                lambda it, a, first=first, heads=heads: tile_upper(first + it, a, heads), (zero,) * lps)
            accs += lax.fori_loop(
                0, i - first_all,
                lambda it, a, first_all=first_all, heads=heads: tile_plain(first_all + it, a, heads), part)
        store(diagonal_plain(accs))

    @pl.when(jnp.logical_not(small))
    def _():
        init = ((jnp.full((tq, 1), MASK_VALUE, F32), zero),) * hps
        carry = lax.fori_loop(0, i, functools.partial(tile_shifted, masked=False), init)
        store([c[1] for c in tile_shifted(i, carry, True)])


def _attn_call(qa, ka, va, stats, tq, hps, lps):
    bsz, _, t, _ = qa.shape
    nt, ng, nl = t // tq, N_HEADS // hps, N_HEADS // lps
    assert stats.shape[1] == nt
    bound = jnp.sqrt(jnp.max(stats[:, :, 0, :N_HEADS], axis=1) *
                     jnp.max(stats[:, :, 1, :N_HEADS], axis=1)) * NORM_SLACK
    small = jnp.max(bound.reshape(bsz, ng, hps), axis=-1) < SCORE_LIMIT
    f_first, f_last, f_mid = (stats[:, :, r, :N_HEADS] * LOG2E for r in (2, 3, 4))

    def first_needed(f_query):
        top = bound[:, None, None, :] + f_query[:, :, None, :] - f_last[:, None, :, :]
        needed = jnp.max((top >= EXP2_ZERO_BELOW).astype(jnp.int32).reshape(bsz, nt, nt, nl, lps), axis=-1)
        return jnp.where(jnp.repeat(small, nl // ng, axis=1)[:, :, None],
                         jnp.sum(1 - needed, axis=2).transpose(0, 2, 1), 0).astype(jnp.int32)

    first, first_all = first_needed(f_first), first_needed(f_mid)
    whole = pl.BlockSpec((1, hps, t, LANES), lambda b, p, i, *_: (b, p, 0, 0))
    return pl.pallas_call(
        functools.partial(_attn_kernel, tq=tq, hps=hps, lps=lps),
        grid_spec=pltpu.PrefetchScalarGridSpec(
            num_scalar_prefetch=3,
            grid=(bsz, ng, nt),
            in_specs=[pl.BlockSpec((1, hps, tq, LANES), lambda b, p, i, *_: (b, p, i, 0)), whole, whole],
            out_specs=pl.BlockSpec((1, tq, HEAD_DIM * hps), lambda b, p, i, *_: (b, i, p)),
        ),
        out_shape=jax.ShapeDtypeStruct((bsz, t, D_ATT), BF16),
        compiler_params=pltpu.CompilerParams(
            dimension_semantics=("arbitrary", "arbitrary", "arbitrary"),
            vmem_limit_bytes=VMEM_LIMIT),
        name="fox_attn",
    )(small.astype(jnp.int32), first, first_all, qa, ka, va)


def _attn_sample_kernel(qa_ref, ka_ref, va_ref, ckt_ref, cvt_ref, clf_ref, o_ref, *, t, n_past):
    lane_p = lax.broadcasted_iota(jnp.int32, (N_HEADS, n_past), 1)
    c = clf_ref[...]
    total = jnp.sum(c, axis=1, keepdims=True)
    d = 1
    while d < n_past:
        c = c + jnp.where(lane_p >= d, pltpu.roll(c, d, axis=1), 0.0)
        d *= 2
    f_past = (c - total) * LOG2E
    lane = lax.broadcasted_iota(jnp.int32, (t, LANES), 1)
    causal = (lax.broadcasted_iota(jnp.int32, (t, t), 1) <= lax.broadcasted_iota(jnp.int32, (t, t), 0))
    den_rows = jnp.where(lax.broadcasted_iota(jnp.int32, (LANES - HEAD_DIM, n_past), 0) == 0, 1.0, 0.0)
    heads = range(N_HEADS)
    qs = [qa_ref[0, h] for h in heads]
    fqs = [jnp.sum(jnp.where((lane >= HEAD_DIM) & (lane < HEAD_DIM + 3), q.astype(F32), 0.0),
                   axis=1, keepdims=True) for q in qs]
    s_past = [_dot(qs[h][:, 0:HEAD_DIM], ckt_ref[h].astype(BF16)) + fqs[h] - f_past[h:h + 1, :]
              for h in heads]
    s_new = [jnp.where(causal, _dot_nt(qs[h], ka_ref[0, h]), MASK_VALUE) for h in heads]
    ms = [jnp.maximum(jnp.max(s_past[h], axis=1, keepdims=True), jnp.max(s_new[h], axis=1, keepdims=True))
          for h in heads]
    p_past = [jnp.exp2(s_past[h] - ms[h]).astype(BF16) for h in heads]
    p_new = [jnp.exp2(s_new[h] - ms[h]).astype(BF16) for h in heads]
    cvt_aug = [jnp.concatenate([cvt_ref[h], den_rows], axis=0).astype(BF16) for h in heads]
    accs = [_dot_nt(p_past[h], cvt_aug[h]) + _dot(p_new[h], va_ref[0, h]) for h in heads]
    for pr in range(N_HEADS // 2):
        o_ref[0, :, LANES * pr:LANES * (pr + 1)] = _finish_heads(accs[2 * pr:2 * pr + 2], lane)


def _attn_sample_call(qa, ka, va, cache_kt, cache_vt, cache_logf, layer):
    bsz, _, t, _ = qa.shape
    n_past = cache_kt.shape[4]
    cache_spec = pl.BlockSpec((None, None, N_HEADS, HEAD_DIM, n_past), lambda b: (layer, b, 0, 0, 0))
    new_spec = pl.BlockSpec((1, N_HEADS, t, LANES), lambda b: (b, 0, 0, 0))
    return pl.pallas_call(
        functools.partial(_attn_sample_kernel, t=t, n_past=n_past),
        grid=(bsz,),
        in_specs=[
            new_spec, new_spec, new_spec, cache_spec, cache_spec,
            pl.BlockSpec((None, None, N_HEADS, n_past), lambda b: (layer, b, 0, 0)),
        ],
        out_specs=pl.BlockSpec((1, t, D_ATT), lambda b: (b, 0, 0)),
        out_shape=jax.ShapeDtypeStruct((bsz, t, D_ATT), BF16),
        compiler_params=pltpu.CompilerParams(
            dimension_semantics=("arbitrary",), vmem_limit_bytes=VMEM_LIMIT),
        name="fox_attn_sample",
    )(qa, ka, va, cache_kt, cache_vt, cache_logf)


def _s5_table_kernel(ca_ref, cb_ref, ba_ref, bb_ref, prr_ref, pii_ref, dv_ref,
                     toep_ref, bpow_ref, cpow_ref, tt_scr, *, gb):
    row = lax.broadcasted_iota(jnp.int32, (SSM_GROUP, S5_ROWS), 0)
    lane = lax.broadcasted_iota(jnp.int32, (SSM_GROUP, S5_ROWS), 1)
    for gi in range(gb):
        ca, cb, ba, bb = ca_ref[gi], cb_ref[gi], ba_ref[gi], bb_ref[gi]
        power = lambda a, b, n: a * prr_ref[gi, n:n + 1, :] + b * pii_ref[gi, n:n + 1, :]
        xs = [power(ca, cb, n) for n in range(S5_CHUNK + 1)]
        cpow_ref[gi] = jnp.concatenate(xs[1:], axis=0).astype(BF16)
        bpow_ref[gi] = jnp.concatenate(
            [power(ba, bb, S5_CHUNK - 1 - s) for s in range(S5_CHUNK)], axis=0).T.astype(BF16)
        x_hi, x_lo = _split2(jnp.concatenate(xs[:S5_CHUNK], axis=0))
        b_hi, b_lo = _split2(ba)
        r0 = _dot_nt(b_hi, x_hi) + _dot_nt(b_lo, x_hi) + _dot_nt(b_hi, x_lo)
        r0 = r0 + jnp.where(lane == row, dv_ref[gi], 0.0)
        for s in range(S5_CHUNK):
            blk = r0
            if s:
                blk = jnp.where(lane >= SSM_GROUP * s, pltpu.roll(r0, SSM_GROUP * s, axis=1), 0.0)
            tt_scr[SSM_GROUP * s:SSM_GROUP * (s + 1), :] = blk
        toep_ref[gi] = tt_scr[...].T.astype(BF16)


def _s5_tables(a_re, a_im, log_dt, b_re_t, b_im_t, c_re, c_im, d_skip):
    depth = a_re.shape[0]
    dt = jnp.exp(log_dt)[..., None]
    x, y = a_re * dt, a_im * dt
    ex, cy, sy = jnp.exp(x), jnp.cos(y), jnp.sin(y)
    ar, ai = ex * cy, ex * sy
    sh = jnp.sin(0.5 * y)
    nr, ni = jnp.expm1(x) * cy - 2.0 * sh * sh, ai
    den = a_re * a_re + a_im * a_im
    fr = ((nr * a_re + ni * a_im) / den)[:, :, None, :]
    fi = ((ni * a_re - nr * a_im) / den)[:, :, None, :]
    bbr, bbi = fr * b_re_t - fi * b_im_t, fr * b_im_t + fi * b_re_t
    pr, pi = [jnp.ones_like(ar)], [jnp.zeros_like(ar)]
    for _ in range(S5_CHUNK):
        pr, pi = pr + [pr[-1] * ar - pi[-1] * ai], pi + [pr[-1] * ai + pi[-1] * ar]
    mr, mi = pr[S5_CHUNK], pi[S5_CHUNK]
    pr, pi = jnp.stack(pr, axis=2), jnp.stack(pi, axis=2)
    cat = lambda a, b: jnp.concatenate([a, b], axis=-1)
    ins = [cat(c_re, -c_im), cat(-c_im, -c_re), cat(bbr, bbi), cat(-bbi, bbr), cat(pr, pr), cat(pi, pi),
           jnp.pad(d_skip.reshape(depth, N_GROUPS, 1, SSM_GROUP), ((0, 0), (0, 0), (0, 0), (0, S5_ROWS - SSM_GROUP)))]
    gb = TABLE_GROUPS
    spec = lambda rows, cols: pl.BlockSpec((None, gb, rows, cols), lambda l, j: (l, j, 0, 0))
    tab = lambda rows, cols: jax.ShapeDtypeStruct((depth, N_GROUPS, rows, cols), BF16)
    toep, bpow, cpow = pl.pallas_call(
        functools.partial(_s5_table_kernel, gb=gb),
        grid=(depth, N_GROUPS // gb),
        in_specs=[spec(SSM_GROUP, LANES)] * 4 + [spec(S5_CHUNK + 1, LANES)] * 2 + [spec(1, S5_ROWS)],
        out_specs=[spec(S5_ROWS, S5_ROWS), spec(LANES, S5_ROWS), spec(S5_ROWS, LANES)],
        out_shape=[tab(S5_ROWS, S5_ROWS), tab(LANES, S5_ROWS), tab(S5_ROWS, LANES)],
        scratch_shapes=[pltpu.VMEM((S5_ROWS, S5_ROWS), F32)],
        compiler_params=pltpu.CompilerParams(
            dimension_semantics=("arbitrary", "arbitrary"), vmem_limit_bytes=VMEM_LIMIT),
        name="s5_tables",
    )(*ins)
    rows_r, rows_i = [], []
    for _ in range(8):
        rows_r.append(cat(mr, mr))
        rows_i.append(cat(-mi, mi))
        mr, mi = mr * mr - mi * mi, 2.0 * mr * mi
    scm = jnp.stack(rows_r + rows_i, axis=2)
    return toep, bpow, cpow, scm


def _gelu_tanh(y):
    return 0.5 * y * (1.0 + jnp.tanh(math.sqrt(2.0 / math.pi) * (y + 0.044715 * (y * y * y))))


def _s5_kernel(*refs, n, n_real, seglen, fin_row0, use_h0):
    if use_h0:
        u_ref, toep_ref, bpow_ref, cpow_ref, scm_ref, wglu_ref, h0_ref = refs[:7]
        o_ref, hfin_ref, z_scr, zt_scr, h_scr, h0_scr = refs[7:]
    else:
        u_ref, toep_ref, bpow_ref, cpow_ref, scm_ref, wglu_ref = refs[:6]
        o_ref, hfin_ref, z_scr, zt_scr, h_scr = refs[6:]
    gps = LANES // SSM_GROUP
    for pp in range(S5_PAIRS_PER_ITER):
        for part in range(2):
            h_scr[pp, part, 0:SCAN_PAD, :] = jnp.zeros((SCAN_PAD, LANES), F32)
    if use_h0:
        h0_scr[...] = jnp.zeros(h0_scr.shape, F32)

    for k in range(S5_CHUNK):
        for j in range(N_U_SLABS):
            vv = u_ref[0, j, pl.ds(k, n_real, stride=S5_CHUNK), :]
            if n_real < n:
                vv = jnp.concatenate([vv, jnp.zeros((n - n_real, LANES), F32)], axis=0)
            z_scr[gps * j:gps * (j + 1), SSM_GROUP * k:SSM_GROUP * (k + 1), :] = (
                vv.T.reshape(gps, SSM_GROUP, n).astype(BF16))

    chunk_in_seg = lax.broadcasted_iota(jnp.int32, (n, LANES), 0) % seglen
    lo = lax.broadcasted_iota(jnp.int32, (n, LANES), 1) < SSM_STATE
    lo_row = lax.broadcasted_iota(jnp.int32, (1, LANES), 1) < SSM_STATE
    lo_fin = lax.broadcasted_iota(jnp.int32, (FIN_ROWS, LANES), 1) < SSM_STATE
    swap = lambda a: pltpu.roll(a, SSM_STATE, axis=1)

    def pair_step(it, _):
        pairs = range(S5_PAIRS_PER_ITER)
        gss = [[2 * (it * S5_PAIRS_PER_ITER + pp), 2 * (it * S5_PAIRS_PER_ITER + pp) + 1] for pp in pairs]
        flat = [g for gs in gss for g in gs]
        ucts = [z_scr[g] for g in flat]
        yts = [_dot(toep_ref[g], u) for g, u in zip(flat, ucts)]
        hs = [_dot(bpow_ref[g], u).T for g, u in zip(flat, ucts)]
        h0s = [0.0] * len(flat)
        if use_h0:
            for idx, g in enumerate(flat):
                h0_scr[idx, pl.ds(0, h0_ref.shape[1], stride=seglen), :] = h0_ref[g]
                h0s[idx] = h0_scr[idx]
                hs[idx] = hs[idx] + scm_ref[g, 0:1, :] * h0s[idx] + scm_ref[g, 8:9, :] * swap(h0s[idx])
        hre = [jnp.where(lo, hs[2 * pp], swap(hs[2 * pp + 1])) for pp in pairs]
        him = [jnp.where(lo, swap(hs[2 * pp]), hs[2 * pp + 1]) for pp in pairs]

        def shifted(by):
            for pp in pairs:
                h_scr[pp, 0, SCAN_PAD:SCAN_PAD + n, :] = hre[pp]
                h_scr[pp, 1, SCAN_PAD:SCAN_PAD + n, :] = him[pp]
            return ([h_scr[pp, 0, pl.ds(SCAN_PAD - by, n), :] for pp in pairs],
                    [h_scr[pp, 1, pl.ds(SCAN_PAD - by, n), :] for pp in pairs])

        d, j = 1, 0
        while d < seglen:
            sre, sim = shifted(d)
            for pp in pairs:
                gs = gss[pp]
                a, b = sre[pp], sim[pp]
                if seglen < n:
                    a = jnp.where(chunk_in_seg >= d, a, 0.0)
                    b = jnp.where(chunk_in_seg >= d, b, 0.0)
                ar = jnp.where(lo_row, scm_ref[gs[0], j:j + 1, :], scm_ref[gs[1], j:j + 1, :])
                ai = jnp.where(lo_row, -scm_ref[gs[0], 8 + j:9 + j, :], scm_ref[gs[1], 8 + j:9 + j, :])
                hre[pp], him[pp] = hre[pp] + ar * a - ai * b, him[pp] + ar * b + ai * a
            d, j = 2 * d, j + 1
        pre, pim = shifted(1)
        for pp in pairs:
            hps = [jnp.where(lo, pre[pp], swap(pim[pp])), jnp.where(lo, swap(pre[pp]), pim[pp])]
            fre = hre[pp][fin_row0:fin_row0 + FIN_ROWS, :]
            fim = him[pp][fin_row0:fin_row0 + FIN_ROWS, :]
            fins = [jnp.where(lo_fin, fre, swap(fim)), jnp.where(lo_fin, swap(fre), fim)]
            for idx, g in enumerate(gss[pp]):
                hp = hps[idx]
                if seglen < n:
                    hp = jnp.where(chunk_in_seg >= 1, hp, h0s[2 * pp + idx])
                zt_scr[g] = _gelu_tanh(yts[2 * pp + idx] + _dot_nt(cpow_ref[g], hp.astype(BF16)))
                hfin_ref[0, g] = fins[idx]
        return 0

    lax.fori_loop(0, N_GROUPS // (2 * S5_PAIRS_PER_ITER), pair_step, 0)

    for k in range(S5_CHUNK):
        slabs = []
        for j in range(N_U_SLABS):
            w = zt_scr[gps * j:gps * (j + 1), SSM_GROUP * k:SSM_GROUP * (k + 1), :]
            slabs.append(w.reshape(LANES, n).T)
        zk = jnp.concatenate(slabs, axis=1)
        out = zk * _sigmoid(_dot(zk.astype(BF16), wglu_ref[...]))
        for j in range(N_U_SLABS):
            o_ref[0, j, pl.ds(k, n_real, stride=S5_CHUNK), :] = out[0:n_real, LANES * j:LANES * (j + 1)]


def _s5_call(u, toep, bpow, cpow, scm, wglu, h0, layer, n, n_real, seglen, fin_row0):
    nb, _, tt, _ = u.shape
    use_h0 = h0 is not None
    one = pl.Buffered(1)
    tab = lambda rows, cols: pl.BlockSpec((None, N_GROUPS, rows, cols), lambda b: (layer, 0, 0, 0),
                                          pipeline_mode=one)
    in_specs = [
        pl.BlockSpec((1, N_U_SLABS, tt, LANES), lambda b: (b, 0, 0, 0)),
        tab(S5_ROWS, S5_ROWS), tab(LANES, S5_ROWS), tab(S5_ROWS, LANES), tab(16, LANES),
        pl.BlockSpec((None, D_SSM, D_SSM), lambda b: (layer, 0, 0), pipeline_mode=one),
    ]
    args = [u, toep, bpow, cpow, scm, wglu]
    scratch = [
        pltpu.VMEM((N_GROUPS, S5_ROWS, n), BF16),
        pltpu.VMEM((N_GROUPS, S5_ROWS, n), F32),
        pltpu.VMEM((S5_PAIRS_PER_ITER, 2, SCAN_PAD + n, LANES), F32),
    ]
    if use_h0:
        in_specs.append(tab(h0.shape[2], LANES))
        args.append(h0)
        scratch.append(pltpu.VMEM((2 * S5_PAIRS_PER_ITER, n, LANES), F32))
    return pl.pallas_call(
        functools.partial(_s5_kernel, n=n, n_real=n_real, seglen=seglen, fin_row0=fin_row0,
                          use_h0=use_h0),
        grid=(nb,),
        in_specs=in_specs,
        out_specs=[
            pl.BlockSpec((1, N_U_SLABS, tt, LANES), lambda b: (b, 0, 0, 0)),
            pl.BlockSpec((1, N_GROUPS, FIN_ROWS, LANES), lambda b: (b, 0, 0, 0)),
        ],
        out_shape=[
            jax.ShapeDtypeStruct((nb, N_U_SLABS, tt, LANES), F32),
            jax.ShapeDtypeStruct((nb, N_GROUPS, FIN_ROWS, LANES), F32),
        ],
        scratch_shapes=scratch,
        compiler_params=pltpu.CompilerParams(
            dimension_semantics=("arbitrary",), vmem_limit_bytes=VMEM_LIMIT),
        name="s5_mix",
    )(*args)


def _out_ffn_kernel(x_ref, att_ref, ssm_ref, g1_ref, sh2_ref, sc2_ref, g2_ref,
                    gpm_ref, gpf_ref, gpo_ref, wo_ref, wg_ref, wu_ref, wd_ref, o_ref, *, nb, tb):
    rows = nb * tb
    mix = jnp.concatenate(
        [att_ref[...].reshape(rows, D_ATT)] + [ssm_ref[0, j].astype(BF16) for j in range(N_U_SLABS)],
        axis=1)
    o = _dot(mix, wo_ref[...])
    x1 = x_ref[...].reshape(rows, D_MODEL) + _per_row(g1_ref, nb, tb) * _rms(o, gpm_ref[...])
    hf = (_rms(x1, gpf_ref[...]) * (1.0 + _per_row(sc2_ref, nb, tb)) + _per_row(sh2_ref, nb, tb)).astype(BF16)
    gt = _dot(hf, wg_ref[...])
    up = _dot(hf, wu_ref[...])
    hid = (gt * _sigmoid(gt) * up).astype(BF16)
    f = _dot(hid, wd_ref[...])
    o_ref[...] = (x1 + _per_row(g2_ref, nb, tb) * _rms(f, gpo_ref[...])).reshape(nb, tb, D_MODEL)


def _out_ffn_call(x, att, ssm, mods, layer, row0, g_post_mix, g_pre_ffn, g_post_ffn,
                  w_out, w_gate, w_up, w_down, nb, tb):
    bsz, t, _ = x.shape
    nblk = t // tb
    one = pl.Buffered(1)
    const = lambda b, i: (0, 0)
    lay = lambda b, i: (layer, 0, 0)
    if nb > 1:
        ssm_spec = pl.BlockSpec((1, N_U_SLABS, nb * tb, LANES), lambda b, i: (0, 0, b, 0))
    else:
        ssm_spec = pl.BlockSpec((1, N_U_SLABS, tb, LANES), lambda b, i: (b, 0, i, 0))
    row = pl.BlockSpec((1, D_MODEL), const)
    return pl.pallas_call(
        functools.partial(_out_ffn_kernel, nb=nb, tb=tb),
        grid=(bsz // nb, nblk),
        in_specs=[
            pl.BlockSpec((nb, tb, D_MODEL), lambda b, i: (b, i, 0)),
            pl.BlockSpec((nb, tb, D_ATT), lambda b, i: (b, i, 0)),
            ssm_spec,
            _mod_spec(layer, row0, 2, nb), _mod_spec(layer, row0, 3, nb), _mod_spec(layer, row0, 4, nb),
            _mod_spec(layer, row0, 5, nb),
            row, row, row,
            pl.BlockSpec((None, D_MODEL, D_MODEL), lay, pipeline_mode=one),
            pl.BlockSpec((None, D_MODEL, D_FF), lay, pipeline_mode=one),
            pl.BlockSpec((None, D_MODEL, D_FF), lay, pipeline_mode=one),
            pl.BlockSpec((None, D_FF, D_MODEL), lay, pipeline_mode=one),
        ],
        out_specs=pl.BlockSpec((nb, tb, D_MODEL), lambda b, i: (b, i, 0)),
        out_shape=jax.ShapeDtypeStruct((bsz, t, D_MODEL), F32),
        compiler_params=pltpu.CompilerParams(
            dimension_semantics=("arbitrary", "arbitrary"), vmem_limit_bytes=VMEM_LIMIT),
        name="out_ffn",
    )(x, att, ssm, mods, mods, mods, mods, g_post_mix, g_pre_ffn, g_post_ffn,
      w_out, w_gate, w_up, w_down)


def kernel(x_prompt, x_sample, c_prompt, c_sample, cache_k, cache_v, cache_logf, state_ssm_re,
           state_ssm_im, w_ada, b_ada, g_pre_mix, g_post_mix, g_pre_ffn, g_post_ffn, w_in, b_forget,
           ssm_a_re, ssm_a_im, ssm_log_dt, ssm_b_re, ssm_b_im, ssm_c_re, ssm_c_im, ssm_d, w_glu,
           w_out, w_gate, w_up, w_down):
    depth = w_in.shape[0]
    bp, tp, _ = x_prompt.shape
    bs, ts, _ = x_sample.shape
    n_att = 3 * D_ATT + N_HEADS

    c_all = jnp.concatenate([c_prompt, c_sample, jnp.zeros((16 - bp - bs, D_MODEL), F32)], axis=0)
    mod = _ada_call(c_all, w_ada, b_ada)
    modr = mod.reshape(depth * 16 * 6, 1, D_MODEL)
    mod_s = mod[:, bp:bp + bs].reshape(depth, bs, 6, D_MODEL).transpose(0, 2, 1, 3)

    w_in_t = jnp.swapaxes(w_in, 1, 2)
    w_in_t = jnp.concatenate(
        [w_in_t[:, :n_att], jnp.zeros((depth, GATE_PAD - N_HEADS, D_MODEL), F32), w_in_t[:, n_att:]],
        axis=1).astype(BF16)
    bf_p = jnp.pad(b_forget, ((0, 0), (0, GATE_PAD - N_HEADS)))
    w_glu_b, w_out_b = w_glu.astype(BF16), w_out.astype(BF16)
    w_gate_b, w_up_b, w_down_b = w_gate.astype(BF16), w_up.astype(BF16), w_down.astype(BF16)
    toep, bpow, cpow, scm = _s5_tables(ssm_a_re, ssm_a_im, ssm_log_dt, jnp.swapaxes(ssm_b_re, 2, 3),
                                       jnp.swapaxes(ssm_b_im, 2, 3), ssm_c_re, ssm_c_im, ssm_d)
    cache_kt, cache_vt = jnp.swapaxes(cache_k, 3, 4), jnp.swapaxes(cache_v, 3, 4)
    h0 = jnp.concatenate([state_ssm_re, state_ssm_im], axis=-1).transpose(0, 2, 1, 3)

    n_chunk_p = tp // S5_CHUNK
    seg_s = ts // S5_CHUNK
    yp, ys = x_prompt, x_sample
    kvp = None
    lfp, ssp, kss, vss, lfs, sss = [], [], [], [], [], []
    for l in range(depth):
        qa, ka, va, kbuf, vbuf, lf, u, stats = _inproj_call(
            yp, modr, l, 0, g_pre_mix[l:l + 1], w_in_t, bf_p[l:l + 1], kvp, depth, 1, PROMPT_BLOCK)
        kvp = (kbuf, vbuf)
        lfp.append(lf)
        att = _attn_call(qa, ka, va, stats, PROMPT_BLOCK, ATTN_HEADS_PER_STEP, ATTN_HEADS_PER_LOOP)
        ssm, hfin = _s5_call(u, toep, bpow, cpow, scm, w_glu_b, None, l, n_chunk_p, n_chunk_p,
                             n_chunk_p, n_chunk_p - FIN_ROWS)
        ssp.append(hfin[:, :, FIN_ROWS - 1, :])
        yp = _out_ffn_call(yp, att, ssm, modr, l, 0, g_post_mix[l:l + 1], g_pre_ffn[l:l + 1],
                           g_post_ffn[l:l + 1], w_out_b, w_gate_b, w_up_b, w_down_b, 1, PROMPT_BLOCK)

        qa, ka, va, k_s, v_s, lf, u, _ = _inproj_call(
            ys, mod_s, l, bp, g_pre_mix[l:l + 1], w_in_t, bf_p[l:l + 1], None, None, bs, ts)
        kss.append(k_s)
        vss.append(v_s)
        lfs.append(lf.reshape(N_HEADS, bs, ts).transpose(1, 0, 2))
        att = _attn_sample_call(qa, ka, va, cache_kt, cache_vt, cache_logf, l)
        ssm, hfin = _s5_call(u, toep, bpow, cpow, scm, w_glu_b, h0, l, LANES, bs * seg_s, seg_s, 0)
        sss.append(hfin[0, :, seg_s - 1:bs * seg_s:seg_s, :].transpose(1, 0, 2))
        ys = _out_ffn_call(ys, att, ssm, mod_s, l, bp, g_post_mix[l:l + 1], g_pre_ffn[l:l + 1],
                           g_post_ffn[l:l + 1], w_out_b, w_gate_b, w_up_b, w_down_b, bs, ts)

    ssp, sss = jnp.stack(ssp), jnp.stack(sss)
    return (yp, ys, jnp.swapaxes(kvp[0], 3, 4), jnp.swapaxes(kvp[1], 3, 4), jnp.stack(lfp),
            ssp[..., :SSM_STATE], ssp[..., SSM_STATE:],
            jnp.stack(kss), jnp.stack(vss), jnp.stack(lfs),
            sss[..., :SSM_STATE], sss[..., SSM_STATE:])
```

```python
import functools
import math

import numpy as np
import jax
import jax.numpy as jnp
from jax import lax
from jax.experimental import pallas as pl
from jax.experimental.pallas import tpu as pltpu

F32 = jnp.float32
BF16 = jnp.bfloat16

D_MODEL = 1024
N_HEADS = 8
HEAD_DIM = 64
D_ATT = N_HEADS * HEAD_DIM
D_SSM = D_MODEL - D_ATT
SSM_GROUP = 16
N_GROUPS = D_SSM // SSM_GROUP
SSM_STATE = 64
D_FF = 2816
EPS = 1e-6

LANES = 128
S5_CHUNK = 16
S5_ROWS = S5_CHUNK * SSM_GROUP
GATE_PAD = LANES
W_IN_COLS = 3 * D_ATT + GATE_PAD + D_SSM
U_COL0 = 3 * D_ATT + GATE_PAD
N_U_SLABS = D_SSM // LANES
SCAN_PAD = 128
S5_PAIRS_PER_ITER = 2
VMEM_LIMIT = 56 * 1024 * 1024
MASK_VALUE = -1e30
LOG2E = math.log2(math.e)
DEN_LANE = HEAD_DIM
N_EXTRA = 6
EXTRA_SLOT = 8
SCORE_LIMIT = 80.0
NORM_SLACK = 1.02
EXP2_ZERO_BELOW = -150.0

FIN_ROWS = 32

PROMPT_BLOCK = 512
ATTN_HEADS_PER_STEP = N_HEADS
ATTN_HEADS_PER_LOOP = 4
ADA_COLS = 1536
TABLE_GROUPS = 8

NT_DIMS = (((1,), (1,)), ((), ()))


def _sigmoid(x):
    return 1.0 / (1.0 + jnp.exp(-x))


def _split3(x):
    hi = x.astype(BF16)
    r = x - hi.astype(F32)
    mid = r.astype(BF16)
    lo = (r - mid.astype(F32)).astype(BF16)
    return hi, mid, lo


def _split2(x):
    hi = x.astype(BF16)
    return hi, (x - hi.astype(F32)).astype(BF16)


def _dot(a, b):
    return jnp.dot(a, b, preferred_element_type=F32)


def _dot_nt(a, b):
    return lax.dot_general(a, b, NT_DIMS, preferred_element_type=F32)


def _rms(x, g):
    ms = jnp.mean(x * x, axis=-1, keepdims=True)
    return x * lax.rsqrt(ms + EPS) * g


def _per_row(ref, nb, tb):
    m = ref[...]
    if nb == 1:
        return m.reshape(1, D_MODEL)
    return jnp.broadcast_to(m[:, None, :], (nb, tb, D_MODEL)).reshape(nb * tb, D_MODEL)


def _ada_kernel(c_ref, w_ref, b_ref, o_ref):
    c = c_ref[...]
    a_hi, a_lo = _split2(c * _sigmoid(c))
    w_hi, w_lo = _split2(w_ref[0])
    o_ref[0] = _dot(a_hi, w_hi) + _dot(a_lo, w_hi) + _dot(a_hi, w_lo) + b_ref[0]


def _ada_call(c_all, w_ada, b_ada):
    depth = w_ada.shape[0]
    nb = ADA_COLS
    n_out = w_ada.shape[2]
    return pl.pallas_call(
        _ada_kernel,
        grid=(depth, n_out // nb),
        in_specs=[
            pl.BlockSpec((16, D_MODEL), lambda l, j: (0, 0)),
            pl.BlockSpec((1, D_MODEL, nb), lambda l, j: (l, 0, j)),
            pl.BlockSpec((1, 1, nb), lambda l, j: (l, 0, j)),
        ],
        out_specs=pl.BlockSpec((1, 16, nb), lambda l, j: (l, 0, j)),
        out_shape=jax.ShapeDtypeStruct((depth, 16, n_out), F32),
        compiler_params=pltpu.CompilerParams(
            dimension_semantics=("arbitrary", "arbitrary"), vmem_limit_bytes=VMEM_LIMIT),
        name="ada",
    )(c_all, w_ada, b_ada.reshape(depth, 1, n_out))


def _inproj_kernel(x_ref, sh_ref, sc_ref, g_ref, wt_ref, bf_ref, tri_ref, sel_ref, one_ref, eye_ref,
                   go_ref, qa_ref, ka_ref, va_ref, kc_ref, vc_ref, lf_ref, u_ref, st_ref, carry_ref,
                   *, nb, tb, kv_transposed):
    rows = nb * tb

    @pl.when(pl.program_id(1) == 0)
    def _():
        carry_ref[...] = jnp.zeros_like(carry_ref)

    x = x_ref[...].reshape(rows, D_MODEL)
    hm = _rms(x, g_ref[...]) * (1.0 + _per_row(sc_ref, nb, tb)) + _per_row(sh_ref, nb, tb)
    proj = _dot_nt(hm.astype(BF16), wt_ref[...])
    q = proj[:, 0:D_ATT] * (LOG2E * HEAD_DIM ** -0.5)
    k = proj[:, D_ATT:2 * D_ATT]
    v = proj[:, 2 * D_ATT:3 * D_ATT]
    gate = proj[:, 3 * D_ATT:U_COL0] + bf_ref[...]
    logf = jnp.minimum(gate, 0.0) - jnp.log1p(jnp.exp(-jnp.abs(gate)))
    lane = lax.broadcasted_iota(jnp.int32, (rows, LANES), 1)

    def pack3(val):
        hi, mid, lo = (part.astype(F32) for part in _split3(val))
        packed = jnp.where(lane < 2 * N_HEADS, pltpu.roll(mid, N_HEADS, axis=1),
                           jnp.where(lane < 3 * N_HEADS, pltpu.roll(lo, 2 * N_HEADS, axis=1), 0.0))
        return jnp.where(lane < N_HEADS, hi, packed).astype(BF16)

    lf_pack = pack3(logf)
    cs = _dot(tri_ref[...], lf_pack)
    fcum = (cs + pltpu.roll(cs, LANES - N_HEADS, axis=1) + pltpu.roll(cs, LANES - 2 * N_HEADS, axis=1)
            + carry_ref[...])
    carry_ref[...] = fcum[rows - 1:rows, :]

    extra = _dot(pack3(fcum * LOG2E), sel_ref[...]) + one_ref[...]
    in_extra = (lane >= HEAD_DIM) & (lane < HEAD_DIM + N_EXTRA)
    qn2 = _dot((q * q).astype(BF16), go_ref[...])
    kn2 = _dot((k * k).astype(BF16), go_ref[...])
    st_ref[0, 0] = jnp.concatenate(
        [jnp.max(qn2, axis=0, keepdims=True), jnp.max(kn2, axis=0, keepdims=True),
         fcum[0:1, :], fcum[rows - 1:rows, :], jnp.zeros((4, LANES), F32)], axis=0)
    lo_half = lane < HEAD_DIM
    den_one = jnp.where(lane == DEN_LANE, 1.0, 0.0)
    if kv_transposed:
        kt, vt = k.T, v.T
    for h in range(N_HEADS):
        pair = slice(LANES * (h // 2), LANES * (h // 2) + LANES)
        q2, k2, v2 = q[:, pair], k[:, pair], v[:, pair]
        if h % 2:
            q2 = pltpu.roll(q2, HEAD_DIM, axis=1)
            k2 = pltpu.roll(k2, HEAD_DIM, axis=1)
            v2 = pltpu.roll(v2, HEAD_DIM, axis=1)
        per_stream = lambda val: val.reshape(nb, tb, val.shape[-1])
        to_extra = lambda side: jnp.where(in_extra, pltpu.roll(
            extra[:, LANES * side:LANES * (side + 1)], (HEAD_DIM - EXTRA_SLOT * h) % LANES, axis=1), 0.0)
        qa_ref[:, h] = per_stream(jnp.where(lo_half, q2, to_extra(0)).astype(BF16))
        ka_ref[:, h] = per_stream(jnp.where(lo_half, k2, to_extra(1)).astype(BF16))
        va_ref[:, h] = per_stream(jnp.where(lo_half, v2, den_one).astype(BF16))
        if kv_transposed:
            kc_ref[h] = kt[HEAD_DIM * h:HEAD_DIM * (h + 1), :]
            vc_ref[h] = vt[HEAD_DIM * h:HEAD_DIM * (h + 1), :]
        else:
            kc_ref[:, h] = per_stream(k2[:, 0:HEAD_DIM])
            vc_ref[:, h] = per_stream(v2[:, 0:HEAD_DIM])
    lf_t = _dot_nt(eye_ref[...], lf_pack)
    lf_ref[0] = lf_t[0:N_HEADS] + lf_t[N_HEADS:2 * N_HEADS] + lf_t[2 * N_HEADS:3 * N_HEADS]
    for j in range(N_U_SLABS):
        u_ref[0, j] = proj[:, U_COL0 + LANES * j:U_COL0 + LANES * (j + 1)]


def _inproj_consts(nb, tb):
    tri = np.kron(np.eye(nb, dtype=np.float32), np.tril(np.ones((tb, tb), np.float32)))
    sel = np.zeros((LANES, 2 * LANES), np.float32)
    one = np.zeros((1, 2 * LANES), np.float32)
    group_ones = np.zeros((D_ATT, LANES), np.float32)
    for h in range(N_HEADS):
        group_ones[HEAD_DIM * h:HEAD_DIM * (h + 1), h] = 1.0
        for p in range(3):
            sel[p * N_HEADS + h, EXTRA_SLOT * h + p] = 1.0
            one[0, EXTRA_SLOT * h + 3 + p] = 1.0
            one[0, LANES + EXTRA_SLOT * h + p] = 1.0
            sel[p * N_HEADS + h, LANES + EXTRA_SLOT * h + 3 + p] = -1.0
    eye = np.eye(4 * N_HEADS, LANES, dtype=np.float32)
    eye[3 * N_HEADS:] = 0.0
    return (jnp.asarray(tri, BF16), jnp.asarray(sel, BF16), jnp.asarray(one, F32), jnp.asarray(eye, BF16),
            jnp.asarray(group_ones, BF16))


def _mod_spec(layer, row0, chunk, nb=1):
    if nb == 1:
        return pl.BlockSpec((1, 1, D_MODEL), lambda b, i: ((layer * 16 + row0 + b) * 6 + chunk, 0, 0))
    return pl.BlockSpec((None, None, nb, D_MODEL), lambda b, i: (layer, chunk, 0, 0))


def _inproj_call(x, mods, layer, row0, g_pre, w_in_t, b_forget, kv_bufs, kv_depth, nb, tb):
    bsz, t, _ = x.shape
    nblk = t // tb
    rows = nb * tb
    assert nb == 1 or (nblk == 1 and kv_depth is None)
    tri, sel, one, eye, group_ones = _inproj_consts(nb, tb)
    if nb > 1:
        u_shape = (1, N_U_SLABS, bsz * t, LANES)
        u_spec = pl.BlockSpec((1, N_U_SLABS, rows, LANES), lambda b, i: (0, 0, b, 0))
        lf_shape = (1, N_HEADS, bsz * t)
        lf_spec = pl.BlockSpec((1, N_HEADS, rows), lambda b, i: (0, 0, b))
    else:
        u_shape = (bsz, N_U_SLABS, t, LANES)
        u_spec = pl.BlockSpec((1, N_U_SLABS, tb, LANES), lambda b, i: (b, 0, i, 0))
        lf_shape = (bsz, N_HEADS, t)
        lf_spec = pl.BlockSpec((1, N_HEADS, tb), lambda b, i: (b, 0, i))
    const = lambda b, i: (0, 0)
    in_specs = [
        pl.BlockSpec((nb, tb, D_MODEL), lambda b, i: (b, i, 0)),
        _mod_spec(layer, row0, 0, nb), _mod_spec(layer, row0, 1, nb),
        pl.BlockSpec((1, D_MODEL), const),
        pl.BlockSpec((None, W_IN_COLS, D_MODEL), lambda b, i: (layer, 0, 0)),
        pl.BlockSpec((1, GATE_PAD), const),
        pl.BlockSpec((rows, rows), const),
        pl.BlockSpec((LANES, 2 * LANES), const),
        pl.BlockSpec((1, 2 * LANES), const),
        pl.BlockSpec((4 * N_HEADS, LANES), const),
        pl.BlockSpec((D_ATT, LANES), const),
    ]
    args = [x, mods, mods, g_pre, w_in_t, b_forget, tri, sel, one, eye, group_ones]
    n_main = len(args)
    head_blk = pl.BlockSpec((nb, N_HEADS, tb, LANES), lambda b, i: (b, 0, i, 0))
    aliases = {}
    if kv_depth is None:
        kv_shape = jax.ShapeDtypeStruct((bsz, N_HEADS, t, HEAD_DIM), F32)
        kv_spec = pl.BlockSpec((nb, N_HEADS, tb, HEAD_DIM), lambda b, i: (b, 0, i, 0))
    else:
        kv_shape = jax.ShapeDtypeStruct((kv_depth, bsz, N_HEADS, HEAD_DIM, t), F32)
        kv_spec = pl.BlockSpec((None, None, N_HEADS, HEAD_DIM, tb), lambda b, i: (layer, b, 0, 0, i))
        if kv_bufs is not None:
            in_specs += [pl.BlockSpec(memory_space=pl.ANY), pl.BlockSpec(memory_space=pl.ANY)]
            args += list(kv_bufs)
            aliases = {n_main: 3, n_main + 1: 4}
    out_shape = [
        jax.ShapeDtypeStruct((bsz, N_HEADS, t, LANES), BF16),
        jax.ShapeDtypeStruct((bsz, N_HEADS, t, LANES), BF16),
        jax.ShapeDtypeStruct((bsz, N_HEADS, t, LANES), BF16),
        kv_shape, kv_shape,
        jax.ShapeDtypeStruct(lf_shape, F32),
        jax.ShapeDtypeStruct(u_shape, F32),
        jax.ShapeDtypeStruct((bsz // nb, nblk, 8, LANES), F32),
    ]
    out_specs = [
        head_blk, head_blk, head_blk,
        kv_spec, kv_spec,
        lf_spec,
        u_spec,
        pl.BlockSpec((1, 1, 8, LANES), lambda b, i: (b, i, 0, 0)),
    ]
    n_args = len(args)

    def body(*refs):
        _inproj_kernel(*refs[:n_main], *refs[n_args:], nb=nb, tb=tb, kv_transposed=kv_depth is not None)

    return pl.pallas_call(
        body,
        grid=(bsz // nb, nblk),
        in_specs=in_specs,
        out_specs=out_specs,
        out_shape=out_shape,
        scratch_shapes=[pltpu.VMEM((1, LANES), F32)],
        input_output_aliases=aliases,
        compiler_params=pltpu.CompilerParams(
            dimension_semantics=("arbitrary", "arbitrary"), vmem_limit_bytes=VMEM_LIMIT),
        name="inproj",
    )(*args)


def _finish_heads(accs, lane):
    outs = []
    for acc in accs:
        den = jnp.sum(jnp.where(lane == DEN_LANE, acc, 0.0), axis=1, keepdims=True)
        outs.append(acc * (1.0 / den))
    return jnp.where(lane < HEAD_DIM, outs[0], pltpu.roll(outs[1], HEAD_DIM, axis=1)).astype(BF16)


def _attn_kernel(small_ref, first_ref, qa_ref, ka_ref, va_ref, o_ref, *, tq, hps, lps):
    i = pl.program_id(2)
    row = lax.broadcasted_iota(jnp.int32, (tq, tq), 0)
    col = lax.broadcasted_iota(jnp.int32, (tq, tq), 1)
    causal = col <= row
    lane = lax.broadcasted_iota(jnp.int32, (tq, LANES), 1)
    qs = [qa_ref[0, hh] for hh in range(hps)]

    def score(j, hh, masked):
        start = pl.multiple_of(j * tq, tq)
        s = _dot_nt(qs[hh], ka_ref[0, hh, pl.ds(start, tq), :])
        return jnp.where(causal, s, MASK_VALUE) if masked else s

    def values(j, hh):
        return va_ref[0, hh, pl.ds(pl.multiple_of(j * tq, tq), tq), :]

    def store(accs):
        for pr in range(hps // 2):
            o_ref[0, :, LANES * pr:LANES * (pr + 1)] = _finish_heads(accs[2 * pr:2 * pr + 2], lane)

    def tile_plain(j, accs, heads):
        return tuple(acc + _dot(jnp.exp2(score(j, hh, False)).astype(BF16), values(j, hh))
                     for acc, hh in zip(accs, heads))

    def diagonal_plain(accs):
        half = tq // 2
        start = pl.multiple_of(i * tq, tq)
        upper = (lax.broadcasted_iota(jnp.int32, (half, half), 1) <=
                 lax.broadcasted_iota(jnp.int32, (half, half), 0))
        lower = (lax.broadcasted_iota(jnp.int32, (half, tq), 1) <=
                 lax.broadcasted_iota(jnp.int32, (half, tq), 0) + half)
        new = []
        for hh in range(hps):
            k_t = ka_ref[0, hh, pl.ds(start, tq), :]
            v_t = va_ref[0, hh, pl.ds(start, tq), :]
            s_top = jnp.where(upper, _dot_nt(qs[hh][0:half], k_t[0:half]), MASK_VALUE)
            s_bot = jnp.where(lower, _dot_nt(qs[hh][half:tq], k_t), MASK_VALUE)
            o_top = _dot(jnp.exp2(s_top).astype(BF16), v_t[0:half])
            o_bot = _dot(jnp.exp2(s_bot).astype(BF16), v_t)
            new.append(accs[hh] + jnp.concatenate([o_top, o_bot], axis=0))
        return new

    def tile_shifted(j, carry, masked):
        new = []
        for hh in range(hps):
            m, acc = carry[hh]
            s = score(j, hh, masked)
            m_new = jnp.maximum(m, jnp.max(s, axis=1, keepdims=True))
            p = jnp.exp2(s - m_new)
            new.append((m_new, jnp.exp2(m - m_new) * acc + _dot(p.astype(BF16), values(j, hh))))
        return tuple(new)

    small = small_ref[pl.program_id(0), pl.program_id(1)] != 0
    zero = jnp.zeros((tq, LANES), F32)

    @pl.when(small)
    def _():
        accs = []
        for sub in range(hps // lps):
            heads = tuple(range(sub * lps, (sub + 1) * lps))
            first = first_ref[pl.program_id(0), pl.program_id(1) * (hps // lps) + sub, i]
            count = i - first
            two = lambda it, a, first=first, heads=heads: tile_plain(
                first + 2 * it + 1, tile_plain(first + 2 * it, a, heads), heads)
            last = lambda it, a, heads=heads: tile_plain(i - 1, a, heads)
            accs += lax.fori_loop(0, count % 2, last, lax.fori_loop(0, count // 2, two, (zero,) * lps))
        store(diagonal_plain(accs))

    @pl.when(jnp.logical_not(small))
    def _():
        init = ((jnp.full((tq, 1), MASK_VALUE, F32), zero),) * hps
        carry = lax.fori_loop(0, i, functools.partial(tile_shifted, masked=False), init)
        store([c[1] for c in tile_shifted(i, carry, True)])


def _attn_call(qa, ka, va, stats, tq, hps, lps):
    bsz, _, t, _ = qa.shape
    nt, ng, nl = t // tq, N_HEADS // hps, N_HEADS // lps
    assert stats.shape[1] == nt
    bound = jnp.sqrt(jnp.max(stats[:, :, 0, :N_HEADS], axis=1) *
                     jnp.max(stats[:, :, 1, :N_HEADS], axis=1)) * NORM_SLACK
    small = jnp.max(bound.reshape(bsz, ng, hps), axis=-1) < SCORE_LIMIT
    f_first, f_last = stats[:, :, 2, :N_HEADS] * LOG2E, stats[:, :, 3, :N_HEADS] * LOG2E
    top = bound[:, None, None, :] + f_first[:, :, None, :] - f_last[:, None, :, :]
    needed = jnp.max((top >= EXP2_ZERO_BELOW).astype(jnp.int32).reshape(bsz, nt, nt, nl, lps), axis=-1)
    first = jnp.where(jnp.repeat(small, nl // ng, axis=1)[:, :, None],
                      jnp.sum(1 - needed, axis=2).transpose(0, 2, 1), 0)
    whole = pl.BlockSpec((1, hps, t, LANES), lambda b, p, i, *_: (b, p, 0, 0))
    return pl.pallas_call(
        functools.partial(_attn_kernel, tq=tq, hps=hps, lps=lps),
        grid_spec=pltpu.PrefetchScalarGridSpec(
            num_scalar_prefetch=2,
            grid=(bsz, ng, nt),
            in_specs=[pl.BlockSpec((1, hps, tq, LANES), lambda b, p, i, *_: (b, p, i, 0)), whole, whole],
            out_specs=pl.BlockSpec((1, tq, HEAD_DIM * hps), lambda b, p, i, *_: (b, i, p)),
        ),
        out_shape=jax.ShapeDtypeStruct((bsz, t, D_ATT), BF16),
        compiler_params=pltpu.CompilerParams(
            dimension_semantics=("arbitrary", "arbitrary", "arbitrary"),
            vmem_limit_bytes=VMEM_LIMIT),
        name="fox_attn",
    )(small.astype(jnp.int32), first.astype(jnp.int32), qa, ka, va)


def _attn_sample_kernel(qa_ref, ka_ref, va_ref, ckt_ref, cvt_ref, clf_ref, o_ref, *, t, n_past):
    lane_p = lax.broadcasted_iota(jnp.int32, (N_HEADS, n_past), 1)
    c = clf_ref[...]
    total = jnp.sum(c, axis=1, keepdims=True)
    d = 1
    while d < n_past:
        c = c + jnp.where(lane_p >= d, pltpu.roll(c, d, axis=1), 0.0)
        d *= 2
    f_past = (c - total) * LOG2E
    lane = lax.broadcasted_iota(jnp.int32, (t, LANES), 1)
    causal = (lax.broadcasted_iota(jnp.int32, (t, t), 1) <= lax.broadcasted_iota(jnp.int32, (t, t), 0))
    den_rows = jnp.where(lax.broadcasted_iota(jnp.int32, (LANES - HEAD_DIM, n_past), 0) == 0, 1.0, 0.0)
    heads = range(N_HEADS)
    qs = [qa_ref[0, h] for h in heads]
    fqs = [jnp.sum(jnp.where((lane >= HEAD_DIM) & (lane < HEAD_DIM + 3), q.astype(F32), 0.0),
                   axis=1, keepdims=True) for q in qs]
    s_past = [_dot(qs[h][:, 0:HEAD_DIM], ckt_ref[h].astype(BF16)) + fqs[h] - f_past[h:h + 1, :]
              for h in heads]
    s_new = [jnp.where(causal, _dot_nt(qs[h], ka_ref[0, h]), MASK_VALUE) for h in heads]
    ms = [jnp.maximum(jnp.max(s_past[h], axis=1, keepdims=True), jnp.max(s_new[h], axis=1, keepdims=True))
          for h in heads]
    p_past = [jnp.exp2(s_past[h] - ms[h]).astype(BF16) for h in heads]
    p_new = [jnp.exp2(s_new[h] - ms[h]).astype(BF16) for h in heads]
    cvt_aug = [jnp.concatenate([cvt_ref[h], den_rows], axis=0).astype(BF16) for h in heads]
    accs = [_dot_nt(p_past[h], cvt_aug[h]) + _dot(p_new[h], va_ref[0, h]) for h in heads]
    for pr in range(N_HEADS // 2):
        o_ref[0, :, LANES * pr:LANES * (pr + 1)] = _finish_heads(accs[2 * pr:2 * pr + 2], lane)


def _attn_sample_call(qa, ka, va, cache_kt, cache_vt, cache_logf, layer):
    bsz, _, t, _ = qa.shape
    n_past = cache_kt.shape[4]
    cache_spec = pl.BlockSpec((None, None, N_HEADS, HEAD_DIM, n_past), lambda b: (layer, b, 0, 0, 0))
    new_spec = pl.BlockSpec((1, N_HEADS, t, LANES), lambda b: (b, 0, 0, 0))
    return pl.pallas_call(
        functools.partial(_attn_sample_kernel, t=t, n_past=n_past),
        grid=(bsz,),
        in_specs=[
            new_spec, new_spec, new_spec, cache_spec, cache_spec,
            pl.BlockSpec((None, None, N_HEADS, n_past), lambda b: (layer, b, 0, 0)),
        ],
        out_specs=pl.BlockSpec((1, t, D_ATT), lambda b: (b, 0, 0)),
        out_shape=jax.ShapeDtypeStruct((bsz, t, D_ATT), BF16),
        compiler_params=pltpu.CompilerParams(
            dimension_semantics=("arbitrary",), vmem_limit_bytes=VMEM_LIMIT),
        name="fox_attn_sample",
    )(qa, ka, va, cache_kt, cache_vt, cache_logf)


def _s5_table_kernel(ca_ref, cb_ref, ba_ref, bb_ref, prr_ref, pii_ref, dv_ref,
                     toep_ref, bpow_ref, cpow_ref, tt_scr, *, gb):
    row = lax.broadcasted_iota(jnp.int32, (SSM_GROUP, S5_ROWS), 0)
    lane = lax.broadcasted_iota(jnp.int32, (SSM_GROUP, S5_ROWS), 1)
    for gi in range(gb):
        ca, cb, ba, bb = ca_ref[gi], cb_ref[gi], ba_ref[gi], bb_ref[gi]
        power = lambda a, b, n: a * prr_ref[gi, n:n + 1, :] + b * pii_ref[gi, n:n + 1, :]
        xs = [power(ca, cb, n) for n in range(S5_CHUNK + 1)]
        cpow_ref[gi] = jnp.concatenate(xs[1:], axis=0).astype(BF16)
        bpow_ref[gi] = jnp.concatenate(
            [power(ba, bb, S5_CHUNK - 1 - s) for s in range(S5_CHUNK)], axis=0).T.astype(BF16)
        x_hi, x_lo = _split2(jnp.concatenate(xs[:S5_CHUNK], axis=0))
        b_hi, b_lo = _split2(ba)
        r0 = _dot_nt(b_hi, x_hi) + _dot_nt(b_lo, x_hi) + _dot_nt(b_hi, x_lo)
        r0 = r0 + jnp.where(lane == row, dv_ref[gi], 0.0)
        for s in range(S5_CHUNK):
            blk = r0
            if s:
                blk = jnp.where(lane >= SSM_GROUP * s, pltpu.roll(r0, SSM_GROUP * s, axis=1), 0.0)
            tt_scr[SSM_GROUP * s:SSM_GROUP * (s + 1), :] = blk
        toep_ref[gi] = tt_scr[...].T.astype(BF16)


def _s5_tables(a_re, a_im, log_dt, b_re_t, b_im_t, c_re, c_im, d_skip):
    depth = a_re.shape[0]
    dt = jnp.exp(log_dt)[..., None]
    x, y = a_re * dt, a_im * dt
    ex, cy, sy = jnp.exp(x), jnp.cos(y), jnp.sin(y)
    ar, ai = ex * cy, ex * sy
    sh = jnp.sin(0.5 * y)
    nr, ni = jnp.expm1(x) * cy - 2.0 * sh * sh, ai
    den = a_re * a_re + a_im * a_im
    fr = ((nr * a_re + ni * a_im) / den)[:, :, None, :]
    fi = ((ni * a_re - nr * a_im) / den)[:, :, None, :]
    bbr, bbi = fr * b_re_t - fi * b_im_t, fr * b_im_t + fi * b_re_t
    pr, pi = [jnp.ones_like(ar)], [jnp.zeros_like(ar)]
    for _ in range(S5_CHUNK):
        pr, pi = pr + [pr[-1] * ar - pi[-1] * ai], pi + [pr[-1] * ai + pi[-1] * ar]
    mr, mi = pr[S5_CHUNK], pi[S5_CHUNK]
    pr, pi = jnp.stack(pr, axis=2), jnp.stack(pi, axis=2)
    cat = lambda a, b: jnp.concatenate([a, b], axis=-1)
    ins = [cat(c_re, -c_im), cat(-c_im, -c_re), cat(bbr, bbi), cat(-bbi, bbr), cat(pr, pr), cat(pi, pi),
           jnp.pad(d_skip.reshape(depth, N_GROUPS, 1, SSM_GROUP), ((0, 0), (0, 0), (0, 0), (0, S5_ROWS - SSM_GROUP)))]
    gb = TABLE_GROUPS
    spec = lambda rows, cols: pl.BlockSpec((None, gb, rows, cols), lambda l, j: (l, j, 0, 0))
    tab = lambda rows, cols: jax.ShapeDtypeStruct((depth, N_GROUPS, rows, cols), BF16)
    toep, bpow, cpow = pl.pallas_call(
        functools.partial(_s5_table_kernel, gb=gb),
        grid=(depth, N_GROUPS // gb),
        in_specs=[spec(SSM_GROUP, LANES)] * 4 + [spec(S5_CHUNK + 1, LANES)] * 2 + [spec(1, S5_ROWS)],
        out_specs=[spec(S5_ROWS, S5_ROWS), spec(LANES, S5_ROWS), spec(S5_ROWS, LANES)],
        out_shape=[tab(S5_ROWS, S5_ROWS), tab(LANES, S5_ROWS), tab(S5_ROWS, LANES)],
        scratch_shapes=[pltpu.VMEM((S5_ROWS, S5_ROWS), F32)],
        compiler_params=pltpu.CompilerParams(
            dimension_semantics=("arbitrary", "arbitrary"), vmem_limit_bytes=VMEM_LIMIT),
        name="s5_tables",
    )(*ins)
    rows_r, rows_i = [], []
    for _ in range(8):
        rows_r.append(cat(mr, mr))
        rows_i.append(cat(-mi, mi))
        mr, mi = mr * mr - mi * mi, 2.0 * mr * mi
    scm = jnp.stack(rows_r + rows_i, axis=2)
    return toep, bpow, cpow, scm


def _gelu_tanh(y):
    return 0.5 * y * (1.0 + jnp.tanh(math.sqrt(2.0 / math.pi) * (y + 0.044715 * (y * y * y))))


def _s5_kernel(*refs, n, n_real, seglen, fin_row0, use_h0):
    if use_h0:
        u_ref, toep_ref, bpow_ref, cpow_ref, scm_ref, wglu_ref, h0_ref = refs[:7]
        o_ref, hfin_ref, z_scr, zt_scr, h_scr, h0_scr = refs[7:]
    else:
        u_ref, toep_ref, bpow_ref, cpow_ref, scm_ref, wglu_ref = refs[:6]
        o_ref, hfin_ref, z_scr, zt_scr, h_scr = refs[6:]
    gps = LANES // SSM_GROUP
    for pp in range(S5_PAIRS_PER_ITER):
        for part in range(2):
            h_scr[pp, part, 0:SCAN_PAD, :] = jnp.zeros((SCAN_PAD, LANES), F32)
    if use_h0:
        h0_scr[...] = jnp.zeros(h0_scr.shape, F32)

    for k in range(S5_CHUNK):
        for j in range(N_U_SLABS):
            vv = u_ref[0, j, pl.ds(k, n_real, stride=S5_CHUNK), :]
            if n_real < n:
                vv = jnp.concatenate([vv, jnp.zeros((n - n_real, LANES), F32)], axis=0)
            z_scr[gps * j:gps * (j + 1), SSM_GROUP * k:SSM_GROUP * (k + 1), :] = (
                vv.T.reshape(gps, SSM_GROUP, n).astype(BF16))

    chunk_in_seg = lax.broadcasted_iota(jnp.int32, (n, LANES), 0) % seglen
    lo = lax.broadcasted_iota(jnp.int32, (n, LANES), 1) < SSM_STATE
    lo_row = lax.broadcasted_iota(jnp.int32, (1, LANES), 1) < SSM_STATE
    lo_fin = lax.broadcasted_iota(jnp.int32, (FIN_ROWS, LANES), 1) < SSM_STATE
    swap = lambda a: pltpu.roll(a, SSM_STATE, axis=1)

    def pair_step(it, _):
        pairs = range(S5_PAIRS_PER_ITER)
        gss = [[2 * (it * S5_PAIRS_PER_ITER + pp), 2 * (it * S5_PAIRS_PER_ITER + pp) + 1] for pp in pairs]
        flat = [g for gs in gss for g in gs]
        ucts = [z_scr[g] for g in flat]
        yts = [_dot(toep_ref[g], u) for g, u in zip(flat, ucts)]
        hs = [_dot(bpow_ref[g], u).T for g, u in zip(flat, ucts)]
        h0s = [0.0] * len(flat)
        if use_h0:
            for idx, g in enumerate(flat):
                h0_scr[idx, pl.ds(0, h0_ref.shape[1], stride=seglen), :] = h0_ref[g]
                h0s[idx] = h0_scr[idx]
                hs[idx] = hs[idx] + scm_ref[g, 0:1, :] * h0s[idx] + scm_ref[g, 8:9, :] * swap(h0s[idx])
        hre = [jnp.where(lo, hs[2 * pp], swap(hs[2 * pp + 1])) for pp in pairs]
        him = [jnp.where(lo, swap(hs[2 * pp]), hs[2 * pp + 1]) for pp in pairs]

        def shifted(by):
            for pp in pairs:
                h_scr[pp, 0, SCAN_PAD:SCAN_PAD + n, :] = hre[pp]
                h_scr[pp, 1, SCAN_PAD:SCAN_PAD + n, :] = him[pp]
            return ([h_scr[pp, 0, pl.ds(SCAN_PAD - by, n), :] for pp in pairs],
                    [h_scr[pp, 1, pl.ds(SCAN_PAD - by, n), :] for pp in pairs])

        d, j = 1, 0
        while d < seglen:
            sre, sim = shifted(d)
            for pp in pairs:
                gs = gss[pp]
                a, b = sre[pp], sim[pp]
                if seglen < n:
                    a = jnp.where(chunk_in_seg >= d, a, 0.0)
                    b = jnp.where(chunk_in_seg >= d, b, 0.0)
                ar = jnp.where(lo_row, scm_ref[gs[0], j:j + 1, :], scm_ref[gs[1], j:j + 1, :])
                ai = jnp.where(lo_row, -scm_ref[gs[0], 8 + j:9 + j, :], scm_ref[gs[1], 8 + j:9 + j, :])
                hre[pp], him[pp] = hre[pp] + ar * a - ai * b, him[pp] + ar * b + ai * a
            d, j = 2 * d, j + 1
        pre, pim = shifted(1)
        for pp in pairs:
            hps = [jnp.where(lo, pre[pp], swap(pim[pp])), jnp.where(lo, swap(pre[pp]), pim[pp])]
            fre = hre[pp][fin_row0:fin_row0 + FIN_ROWS, :]
            fim = him[pp][fin_row0:fin_row0 + FIN_ROWS, :]
            fins = [jnp.where(lo_fin, fre, swap(fim)), jnp.where(lo_fin, swap(fre), fim)]
            for idx, g in enumerate(gss[pp]):
                hp = hps[idx]
                if seglen < n:
                    hp = jnp.where(chunk_in_seg >= 1, hp, h0s[2 * pp + idx])
                zt_scr[g] = _gelu_tanh(yts[2 * pp + idx] + _dot_nt(cpow_ref[g], hp.astype(BF16)))
                hfin_ref[0, g] = fins[idx]
        return 0

    lax.fori_loop(0, N_GROUPS // (2 * S5_PAIRS_PER_ITER), pair_step, 0)

    for k in range(S5_CHUNK):
        slabs = []
        for j in range(N_U_SLABS):
            w = zt_scr[gps * j:gps * (j + 1), SSM_GROUP * k:SSM_GROUP * (k + 1), :]
            slabs.append(w.reshape(LANES, n).T)
        zk = jnp.concatenate(slabs, axis=1)
        out = zk * _sigmoid(_dot(zk.astype(BF16), wglu_ref[...]))
        for j in range(N_U_SLABS):
            o_ref[0, j, pl.ds(k, n_real, stride=S5_CHUNK), :] = out[0:n_real, LANES * j:LANES * (j + 1)]


def _s5_call(u, toep, bpow, cpow, scm, wglu, h0, layer, n, n_real, seglen, fin_row0):
    nb, _, tt, _ = u.shape
    use_h0 = h0 is not None
    one = pl.Buffered(1)
    tab = lambda rows, cols: pl.BlockSpec((None, N_GROUPS, rows, cols), lambda b: (layer, 0, 0, 0),
                                          pipeline_mode=one)
    in_specs = [
        pl.BlockSpec((1, N_U_SLABS, tt, LANES), lambda b: (b, 0, 0, 0)),
        tab(S5_ROWS, S5_ROWS), tab(LANES, S5_ROWS), tab(S5_ROWS, LANES), tab(16, LANES),
        pl.BlockSpec((None, D_SSM, D_SSM), lambda b: (layer, 0, 0), pipeline_mode=one),
    ]
    args = [u, toep, bpow, cpow, scm, wglu]
    scratch = [
        pltpu.VMEM((N_GROUPS, S5_ROWS, n), BF16),
        pltpu.VMEM((N_GROUPS, S5_ROWS, n), F32),
        pltpu.VMEM((S5_PAIRS_PER_ITER, 2, SCAN_PAD + n, LANES), F32),
    ]
    if use_h0:
        in_specs.append(tab(h0.shape[2], LANES))
        args.append(h0)
        scratch.append(pltpu.VMEM((2 * S5_PAIRS_PER_ITER, n, LANES), F32))
    return pl.pallas_call(
        functools.partial(_s5_kernel, n=n, n_real=n_real, seglen=seglen, fin_row0=fin_row0,
                          use_h0=use_h0),
        grid=(nb,),
        in_specs=in_specs,
        out_specs=[
            pl.BlockSpec((1, N_U_SLABS, tt, LANES), lambda b: (b, 0, 0, 0)),
            pl.BlockSpec((1, N_GROUPS, FIN_ROWS, LANES), lambda b: (b, 0, 0, 0)),
        ],
        out_shape=[
            jax.ShapeDtypeStruct((nb, N_U_SLABS, tt, LANES), F32),
            jax.ShapeDtypeStruct((nb, N_GROUPS, FIN_ROWS, LANES), F32),
        ],
        scratch_shapes=scratch,
        compiler_params=pltpu.CompilerParams(
            dimension_semantics=("arbitrary",), vmem_limit_bytes=VMEM_LIMIT),
        name="s5_mix",
    )(*args)


def _out_ffn_kernel(x_ref, att_ref, ssm_ref, g1_ref, sh2_ref, sc2_ref, g2_ref,
                    gpm_ref, gpf_ref, gpo_ref, wo_ref, wg_ref, wu_ref, wd_ref, o_ref, *, nb, tb):
    rows = nb * tb
    mix = jnp.concatenate(
        [att_ref[...].reshape(rows, D_ATT)] + [ssm_ref[0, j].astype(BF16) for j in range(N_U_SLABS)],
        axis=1)
    o = _dot(mix, wo_ref[...])
    x1 = x_ref[...].reshape(rows, D_MODEL) + _per_row(g1_ref, nb, tb) * _rms(o, gpm_ref[...])
    hf = (_rms(x1, gpf_ref[...]) * (1.0 + _per_row(sc2_ref, nb, tb)) + _per_row(sh2_ref, nb, tb)).astype(BF16)
    gt = _dot(hf, wg_ref[...])
    up = _dot(hf, wu_ref[...])
    hid = (gt * _sigmoid(gt) * up).astype(BF16)
    f = _dot(hid, wd_ref[...])
    o_ref[...] = (x1 + _per_row(g2_ref, nb, tb) * _rms(f, gpo_ref[...])).reshape(nb, tb, D_MODEL)


def _out_ffn_call(x, att, ssm, mods, layer, row0, g_post_mix, g_pre_ffn, g_post_ffn,
                  w_out, w_gate, w_up, w_down, nb, tb):
    bsz, t, _ = x.shape
    nblk = t // tb
    one = pl.Buffered(1)
    const = lambda b, i: (0, 0)
    lay = lambda b, i: (layer, 0, 0)
    if nb > 1:
        ssm_spec = pl.BlockSpec((1, N_U_SLABS, nb * tb, LANES), lambda b, i: (0, 0, b, 0))
    else:
        ssm_spec = pl.BlockSpec((1, N_U_SLABS, tb, LANES), lambda b, i: (b, 0, i, 0))
    row = pl.BlockSpec((1, D_MODEL), const)
    return pl.pallas_call(
        functools.partial(_out_ffn_kernel, nb=nb, tb=tb),
        grid=(bsz // nb, nblk),
        in_specs=[
            pl.BlockSpec((nb, tb, D_MODEL), lambda b, i: (b, i, 0)),
            pl.BlockSpec((nb, tb, D_ATT), lambda b, i: (b, i, 0)),
            ssm_spec,
            _mod_spec(layer, row0, 2, nb), _mod_spec(layer, row0, 3, nb), _mod_spec(layer, row0, 4, nb),
            _mod_spec(layer, row0, 5, nb),
            row, row, row,
            pl.BlockSpec((None, D_MODEL, D_MODEL), lay, pipeline_mode=one),
            pl.BlockSpec((None, D_MODEL, D_FF), lay, pipeline_mode=one),
            pl.BlockSpec((None, D_MODEL, D_FF), lay, pipeline_mode=one),
            pl.BlockSpec((None, D_FF, D_MODEL), lay, pipeline_mode=one),
        ],
        out_specs=pl.BlockSpec((nb, tb, D_MODEL), lambda b, i: (b, i, 0)),
        out_shape=jax.ShapeDtypeStruct((bsz, t, D_MODEL), F32),
        compiler_params=pltpu.CompilerParams(
            dimension_semantics=("arbitrary", "arbitrary"), vmem_limit_bytes=VMEM_LIMIT),
        name="out_ffn",
    )(x, att, ssm, mods, mods, mods, mods, g_post_mix, g_pre_ffn, g_post_ffn,
      w_out, w_gate, w_up, w_down)


def kernel(x_prompt, x_sample, c_prompt, c_sample, cache_k, cache_v, cache_logf, state_ssm_re,
           state_ssm_im, w_ada, b_ada, g_pre_mix, g_post_mix, g_pre_ffn, g_post_ffn, w_in, b_forget,
           ssm_a_re, ssm_a_im, ssm_log_dt, ssm_b_re, ssm_b_im, ssm_c_re, ssm_c_im, ssm_d, w_glu,
           w_out, w_gate, w_up, w_down):
    depth = w_in.shape[0]
    bp, tp, _ = x_prompt.shape
    bs, ts, _ = x_sample.shape
    n_att = 3 * D_ATT + N_HEADS

    c_all = jnp.concatenate([c_prompt, c_sample, jnp.zeros((16 - bp - bs, D_MODEL), F32)], axis=0)
    mod = _ada_call(c_all, w_ada, b_ada)
    modr = mod.reshape(depth * 16 * 6, 1, D_MODEL)
    mod_s = mod[:, bp:bp + bs].reshape(depth, bs, 6, D_MODEL).transpose(0, 2, 1, 3)

    w_in_t = jnp.swapaxes(w_in, 1, 2)
    w_in_t = jnp.concatenate(
        [w_in_t[:, :n_att], jnp.zeros((depth, GATE_PAD - N_HEADS, D_MODEL), F32), w_in_t[:, n_att:]],
        axis=1).astype(BF16)
    bf_p = jnp.pad(b_forget, ((0, 0), (0, GATE_PAD - N_HEADS)))
    w_glu_b, w_out_b = w_glu.astype(BF16), w_out.astype(BF16)
    w_gate_b, w_up_b, w_down_b = w_gate.astype(BF16), w_up.astype(BF16), w_down.astype(BF16)
    toep, bpow, cpow, scm = _s5_tables(ssm_a_re, ssm_a_im, ssm_log_dt, jnp.swapaxes(ssm_b_re, 2, 3),
                                       jnp.swapaxes(ssm_b_im, 2, 3), ssm_c_re, ssm_c_im, ssm_d)
    cache_kt, cache_vt = jnp.swapaxes(cache_k, 3, 4), jnp.swapaxes(cache_v, 3, 4)
    h0 = jnp.concatenate([state_ssm_re, state_ssm_im], axis=-1).transpose(0, 2, 1, 3)

    n_chunk_p = tp // S5_CHUNK
    seg_s = ts // S5_CHUNK
    yp, ys = x_prompt, x_sample
    kvp = None
    lfp, ssp, kss, vss, lfs, sss = [], [], [], [], [], []
    for l in range(depth):
        qa, ka, va, kbuf, vbuf, lf, u, stats = _inproj_call(
            yp, modr, l, 0, g_pre_mix[l:l + 1], w_in_t, bf_p[l:l + 1], kvp, depth, 1, PROMPT_BLOCK)
        kvp = (kbuf, vbuf)
        lfp.append(lf)
        att = _attn_call(qa, ka, va, stats, PROMPT_BLOCK, ATTN_HEADS_PER_STEP, ATTN_HEADS_PER_LOOP)
        ssm, hfin = _s5_call(u, toep, bpow, cpow, scm, w_glu_b, None, l, n_chunk_p, n_chunk_p,
                             n_chunk_p, n_chunk_p - FIN_ROWS)
        ssp.append(hfin[:, :, FIN_ROWS - 1, :])
        yp = _out_ffn_call(yp, att, ssm, modr, l, 0, g_post_mix[l:l + 1], g_pre_ffn[l:l + 1],
                           g_post_ffn[l:l + 1], w_out_b, w_gate_b, w_up_b, w_down_b, 1, PROMPT_BLOCK)

        qa, ka, va, k_s, v_s, lf, u, _ = _inproj_call(
            ys, mod_s, l, bp, g_pre_mix[l:l + 1], w_in_t, bf_p[l:l + 1], None, None, bs, ts)
        kss.append(k_s)
        vss.append(v_s)
        lfs.append(lf.reshape(N_HEADS, bs, ts).transpose(1, 0, 2))
        att = _attn_sample_call(qa, ka, va, cache_kt, cache_vt, cache_logf, l)
        ssm, hfin = _s5_call(u, toep, bpow, cpow, scm, w_glu_b, h0, l, LANES, bs * seg_s, seg_s, 0)
        sss.append(hfin[0, :, seg_s - 1:bs * seg_s:seg_s, :].transpose(1, 0, 2))
        ys = _out_ffn_call(ys, att, ssm, mod_s, l, bp, g_post_mix[l:l + 1], g_pre_ffn[l:l + 1],
                           g_post_ffn[l:l + 1], w_out_b, w_gate_b, w_up_b, w_down_b, bs, ts)

    ssp, sss = jnp.stack(ssp), jnp.stack(sss)
    return (yp, ys, jnp.swapaxes(kvp[0], 3, 4), jnp.swapaxes(kvp[1], 3, 4), jnp.stack(lfp),
            ssp[..., :SSM_STATE], ssp[..., SSM_STATE:],
            jnp.stack(kss), jnp.stack(vss), jnp.stack(lfs),
            sss[..., :SSM_STATE], sss[..., SSM_STATE:])
```

```python
import functools
import math

import numpy as np
import jax
import jax.numpy as jnp
from jax import lax
from jax.experimental import pallas as pl
from jax.experimental.pallas import tpu as pltpu

F32 = jnp.float32
BF16 = jnp.bfloat16

D_MODEL = 1024
N_HEADS = 8
HEAD_DIM = 64
D_ATT = N_HEADS * HEAD_DIM
D_SSM = D_MODEL - D_ATT
SSM_GROUP = 16
N_GROUPS = D_SSM // SSM_GROUP
SSM_STATE = 64
D_FF = 2816
EPS = 1e-6

LANES = 128
S5_CHUNK = 16
S5_ROWS = S5_CHUNK * SSM_GROUP
GATE_PAD = LANES
W_IN_COLS = 3 * D_ATT + GATE_PAD + D_SSM
U_COL0 = 3 * D_ATT + GATE_PAD
N_U_SLABS = D_SSM // LANES
SCAN_PAD = 128
S5_PAIRS_PER_ITER = 2
VMEM_LIMIT = 56 * 1024 * 1024
MASK_VALUE = -1e30
LOG2E = math.log2(math.e)
DEN_LANE = HEAD_DIM
N_EXTRA = 6
EXTRA_SLOT = 8
SCORE_LIMIT = 80.0
NORM_SLACK = 1.02
EXP2_ZERO_BELOW = -150.0

FIN_ROWS = 32

PROMPT_BLOCK = 512
ATTN_HEADS_PER_STEP = N_HEADS
ATTN_HEADS_PER_LOOP = 4
ADA_COLS = 1536
TABLE_GROUPS = 8

NT_DIMS = (((1,), (1,)), ((), ()))


def _sigmoid(x):
    return 1.0 / (1.0 + jnp.exp(-x))


def _split3(x):
    hi = x.astype(BF16)
    r = x - hi.astype(F32)
    mid = r.astype(BF16)
    lo = (r - mid.astype(F32)).astype(BF16)
    return hi, mid, lo


def _split2(x):
    hi = x.astype(BF16)
    return hi, (x - hi.astype(F32)).astype(BF16)


def _dot(a, b):
    return jnp.dot(a, b, preferred_element_type=F32)


def _dot_nt(a, b):
    return lax.dot_general(a, b, NT_DIMS, preferred_element_type=F32)


def _rms(x, g):
    ms = jnp.mean(x * x, axis=-1, keepdims=True)
    return x * lax.rsqrt(ms + EPS) * g


def _per_row(ref, nb, tb):
    m = ref[...]
    if nb == 1:
        return m.reshape(1, D_MODEL)
    return jnp.broadcast_to(m[:, None, :], (nb, tb, D_MODEL)).reshape(nb * tb, D_MODEL)


def _ada_kernel(c_ref, w_ref, b_ref, o_ref):
    c = c_ref[...]
    a_hi, a_lo = _split2(c * _sigmoid(c))
    w_hi, w_lo = _split2(w_ref[0])
    o_ref[0] = _dot(a_hi, w_hi) + _dot(a_lo, w_hi) + _dot(a_hi, w_lo) + b_ref[0]


def _ada_call(c_all, w_ada, b_ada):
    depth = w_ada.shape[0]
    nb = ADA_COLS
    n_out = w_ada.shape[2]
    return pl.pallas_call(
        _ada_kernel,
        grid=(depth, n_out // nb),
        in_specs=[
            pl.BlockSpec((16, D_MODEL), lambda l, j: (0, 0)),
            pl.BlockSpec((1, D_MODEL, nb), lambda l, j: (l, 0, j)),
            pl.BlockSpec((1, 1, nb), lambda l, j: (l, 0, j)),
        ],
        out_specs=pl.BlockSpec((1, 16, nb), lambda l, j: (l, 0, j)),
        out_shape=jax.ShapeDtypeStruct((depth, 16, n_out), F32),
        compiler_params=pltpu.CompilerParams(
            dimension_semantics=("arbitrary", "arbitrary"), vmem_limit_bytes=VMEM_LIMIT),
        name="ada",
    )(c_all, w_ada, b_ada.reshape(depth, 1, n_out))


def _inproj_kernel(x_ref, sh_ref, sc_ref, g_ref, wt_ref, bf_ref, tri_ref, sel_ref, one_ref, eye_ref,
                   go_ref, qa_ref, ka_ref, va_ref, kc_ref, vc_ref, lf_ref, u_ref, st_ref, carry_ref,
                   *, nb, tb, kv_transposed):
    rows = nb * tb

    @pl.when(pl.program_id(1) == 0)
    def _():
        carry_ref[...] = jnp.zeros_like(carry_ref)

    x = x_ref[...].reshape(rows, D_MODEL)
    hm = _rms(x, g_ref[...]) * (1.0 + _per_row(sc_ref, nb, tb)) + _per_row(sh_ref, nb, tb)
    proj = _dot_nt(hm.astype(BF16), wt_ref[...])
    q = proj[:, 0:D_ATT] * (LOG2E * HEAD_DIM ** -0.5)
    k = proj[:, D_ATT:2 * D_ATT]
    v = proj[:, 2 * D_ATT:3 * D_ATT]
    gate = proj[:, 3 * D_ATT:U_COL0] + bf_ref[...]
    logf = jnp.minimum(gate, 0.0) - jnp.log1p(jnp.exp(-jnp.abs(gate)))
    lane = lax.broadcasted_iota(jnp.int32, (rows, LANES), 1)

    def pack3(val):
        hi, mid, lo = (part.astype(F32) for part in _split3(val))
        packed = jnp.where(lane < 2 * N_HEADS, pltpu.roll(mid, N_HEADS, axis=1),
                           jnp.where(lane < 3 * N_HEADS, pltpu.roll(lo, 2 * N_HEADS, axis=1), 0.0))
        return jnp.where(lane < N_HEADS, hi, packed).astype(BF16)

    lf_pack = pack3(logf)
    cs = _dot(tri_ref[...], lf_pack)
    fcum = (cs + pltpu.roll(cs, LANES - N_HEADS, axis=1) + pltpu.roll(cs, LANES - 2 * N_HEADS, axis=1)
            + carry_ref[...])
    carry_ref[...] = fcum[rows - 1:rows, :]

    extra = _dot(pack3(fcum * LOG2E), sel_ref[...]) + one_ref[...]
    in_extra = (lane >= HEAD_DIM) & (lane < HEAD_DIM + N_EXTRA)
    qn2 = _dot((q * q).astype(BF16), go_ref[...])
    kn2 = _dot((k * k).astype(BF16), go_ref[...])
    st_ref[0, 0] = jnp.concatenate(
        [jnp.max(qn2, axis=0, keepdims=True), jnp.max(kn2, axis=0, keepdims=True),
         fcum[0:1, :], fcum[rows - 1:rows, :], jnp.zeros((4, LANES), F32)], axis=0)
    lo_half = lane < HEAD_DIM
    den_one = jnp.where(lane == DEN_LANE, 1.0, 0.0)
    if kv_transposed:
        kt, vt = k.T, v.T
    for h in range(N_HEADS):
        pair = slice(LANES * (h // 2), LANES * (h // 2) + LANES)
        q2, k2, v2 = q[:, pair], k[:, pair], v[:, pair]
        if h % 2:
            q2 = pltpu.roll(q2, HEAD_DIM, axis=1)
            k2 = pltpu.roll(k2, HEAD_DIM, axis=1)
            v2 = pltpu.roll(v2, HEAD_DIM, axis=1)
        per_stream = lambda val: val.reshape(nb, tb, val.shape[-1])
        to_extra = lambda side: jnp.where(in_extra, pltpu.roll(
            extra[:, LANES * side:LANES * (side + 1)], (HEAD_DIM - EXTRA_SLOT * h) % LANES, axis=1), 0.0)
        qa_ref[:, h] = per_stream(jnp.where(lo_half, q2, to_extra(0)).astype(BF16))
        ka_ref[:, h] = per_stream(jnp.where(lo_half, k2, to_extra(1)).astype(BF16))
        va_ref[:, h] = per_stream(jnp.where(lo_half, v2, den_one).astype(BF16))
        if kv_transposed:
            kc_ref[h] = kt[HEAD_DIM * h:HEAD_DIM * (h + 1), :]
            vc_ref[h] = vt[HEAD_DIM * h:HEAD_DIM * (h + 1), :]
        else:
            kc_ref[:, h] = per_stream(k2[:, 0:HEAD_DIM])
            vc_ref[:, h] = per_stream(v2[:, 0:HEAD_DIM])
    lf_t = _dot_nt(eye_ref[...], lf_pack)
    lf_ref[0] = lf_t[0:N_HEADS] + lf_t[N_HEADS:2 * N_HEADS] + lf_t[2 * N_HEADS:3 * N_HEADS]
    for j in range(N_U_SLABS):
        u_ref[0, j] = proj[:, U_COL0 + LANES * j:U_COL0 + LANES * (j + 1)]


def _inproj_consts(nb, tb):
    tri = np.kron(np.eye(nb, dtype=np.float32), np.tril(np.ones((tb, tb), np.float32)))
    sel = np.zeros((LANES, 2 * LANES), np.float32)
    one = np.zeros((1, 2 * LANES), np.float32)
    group_ones = np.zeros((D_ATT, LANES), np.float32)
    for h in range(N_HEADS):
        group_ones[HEAD_DIM * h:HEAD_DIM * (h + 1), h] = 1.0
        for p in range(3):
            sel[p * N_HEADS + h, EXTRA_SLOT * h + p] = 1.0
            one[0, EXTRA_SLOT * h + 3 + p] = 1.0
            one[0, LANES + EXTRA_SLOT * h + p] = 1.0
            sel[p * N_HEADS + h, LANES + EXTRA_SLOT * h + 3 + p] = -1.0
    eye = np.eye(4 * N_HEADS, LANES, dtype=np.float32)
    eye[3 * N_HEADS:] = 0.0
    return (jnp.asarray(tri, BF16), jnp.asarray(sel, BF16), jnp.asarray(one, F32), jnp.asarray(eye, BF16),
            jnp.asarray(group_ones, BF16))


def _mod_spec(layer, row0, chunk, nb=1):
    if nb == 1:
        return pl.BlockSpec((1, 1, D_MODEL), lambda b, i: ((layer * 16 + row0 + b) * 6 + chunk, 0, 0))
    return pl.BlockSpec((None, None, nb, D_MODEL), lambda b, i: (layer, chunk, 0, 0))


def _inproj_call(x, mods, layer, row0, g_pre, w_in_t, b_forget, kv_bufs, kv_depth, nb, tb):
    bsz, t, _ = x.shape
    nblk = t // tb
    rows = nb * tb
    assert nb == 1 or (nblk == 1 and kv_depth is None)
    tri, sel, one, eye, group_ones = _inproj_consts(nb, tb)
    if nb > 1:
        u_shape = (1, N_U_SLABS, bsz * t, LANES)
        u_spec = pl.BlockSpec((1, N_U_SLABS, rows, LANES), lambda b, i: (0, 0, b, 0))
        lf_shape = (1, N_HEADS, bsz * t)
        lf_spec = pl.BlockSpec((1, N_HEADS, rows), lambda b, i: (0, 0, b))
    else:
        u_shape = (bsz, N_U_SLABS, t, LANES)
        u_spec = pl.BlockSpec((1, N_U_SLABS, tb, LANES), lambda b, i: (b, 0, i, 0))
        lf_shape = (bsz, N_HEADS, t)
        lf_spec = pl.BlockSpec((1, N_HEADS, tb), lambda b, i: (b, 0, i))
    const = lambda b, i: (0, 0)
    in_specs = [
        pl.BlockSpec((nb, tb, D_MODEL), lambda b, i: (b, i, 0)),
        _mod_spec(layer, row0, 0, nb), _mod_spec(layer, row0, 1, nb),
        pl.BlockSpec((1, D_MODEL), const),
        pl.BlockSpec((None, W_IN_COLS, D_MODEL), lambda b, i: (layer, 0, 0)),
        pl.BlockSpec((1, GATE_PAD), const),
        pl.BlockSpec((rows, rows), const),
        pl.BlockSpec((LANES, 2 * LANES), const),
        pl.BlockSpec((1, 2 * LANES), const),
        pl.BlockSpec((4 * N_HEADS, LANES), const),
        pl.BlockSpec((D_ATT, LANES), const),
    ]
    args = [x, mods, mods, g_pre, w_in_t, b_forget, tri, sel, one, eye, group_ones]
    n_main = len(args)
    head_blk = pl.BlockSpec((nb, N_HEADS, tb, LANES), lambda b, i: (b, 0, i, 0))
    aliases = {}
    if kv_depth is None:
        kv_shape = jax.ShapeDtypeStruct((bsz, N_HEADS, t, HEAD_DIM), F32)
        kv_spec = pl.BlockSpec((nb, N_HEADS, tb, HEAD_DIM), lambda b, i: (b, 0, i, 0))
    else:
        kv_shape = jax.ShapeDtypeStruct((kv_depth, bsz, N_HEADS, HEAD_DIM, t), F32)
        kv_spec = pl.BlockSpec((None, None, N_HEADS, HEAD_DIM, tb), lambda b, i: (layer, b, 0, 0, i))
        if kv_bufs is not None:
            in_specs += [pl.BlockSpec(memory_space=pl.ANY), pl.BlockSpec(memory_space=pl.ANY)]
            args += list(kv_bufs)
            aliases = {n_main: 3, n_main + 1: 4}
    out_shape = [
        jax.ShapeDtypeStruct((bsz, N_HEADS, t, LANES), BF16),
        jax.ShapeDtypeStruct((bsz, N_HEADS, t, LANES), BF16),
        jax.ShapeDtypeStruct((bsz, N_HEADS, t, LANES), BF16),
        kv_shape, kv_shape,
        jax.ShapeDtypeStruct(lf_shape, F32),
        jax.ShapeDtypeStruct(u_shape, F32),
        jax.ShapeDtypeStruct((bsz // nb, nblk, 8, LANES), F32),
    ]
    out_specs = [
        head_blk, head_blk, head_blk,
        kv_spec, kv_spec,
        lf_spec,
        u_spec,
        pl.BlockSpec((1, 1, 8, LANES), lambda b, i: (b, i, 0, 0)),
    ]
    n_args = len(args)

    def body(*refs):
        _inproj_kernel(*refs[:n_main], *refs[n_args:], nb=nb, tb=tb, kv_transposed=kv_depth is not None)

    return pl.pallas_call(
        body,
        grid=(bsz // nb, nblk),
        in_specs=in_specs,
        out_specs=out_specs,
        out_shape=out_shape,
        scratch_shapes=[pltpu.VMEM((1, LANES), F32)],
        input_output_aliases=aliases,
        compiler_params=pltpu.CompilerParams(
            dimension_semantics=("arbitrary", "arbitrary"), vmem_limit_bytes=VMEM_LIMIT),
        name="inproj",
    )(*args)


def _finish_heads(accs, lane):
    outs = []
    for acc in accs:
        den = jnp.sum(jnp.where(lane == DEN_LANE, acc, 0.0), axis=1, keepdims=True)
        outs.append(acc * (1.0 / den))
    return jnp.where(lane < HEAD_DIM, outs[0], pltpu.roll(outs[1], HEAD_DIM, axis=1)).astype(BF16)


def _attn_kernel(small_ref, first_ref, qa_ref, ka_ref, va_ref, o_ref, *, tq, hps, lps):
    i = pl.program_id(2)
    row = lax.broadcasted_iota(jnp.int32, (tq, tq), 0)
    col = lax.broadcasted_iota(jnp.int32, (tq, tq), 1)
    causal = col <= row
    lane = lax.broadcasted_iota(jnp.int32, (tq, LANES), 1)
    qs = [qa_ref[0, hh] for hh in range(hps)]

    def score(j, hh, masked):
        start = pl.multiple_of(j * tq, tq)
        s = _dot_nt(qs[hh], ka_ref[0, hh, pl.ds(start, tq), :])
        return jnp.where(causal, s, MASK_VALUE) if masked else s

    def values(j, hh):
        return va_ref[0, hh, pl.ds(pl.multiple_of(j * tq, tq), tq), :]

    def store(accs):
        for pr in range(hps // 2):
            o_ref[0, :, LANES * pr:LANES * (pr + 1)] = _finish_heads(accs[2 * pr:2 * pr + 2], lane)

    def tile_plain(j, accs, heads):
        return tuple(acc + _dot(jnp.exp2(score(j, hh, False)).astype(BF16), values(j, hh))
                     for acc, hh in zip(accs, heads))

    def diagonal_plain(accs):
        half = tq // 2
        start = pl.multiple_of(i * tq, tq)
        upper = (lax.broadcasted_iota(jnp.int32, (half, half), 1) <=
                 lax.broadcasted_iota(jnp.int32, (half, half), 0))
        lower = (lax.broadcasted_iota(jnp.int32, (half, tq), 1) <=
                 lax.broadcasted_iota(jnp.int32, (half, tq), 0) + half)
        new = []
        for hh in range(hps):
            k_t = ka_ref[0, hh, pl.ds(start, tq), :]
            v_t = va_ref[0, hh, pl.ds(start, tq), :]
            s_top = jnp.where(upper, _dot_nt(qs[hh][0:half], k_t[0:half]), MASK_VALUE)
            s_bot = jnp.where(lower, _dot_nt(qs[hh][half:tq], k_t), MASK_VALUE)
            o_top = _dot(jnp.exp2(s_top).astype(BF16), v_t[0:half])
            o_bot = _dot(jnp.exp2(s_bot).astype(BF16), v_t)
            new.append(accs[hh] + jnp.concatenate([o_top, o_bot], axis=0))
        return new

    def tile_shifted(j, carry, masked):
        new = []
        for hh in range(hps):
            m, acc = carry[hh]
            s = score(j, hh, masked)
            m_new = jnp.maximum(m, jnp.max(s, axis=1, keepdims=True))
            p = jnp.exp2(s - m_new)
            new.append((m_new, jnp.exp2(m - m_new) * acc + _dot(p.astype(BF16), values(j, hh))))
        return tuple(new)

    small = small_ref[pl.program_id(0), pl.program_id(1)] != 0
    zero = jnp.zeros((tq, LANES), F32)

    @pl.when(small)
    def _():
        accs = []
        for sub in range(hps // lps):
            heads = tuple(range(sub * lps, (sub + 1) * lps))
            first = first_ref[pl.program_id(0), pl.program_id(1) * (hps // lps) + sub, i]
            count = i - first
            two = lambda it, a, first=first, heads=heads: tile_plain(
                first + 2 * it + 1, tile_plain(first + 2 * it, a, heads), heads)
            last = lambda it, a, heads=heads: tile_plain(i - 1, a, heads)
            accs += lax.fori_loop(0, count % 2, last, lax.fori_loop(0, count // 2, two, (zero,) * lps))
        store(diagonal_plain(accs))

    @pl.when(jnp.logical_not(small))
    def _():
        init = ((jnp.full((tq, 1), MASK_VALUE, F32), zero),) * hps
        carry = lax.fori_loop(0, i, functools.partial(tile_shifted, masked=False), init)
        store([c[1] for c in tile_shifted(i, carry, True)])


def _attn_call(qa, ka, va, stats, tq, hps, lps):
    bsz, _, t, _ = qa.shape
    nt, ng, nl = t // tq, N_HEADS // hps, N_HEADS // lps
    assert stats.shape[1] == nt
    bound = jnp.sqrt(jnp.max(stats[:, :, 0, :N_HEADS], axis=1) *
                     jnp.max(stats[:, :, 1, :N_HEADS], axis=1)) * NORM_SLACK
    small = jnp.max(bound.reshape(bsz, ng, hps), axis=-1) < SCORE_LIMIT
    f_first, f_last = stats[:, :, 2, :N_HEADS] * LOG2E, stats[:, :, 3, :N_HEADS] * LOG2E
    top = bound[:, None, None, :] + f_first[:, :, None, :] - f_last[:, None, :, :]
    needed = jnp.max((top >= EXP2_ZERO_BELOW).astype(jnp.int32).reshape(bsz, nt, nt, nl, lps), axis=-1)
    first = jnp.where(jnp.repeat(small, nl // ng, axis=1)[:, :, None],
                      jnp.sum(1 - needed, axis=2).transpose(0, 2, 1), 0)
    whole = pl.BlockSpec((1, hps, t, LANES), lambda b, p, i, *_: (b, p, 0, 0))
    return pl.pallas_call(
        functools.partial(_attn_kernel, tq=tq, hps=hps, lps=lps),
        grid_spec=pltpu.PrefetchScalarGridSpec(
            num_scalar_prefetch=2,
            grid=(bsz, ng, nt),
            in_specs=[pl.BlockSpec((1, hps, tq, LANES), lambda b, p, i, *_: (b, p, i, 0)), whole, whole],
            out_specs=pl.BlockSpec((1, tq, HEAD_DIM * hps), lambda b, p, i, *_: (b, i, p)),
        ),
        out_shape=jax.ShapeDtypeStruct((bsz, t, D_ATT), BF16),
        compiler_params=pltpu.CompilerParams(
            dimension_semantics=("arbitrary", "arbitrary", "arbitrary"),
            vmem_limit_bytes=VMEM_LIMIT),
        name="fox_attn",
    )(small.astype(jnp.int32), first.astype(jnp.int32), qa, ka, va)


def _attn_sample_kernel(qa_ref, ka_ref, va_ref, ckt_ref, cvt_ref, clf_ref, o_ref, *, t, n_past):
    lane_p = lax.broadcasted_iota(jnp.int32, (N_HEADS, n_past), 1)
    c = clf_ref[...]
    total = jnp.sum(c, axis=1, keepdims=True)
    d = 1
    while d < n_past:
        c = c + jnp.where(lane_p >= d, pltpu.roll(c, d, axis=1), 0.0)
        d *= 2
    f_past = (c - total) * LOG2E
    lane = lax.broadcasted_iota(jnp.int32, (t, LANES), 1)
    causal = (lax.broadcasted_iota(jnp.int32, (t, t), 1) <= lax.broadcasted_iota(jnp.int32, (t, t), 0))
    den_rows = jnp.where(lax.broadcasted_iota(jnp.int32, (LANES - HEAD_DIM, n_past), 0) == 0, 1.0, 0.0)
    heads = range(N_HEADS)
    qs = [qa_ref[0, h] for h in heads]
    fqs = [jnp.sum(jnp.where((lane >= HEAD_DIM) & (lane < HEAD_DIM + 3), q.astype(F32), 0.0),
                   axis=1, keepdims=True) for q in qs]
    s_past = [_dot(qs[h][:, 0:HEAD_DIM], ckt_ref[h].astype(BF16)) + fqs[h] - f_past[h:h + 1, :]
              for h in heads]
    s_new = [jnp.where(causal, _dot_nt(qs[h], ka_ref[0, h]), MASK_VALUE) for h in heads]
    ms = [jnp.maximum(jnp.max(s_past[h], axis=1, keepdims=True), jnp.max(s_new[h], axis=1, keepdims=True))
          for h in heads]
    p_past = [jnp.exp2(s_past[h] - ms[h]).astype(BF16) for h in heads]
    p_new = [jnp.exp2(s_new[h] - ms[h]).astype(BF16) for h in heads]
    cvt_aug = [jnp.concatenate([cvt_ref[h], den_rows], axis=0).astype(BF16) for h in heads]
    accs = [_dot_nt(p_past[h], cvt_aug[h]) + _dot(p_new[h], va_ref[0, h]) for h in heads]
    for pr in range(N_HEADS // 2):
        o_ref[0, :, LANES * pr:LANES * (pr + 1)] = _finish_heads(accs[2 * pr:2 * pr + 2], lane)


def _attn_sample_call(qa, ka, va, cache_kt, cache_vt, cache_logf, layer):
    bsz, _, t, _ = qa.shape
    n_past = cache_kt.shape[4]
    cache_spec = pl.BlockSpec((None, None, N_HEADS, HEAD_DIM, n_past), lambda b: (layer, b, 0, 0, 0))
    new_spec = pl.BlockSpec((1, N_HEADS, t, LANES), lambda b: (b, 0, 0, 0))
    return pl.pallas_call(
        functools.partial(_attn_sample_kernel, t=t, n_past=n_past),
        grid=(bsz,),
        in_specs=[
            new_spec, new_spec, new_spec, cache_spec, cache_spec,
            pl.BlockSpec((None, None, N_HEADS, n_past), lambda b: (layer, b, 0, 0)),
        ],
        out_specs=pl.BlockSpec((1, t, D_ATT), lambda b: (b, 0, 0)),
        out_shape=jax.ShapeDtypeStruct((bsz, t, D_ATT), BF16),
        compiler_params=pltpu.CompilerParams(
            dimension_semantics=("arbitrary",), vmem_limit_bytes=VMEM_LIMIT),
        name="fox_attn_sample",
    )(qa, ka, va, cache_kt, cache_vt, cache_logf)


def _s5_table_kernel(ca_ref, cb_ref, ba_ref, bb_ref, prr_ref, pii_ref, dv_ref,
                     toep_ref, bpow_ref, cpow_ref, tt_scr, *, gb):
    row = lax.broadcasted_iota(jnp.int32, (SSM_GROUP, S5_ROWS), 0)
    lane = lax.broadcasted_iota(jnp.int32, (SSM_GROUP, S5_ROWS), 1)
    for gi in range(gb):
        ca, cb, ba, bb = ca_ref[gi], cb_ref[gi], ba_ref[gi], bb_ref[gi]
        power = lambda a, b, n: a * prr_ref[gi, n:n + 1, :] + b * pii_ref[gi, n:n + 1, :]
        xs = [power(ca, cb, n) for n in range(S5_CHUNK + 1)]
        cpow_ref[gi] = jnp.concatenate(xs[1:], axis=0).astype(BF16)
        bpow_ref[gi] = jnp.concatenate(
            [power(ba, bb, S5_CHUNK - 1 - s) for s in range(S5_CHUNK)], axis=0).T.astype(BF16)
        x_hi, x_lo = _split2(jnp.concatenate(xs[:S5_CHUNK], axis=0))
        b_hi, b_lo = _split2(ba)
        r0 = _dot_nt(b_hi, x_hi) + _dot_nt(b_lo, x_hi) + _dot_nt(b_hi, x_lo)
        r0 = r0 + jnp.where(lane == row, dv_ref[gi], 0.0)
        for s in range(S5_CHUNK):
            blk = r0
            if s:
                blk = jnp.where(lane >= SSM_GROUP * s, pltpu.roll(r0, SSM_GROUP * s, axis=1), 0.0)
            tt_scr[SSM_GROUP * s:SSM_GROUP * (s + 1), :] = blk
        toep_ref[gi] = tt_scr[...].T.astype(BF16)


def _s5_tables(a_re, a_im, log_dt, b_re_t, b_im_t, c_re, c_im, d_skip):
    depth = a_re.shape[0]
    dt = jnp.exp(log_dt)[..., None]
    x, y = a_re * dt, a_im * dt
    ex, cy, sy = jnp.exp(x), jnp.cos(y), jnp.sin(y)
    ar, ai = ex * cy, ex * sy
    sh = jnp.sin(0.5 * y)
    nr, ni = jnp.expm1(x) * cy - 2.0 * sh * sh, ai
    den = a_re * a_re + a_im * a_im
    fr = ((nr * a_re + ni * a_im) / den)[:, :, None, :]
    fi = ((ni * a_re - nr * a_im) / den)[:, :, None, :]
    bbr, bbi = fr * b_re_t - fi * b_im_t, fr * b_im_t + fi * b_re_t
    pr, pi = [jnp.ones_like(ar)], [jnp.zeros_like(ar)]
    for _ in range(S5_CHUNK):
        pr, pi = pr + [pr[-1] * ar - pi[-1] * ai], pi + [pr[-1] * ai + pi[-1] * ar]
    mr, mi = pr[S5_CHUNK], pi[S5_CHUNK]
    pr, pi = jnp.stack(pr, axis=2), jnp.stack(pi, axis=2)
    cat = lambda a, b: jnp.concatenate([a, b], axis=-1)
    ins = [cat(c_re, -c_im), cat(-c_im, -c_re), cat(bbr, bbi), cat(-bbi, bbr), cat(pr, pr), cat(pi, pi),
           jnp.pad(d_skip.reshape(depth, N_GROUPS, 1, SSM_GROUP), ((0, 0), (0, 0), (0, 0), (0, S5_ROWS - SSM_GROUP)))]
    gb = TABLE_GROUPS
    spec = lambda rows, cols: pl.BlockSpec((None, gb, rows, cols), lambda l, j: (l, j, 0, 0))
    tab = lambda rows, cols: jax.ShapeDtypeStruct((depth, N_GROUPS, rows, cols), BF16)
    toep, bpow, cpow = pl.pallas_call(
        functools.partial(_s5_table_kernel, gb=gb),
        grid=(depth, N_GROUPS // gb),
        in_specs=[spec(SSM_GROUP, LANES)] * 4 + [spec(S5_CHUNK + 1, LANES)] * 2 + [spec(1, S5_ROWS)],
        out_specs=[spec(S5_ROWS, S5_ROWS), spec(LANES, S5_ROWS), spec(S5_ROWS, LANES)],
        out_shape=[tab(S5_ROWS, S5_ROWS), tab(LANES, S5_ROWS), tab(S5_ROWS, LANES)],
        scratch_shapes=[pltpu.VMEM((S5_ROWS, S5_ROWS), F32)],
        compiler_params=pltpu.CompilerParams(
            dimension_semantics=("arbitrary", "arbitrary"), vmem_limit_bytes=VMEM_LIMIT),
        name="s5_tables",
    )(*ins)
    rows_r, rows_i = [], []
    for _ in range(8):
        rows_r.append(cat(mr, mr))
        rows_i.append(cat(-mi, mi))
        mr, mi = mr * mr - mi * mi, 2.0 * mr * mi
    scm = jnp.stack(rows_r + rows_i, axis=2)
    return toep, bpow, cpow, scm


def _gelu_tanh(y):
    return 0.5 * y * (1.0 + jnp.tanh(math.sqrt(2.0 / math.pi) * (y + 0.044715 * (y * y * y))))


def _s5_kernel(*refs, n, n_real, seglen, fin_row0, use_h0):
    if use_h0:
        u_ref, toep_ref, bpow_ref, cpow_ref, scm_ref, wglu_ref, h0_ref = refs[:7]
        o_ref, hfin_ref, z_scr, zt_scr, h_scr, h0_scr = refs[7:]
    else:
        u_ref, toep_ref, bpow_ref, cpow_ref, scm_ref, wglu_ref = refs[:6]
        o_ref, hfin_ref, z_scr, zt_scr, h_scr = refs[6:]
    gps = LANES // SSM_GROUP
    for pp in range(S5_PAIRS_PER_ITER):
        for part in range(2):
            h_scr[pp, part, 0:SCAN_PAD, :] = jnp.zeros((SCAN_PAD, LANES), F32)
    if use_h0:
        h0_scr[...] = jnp.zeros(h0_scr.shape, F32)

    for k in range(S5_CHUNK):
        for j in range(N_U_SLABS):
            vv = u_ref[0, j, pl.ds(k, n_real, stride=S5_CHUNK), :]
            if n_real < n:
                vv = jnp.concatenate([vv, jnp.zeros((n - n_real, LANES), F32)], axis=0)
            z_scr[gps * j:gps * (j + 1), SSM_GROUP * k:SSM_GROUP * (k + 1), :] = (
                vv.T.reshape(gps, SSM_GROUP, n).astype(BF16))

    chunk_in_seg = lax.broadcasted_iota(jnp.int32, (n, LANES), 0) % seglen
    lo = lax.broadcasted_iota(jnp.int32, (n, LANES), 1) < SSM_STATE
    lo_row = lax.broadcasted_iota(jnp.int32, (1, LANES), 1) < SSM_STATE
    lo_fin = lax.broadcasted_iota(jnp.int32, (FIN_ROWS, LANES), 1) < SSM_STATE
    swap = lambda a: pltpu.roll(a, SSM_STATE, axis=1)

    def pair_step(it, _):
        pairs = range(S5_PAIRS_PER_ITER)
        gss = [[2 * (it * S5_PAIRS_PER_ITER + pp), 2 * (it * S5_PAIRS_PER_ITER + pp) + 1] for pp in pairs]
        flat = [g for gs in gss for g in gs]
        ucts = [z_scr[g] for g in flat]
        yts = [_dot(toep_ref[g], u) for g, u in zip(flat, ucts)]
        hs = [_dot(bpow_ref[g], u).T for g, u in zip(flat, ucts)]
        h0s = [0.0] * len(flat)
        if use_h0:
            for idx, g in enumerate(flat):
                h0_scr[idx, pl.ds(0, h0_ref.shape[1], stride=seglen), :] = h0_ref[g]
                h0s[idx] = h0_scr[idx]
                hs[idx] = hs[idx] + scm_ref[g, 0:1, :] * h0s[idx] + scm_ref[g, 8:9, :] * swap(h0s[idx])
        hre = [jnp.where(lo, hs[2 * pp], swap(hs[2 * pp + 1])) for pp in pairs]
        him = [jnp.where(lo, swap(hs[2 * pp]), hs[2 * pp + 1]) for pp in pairs]

        def shifted(by):
            for pp in pairs:
                h_scr[pp, 0, SCAN_PAD:SCAN_PAD + n, :] = hre[pp]
                h_scr[pp, 1, SCAN_PAD:SCAN_PAD + n, :] = him[pp]
            return ([h_scr[pp, 0, pl.ds(SCAN_PAD - by, n), :] for pp in pairs],
                    [h_scr[pp, 1, pl.ds(SCAN_PAD - by, n), :] for pp in pairs])

        d, j = 1, 0
        while d < seglen:
            sre, sim = shifted(d)
            for pp in pairs:
                gs = gss[pp]
                a, b = sre[pp], sim[pp]
                if seglen < n:
                    a = jnp.where(chunk_in_seg >= d, a, 0.0)
                    b = jnp.where(chunk_in_seg >= d, b, 0.0)
                ar = jnp.where(lo_row, scm_ref[gs[0], j:j + 1, :], scm_ref[gs[1], j:j + 1, :])
                ai = jnp.where(lo_row, -scm_ref[gs[0], 8 + j:9 + j, :], scm_ref[gs[1], 8 + j:9 + j, :])
                hre[pp], him[pp] = hre[pp] + ar * a - ai * b, him[pp] + ar * b + ai * a
            d, j = 2 * d, j + 1
        pre, pim = shifted(1)
        for pp in pairs:
            hps = [jnp.where(lo, pre[pp], swap(pim[pp])), jnp.where(lo, swap(pre[pp]), pim[pp])]
            fre = hre[pp][fin_row0:fin_row0 + FIN_ROWS, :]
            fim = him[pp][fin_row0:fin_row0 + FIN_ROWS, :]
            fins = [jnp.where(lo_fin, fre, swap(fim)), jnp.where(lo_fin, swap(fre), fim)]
            for idx, g in enumerate(gss[pp]):
                hp = hps[idx]
                if seglen < n:
                    hp = jnp.where(chunk_in_seg >= 1, hp, h0s[2 * pp + idx])
                zt_scr[g] = _gelu_tanh(yts[2 * pp + idx] + _dot_nt(cpow_ref[g], hp.astype(BF16)))
                hfin_ref[0, g] = fins[idx]
        return 0

    lax.fori_loop(0, N_GROUPS // (2 * S5_PAIRS_PER_ITER), pair_step, 0)

    for k in range(S5_CHUNK):
        slabs = []
        for j in range(N_U_SLABS):
            w = zt_scr[gps * j:gps * (j + 1), SSM_GROUP * k:SSM_GROUP * (k + 1), :]
            slabs.append(w.reshape(LANES, n).T)
        zk = jnp.concatenate(slabs, axis=1)
        out = zk * _sigmoid(_dot(zk.astype(BF16), wglu_ref[...]))
        for j in range(N_U_SLABS):
            o_ref[0, j, pl.ds(k, n_real, stride=S5_CHUNK), :] = out[0:n_real, LANES * j:LANES * (j + 1)]


def _s5_call(u, toep, bpow, cpow, scm, wglu, h0, layer, n, n_real, seglen, fin_row0):
    nb, _, tt, _ = u.shape
    use_h0 = h0 is not None
    one = pl.Buffered(1)
    tab = lambda rows, cols: pl.BlockSpec((None, N_GROUPS, rows, cols), lambda b: (layer, 0, 0, 0),
                                          pipeline_mode=one)
    in_specs = [
        pl.BlockSpec((1, N_U_SLABS, tt, LANES), lambda b: (b, 0, 0, 0)),
        tab(S5_ROWS, S5_ROWS), tab(LANES, S5_ROWS), tab(S5_ROWS, LANES), tab(16, LANES),
        pl.BlockSpec((None, D_SSM, D_SSM), lambda b: (layer, 0, 0), pipeline_mode=one),
    ]
    args = [u, toep, bpow, cpow, scm, wglu]
    scratch = [
        pltpu.VMEM((N_GROUPS, S5_ROWS, n), BF16),
        pltpu.VMEM((N_GROUPS, S5_ROWS, n), F32),
        pltpu.VMEM((S5_PAIRS_PER_ITER, 2, SCAN_PAD + n, LANES), F32),
    ]
    if use_h0:
        in_specs.append(tab(h0.shape[2], LANES))
        args.append(h0)
        scratch.append(pltpu.VMEM((2 * S5_PAIRS_PER_ITER, n, LANES), F32))
    return pl.pallas_call(
        functools.partial(_s5_kernel, n=n, n_real=n_real, seglen=seglen, fin_row0=fin_row0,
                          use_h0=use_h0),
        grid=(nb,),
        in_specs=in_specs,
        out_specs=[
            pl.BlockSpec((1, N_U_SLABS, tt, LANES), lambda b: (b, 0, 0, 0)),
            pl.BlockSpec((1, N_GROUPS, FIN_ROWS, LANES), lambda b: (b, 0, 0, 0)),
        ],
        out_shape=[
            jax.ShapeDtypeStruct((nb, N_U_SLABS, tt, LANES), F32),
            jax.ShapeDtypeStruct((nb, N_GROUPS, FIN_ROWS, LANES), F32),
        ],
        scratch_shapes=scratch,
        compiler_params=pltpu.CompilerParams(
            dimension_semantics=("arbitrary",), vmem_limit_bytes=VMEM_LIMIT),
        name="s5_mix",
    )(*args)


def _out_ffn_kernel(x_ref, att_ref, ssm_ref, g1_ref, sh2_ref, sc2_ref, g2_ref,
                    gpm_ref, gpf_ref, gpo_ref, wo_ref, wg_ref, wu_ref, wd_ref, o_ref, *, nb, tb):
    rows = nb * tb
    mix = jnp.concatenate(
        [att_ref[...].reshape(rows, D_ATT)] + [ssm_ref[0, j].astype(BF16) for j in range(N_U_SLABS)],
        axis=1)
    o = _dot(mix, wo_ref[...])
    x1 = x_ref[...].reshape(rows, D_MODEL) + _per_row(g1_ref, nb, tb) * _rms(o, gpm_ref[...])
    hf = (_rms(x1, gpf_ref[...]) * (1.0 + _per_row(sc2_ref, nb, tb)) + _per_row(sh2_ref, nb, tb)).astype(BF16)
    gt = _dot(hf, wg_ref[...])
    up = _dot(hf, wu_ref[...])
    hid = (gt * _sigmoid(gt) * up).astype(BF16)
    f = _dot(hid, wd_ref[...])
    o_ref[...] = (x1 + _per_row(g2_ref, nb, tb) * _rms(f, gpo_ref[...])).reshape(nb, tb, D_MODEL)


def _out_ffn_call(x, att, ssm, mods, layer, row0, g_post_mix, g_pre_ffn, g_post_ffn,
                  w_out, w_gate, w_up, w_down, nb, tb):
    bsz, t, _ = x.shape
    nblk = t // tb
    one = pl.Buffered(1)
    const = lambda b, i: (0, 0)
    lay = lambda b, i: (layer, 0, 0)
    if nb > 1:
        ssm_spec = pl.BlockSpec((1, N_U_SLABS, nb * tb, LANES), lambda b, i: (0, 0, b, 0))
    else:
        ssm_spec = pl.BlockSpec((1, N_U_SLABS, tb, LANES), lambda b, i: (b, 0, i, 0))
    row = pl.BlockSpec((1, D_MODEL), const)
    return pl.pallas_call(
        functools.partial(_out_ffn_kernel, nb=nb, tb=tb),
        grid=(bsz // nb, nblk),
        in_specs=[
            pl.BlockSpec((nb, tb, D_MODEL), lambda b, i: (b, i, 0)),
            pl.BlockSpec((nb, tb, D_ATT), lambda b, i: (b, i, 0)),
            ssm_spec,
            _mod_spec(layer, row0, 2, nb), _mod_spec(layer, row0, 3, nb), _mod_spec(layer, row0, 4, nb),
            _mod_spec(layer, row0, 5, nb),
            row, row, row,
            pl.BlockSpec((None, D_MODEL, D_MODEL), lay, pipeline_mode=one),
            pl.BlockSpec((None, D_MODEL, D_FF), lay, pipeline_mode=one),
            pl.BlockSpec((None, D_MODEL, D_FF), lay, pipeline_mode=one),
            pl.BlockSpec((None, D_FF, D_MODEL), lay, pipeline_mode=one),
        ],
        out_specs=pl.BlockSpec((nb, tb, D_MODEL), lambda b, i: (b, i, 0)),
        out_shape=jax.ShapeDtypeStruct((bsz, t, D_MODEL), F32),
        compiler_params=pltpu.CompilerParams(
            dimension_semantics=("arbitrary", "arbitrary"), vmem_limit_bytes=VMEM_LIMIT),
        name="out_ffn",
    )(x, att, ssm, mods, mods, mods, mods, g_post_mix, g_pre_ffn, g_post_ffn,
      w_out, w_gate, w_up, w_down)


def _out_ffn_staggered_kernel(x_ref, att_ref, ssm_ref, g1_ref, sh2_ref, sc2_ref, g2_ref,
                              gpm_ref, gpf_ref, gpo_ref, wo_ref, wg_ref, wu_ref, wd_ref, o_ref,
                              x1_scr, hf_scr, *, nsteps):
    s = pl.program_id(0)
    slot = s % 2

    def front():
        mix = jnp.concatenate([att_ref[0]] + [ssm_ref[0, j].astype(BF16) for j in range(N_U_SLABS)], axis=1)
        x1 = x_ref[0] + g1_ref[0] * _rms(_dot(mix, wo_ref[...]), gpm_ref[...])
        x1_scr[slot] = x1
        hf_scr[slot] = (_rms(x1, gpf_ref[...]) * (1.0 + sc2_ref[0]) + sh2_ref[0]).astype(BF16)

    def back():
        hf = hf_scr[1 - slot]
        gt = _dot(hf, wg_ref[...])
        up = _dot(hf, wu_ref[...])
        f = _dot((gt * _sigmoid(gt) * up).astype(BF16), wd_ref[...])
        o_ref[0] = x1_scr[1 - slot] + g2_ref[0] * _rms(f, gpo_ref[...])

    @pl.when(s == 0)
    def _():
        front()

    @pl.when((s > 0) & (s < nsteps))
    def _():
        back()
        front()

    @pl.when(s == nsteps)
    def _():
        back()


def _out_ffn_staggered_call(x, att, ssm, modr, layer, g_post_mix, g_pre_ffn, g_post_ffn,
                            w_out, w_gate, w_up, w_down, tb):
    bsz, t, _ = x.shape
    nblk = t // tb
    nsteps = bsz * nblk
    one = pl.Buffered(1)
    cur = lambda s: jnp.minimum(s, nsteps - 1)
    prv = lambda s: jnp.maximum(s - 1, 0)
    mod = lambda chunk, blk: pl.BlockSpec(
        (1, 1, D_MODEL), lambda s: ((layer * 16 + blk(s) // nblk) * 6 + chunk, 0, 0))
    row = pl.BlockSpec((1, D_MODEL), lambda s: (0, 0))
    lay = lambda s: (layer, 0, 0)
    return pl.pallas_call(
        functools.partial(_out_ffn_staggered_kernel, nsteps=nsteps),
        grid=(nsteps + 1,),
        in_specs=[
            pl.BlockSpec((1, tb, D_MODEL), lambda s: (cur(s) // nblk, cur(s) % nblk, 0)),
            pl.BlockSpec((1, tb, D_ATT), lambda s: (cur(s) // nblk, cur(s) % nblk, 0)),
            pl.BlockSpec((1, N_U_SLABS, tb, LANES), lambda s: (cur(s) // nblk, 0, cur(s) % nblk, 0)),
            mod(2, cur), mod(3, cur), mod(4, cur), mod(5, prv),
            row, row, row,
            pl.BlockSpec((None, D_MODEL, D_MODEL), lay, pipeline_mode=one),
            pl.BlockSpec((None, D_MODEL, D_FF), lay, pipeline_mode=one),
            pl.BlockSpec((None, D_MODEL, D_FF), lay, pipeline_mode=one),
            pl.BlockSpec((None, D_FF, D_MODEL), lay, pipeline_mode=one),
        ],
        out_specs=pl.BlockSpec((1, tb, D_MODEL), lambda s: (prv(s) // nblk, prv(s) % nblk, 0)),
        out_shape=jax.ShapeDtypeStruct((bsz, t, D_MODEL), F32),
        scratch_shapes=[pltpu.VMEM((2, tb, D_MODEL), F32), pltpu.VMEM((2, tb, D_MODEL), BF16)],
        compiler_params=pltpu.CompilerParams(
            dimension_semantics=("arbitrary",), vmem_limit_bytes=VMEM_LIMIT),
        name="out_ffn_staggered",
    )(x, att, ssm, modr, modr, modr, modr, g_post_mix, g_pre_ffn, g_post_ffn,
      w_out, w_gate, w_up, w_down)


def kernel(x_prompt, x_sample, c_prompt, c_sample, cache_k, cache_v, cache_logf, state_ssm_re,
           state_ssm_im, w_ada, b_ada, g_pre_mix, g_post_mix, g_pre_ffn, g_post_ffn, w_in, b_forget,
           ssm_a_re, ssm_a_im, ssm_log_dt, ssm_b_re, ssm_b_im, ssm_c_re, ssm_c_im, ssm_d, w_glu,
           w_out, w_gate, w_up, w_down):
    depth = w_in.shape[0]
    bp, tp, _ = x_prompt.shape
    bs, ts, _ = x_sample.shape
    n_att = 3 * D_ATT + N_HEADS

    c_all = jnp.concatenate([c_prompt, c_sample, jnp.zeros((16 - bp - bs, D_MODEL), F32)], axis=0)
    mod = _ada_call(c_all, w_ada, b_ada)
    modr = mod.reshape(depth * 16 * 6, 1, D_MODEL)
    mod_s = mod[:, bp:bp + bs].reshape(depth, bs, 6, D_MODEL).transpose(0, 2, 1, 3)

    w_in_t = jnp.swapaxes(w_in, 1, 2)
    w_in_t = jnp.concatenate(
        [w_in_t[:, :n_att], jnp.zeros((depth, GATE_PAD - N_HEADS, D_MODEL), F32), w_in_t[:, n_att:]],
        axis=1).astype(BF16)
    bf_p = jnp.pad(b_forget, ((0, 0), (0, GATE_PAD - N_HEADS)))
    w_glu_b, w_out_b = w_glu.astype(BF16), w_out.astype(BF16)
    w_gate_b, w_up_b, w_down_b = w_gate.astype(BF16), w_up.astype(BF16), w_down.astype(BF16)
    toep, bpow, cpow, scm = _s5_tables(ssm_a_re, ssm_a_im, ssm_log_dt, jnp.swapaxes(ssm_b_re, 2, 3),
                                       jnp.swapaxes(ssm_b_im, 2, 3), ssm_c_re, ssm_c_im, ssm_d)
    cache_kt, cache_vt = jnp.swapaxes(cache_k, 3, 4), jnp.swapaxes(cache_v, 3, 4)
    h0 = jnp.concatenate([state_ssm_re, state_ssm_im], axis=-1).transpose(0, 2, 1, 3)

    n_chunk_p = tp // S5_CHUNK
    seg_s = ts // S5_CHUNK
    yp, ys = x_prompt, x_sample
    kvp = None
    lfp, ssp, kss, vss, lfs, sss = [], [], [], [], [], []
    for l in range(depth):
        qa, ka, va, kbuf, vbuf, lf, u, stats = _inproj_call(
            yp, modr, l, 0, g_pre_mix[l:l + 1], w_in_t, bf_p[l:l + 1], kvp, depth, 1, PROMPT_BLOCK)
        kvp = (kbuf, vbuf)
        lfp.append(lf)
        att = _attn_call(qa, ka, va, stats, PROMPT_BLOCK, ATTN_HEADS_PER_STEP, ATTN_HEADS_PER_LOOP)
        ssm, hfin = _s5_call(u, toep, bpow, cpow, scm, w_glu_b, None, l, n_chunk_p, n_chunk_p,
                             n_chunk_p, n_chunk_p - FIN_ROWS)
        ssp.append(hfin[:, :, FIN_ROWS - 1, :])
        yp = _out_ffn_staggered_call(yp, att, ssm, modr, l, g_post_mix[l:l + 1], g_pre_ffn[l:l + 1],
                                     g_post_ffn[l:l + 1], w_out_b, w_gate_b, w_up_b, w_down_b, PROMPT_BLOCK)

        qa, ka, va, k_s, v_s, lf, u, _ = _inproj_call(
            ys, mod_s, l, bp, g_pre_mix[l:l + 1], w_in_t, bf_p[l:l + 1], None, None, bs, ts)
        kss.append(k_s)
        vss.append(v_s)
        lfs.append(lf.reshape(N_HEADS, bs, ts).transpose(1, 0, 2))
        att = _attn_sample_call(qa, ka, va, cache_kt, cache_vt, cache_logf, l)
        ssm, hfin = _s5_call(u, toep, bpow, cpow, scm, w_glu_b, h0, l, LANES, bs * seg_s, seg_s, 0)
        sss.append(hfin[0, :, seg_s - 1:bs * seg_s:seg_s, :].transpose(1, 0, 2))
        ys = _out_ffn_call(ys, att, ssm, mod_s, l, bp, g_post_mix[l:l + 1], g_pre_ffn[l:l + 1],
                           g_post_ffn[l:l + 1], w_out_b, w_gate_b, w_up_b, w_down_b, bs, ts)

    ssp, sss = jnp.stack(ssp), jnp.stack(sss)
    return (yp, ys, jnp.swapaxes(kvp[0], 3, 4), jnp.swapaxes(kvp[1], 3, 4), jnp.stack(lfp),
            ssp[..., :SSM_STATE], ssp[..., SSM_STATE:],
            jnp.stack(kss), jnp.stack(vss), jnp.stack(lfs),
            sss[..., :SSM_STATE], sss[..., SSM_STATE:])
```
